```python
import jax
import jax.numpy as jnp
from jax import lax
import numpy as np

D_MODEL = 1024
BATCH = 4
SEQ = 8192
DEPTH = 2

GRID_W = 64
CTX_LEN = 256
HEAD_DIM = 64
ROPE_THETA = 10000.0
EPS = 1e-6
NEG_INF = -1e30
Q_BLOCK = 128
POOL_GROUPS = 4
POOL_GROUP_DIM = 64
POOL_DIM = POOL_GROUPS * POOL_GROUP_DIM
POOL_WINDOWS = (2, 4, 8, 16)
B_Q_HEADS = 12
B_KV_HEADS = 4
D_IN_EVEN = POOL_DIM + (B_Q_HEADS + 2 * B_KV_HEADS) * HEAD_DIM
D_MIX_EVEN = POOL_DIM + B_Q_HEADS * HEAD_DIM
C_Q_HEADS = 16
C_KV_HEADS = 4
WINDOW = 128
D_IN_ODD = (C_Q_HEADS + 2 * C_KV_HEADS) * HEAD_DIM
D_MIX_ODD = C_Q_HEADS * HEAD_DIM
PEER_HEADS = 8
PEER_NKEYS = 128
PEER_N_EXPERTS = PEER_NKEYS * PEER_NKEYS
PEER_DKEY = 256
PEER_TOPK = 16
PEER_CHUNK = 128

N_EVEN = (DEPTH + 1) // 2
N_ODD = DEPTH // 2

kernel_name = "hybrid_pool_gqa_swa_peer_dit"


def rmsnorm(x, g):
    x32 = x.astype(jnp.float32)
    y = x32 * lax.rsqrt(jnp.mean(x32 * x32, axis=-1, keepdims=True) + EPS)
    return (y * g.astype(jnp.float32)).astype(x.dtype)


def modulate(h, shift, scale):
    return h * (1.0 + scale) + shift


def rope_2d_tables(rows):
    row = jnp.repeat(jnp.arange(rows, dtype=jnp.int32), GRID_W).astype(jnp.float32)
    col = jnp.tile(jnp.arange(GRID_W, dtype=jnp.int32), rows).astype(jnp.float32)
    half = HEAD_DIM // 2
    inv = 1.0 / (ROPE_THETA ** (jnp.arange(0, half, 2, dtype=jnp.float32) / half))
    ang_r = row[:, None] * inv[None, :]
    ang_c = col[:, None] * inv[None, :]
    return (jnp.cos(ang_r), jnp.sin(ang_r), jnp.cos(ang_c), jnp.sin(ang_c))


def _rotate(x, cos, sin):
    m = cos.shape[-1]
    x1, x2 = x[..., :m], x[..., m:]
    cs = cos[:, None, :].astype(x.dtype)
    sn = sin[:, None, :].astype(x.dtype)
    return jnp.concatenate([x1 * cs - x2 * sn, x1 * sn + x2 * cs], axis=-1)


def apply_rope_2d(x, rope):
    cr, sr, cc, sc = rope
    h = HEAD_DIM // 2
    return jnp.concatenate([_rotate(x[..., :h], cr, sr), _rotate(x[..., h:], cc, sc)], axis=-1)


def split_gqa(q, n_kv):
    B, L, H, D = q.shape
    return q.reshape(B, L, n_kv, H // n_kv, D)


def softmax_sink(s, sink):
    if sink is None:
        return jax.nn.softmax(s, axis=-1)
    sk = sink.astype(jnp.float32)[None, :, :, None, None]
    m = jnp.maximum(jnp.max(s, axis=-1, keepdims=True), sk)
    e = jnp.exp(s - m)
    return e / (jnp.sum(e, axis=-1, keepdims=True) + jnp.exp(sk - m))


def dense_gqa(q, k, v, sink=None):
    s = jnp.einsum('bqkgd,blkd->bkgql', q, k).astype(jnp.float32) * (HEAD_DIM ** -0.5)
    p = softmax_sink(s, sink)
    return jnp.einsum('bkgql,blkd->bqkgd', p.astype(v.dtype), v)


def global_attn_latent(q_lat, k_all, v_all):
    B, S = q_lat.shape[:2]
    nblk = S // Q_BLOCK
    qb = q_lat.reshape(B, nblk, Q_BLOCK, *q_lat.shape[2:]).swapaxes(0, 1)
    out = lax.map(lambda qi: dense_gqa(qi, k_all, v_all), qb)
    return out.swapaxes(0, 1).reshape(B, S, -1)


def window_attn_latent(q_lat, k_lat, v_lat, k_ctx, v_ctx, sink):
    B, S = q_lat.shape[:2]
    n_ctx = k_ctx.shape[1]
    nblk = S // Q_BLOCK
    band = Q_BLOCK + 2 * WINDOW
    pad = ((0, 0), (WINDOW, WINDOW), (0, 0), (0, 0))
    k_pad = jnp.pad(k_lat, pad)
    v_pad = jnp.pad(v_lat, pad)
    qb = q_lat.reshape(B, nblk, Q_BLOCK, *q_lat.shape[2:]).swapaxes(0, 1)
    scale = HEAD_DIM ** -0.5

    def blk(args):
        i, qi = args
        start = i * Q_BLOCK
        kb = lax.dynamic_slice_in_dim(k_pad, start, band, axis=1)
        vb = lax.dynamic_slice_in_dim(v_pad, start, band, axis=1)
        qpos = start + jnp.arange(Q_BLOCK, dtype=jnp.int32)
        kpos = start - WINDOW + jnp.arange(band, dtype=jnp.int32)
        valid = ((kpos[None, :] >= 0) & (kpos[None, :] < S)
                 & (jnp.abs(qpos[:, None] - kpos[None, :]) <= WINDOW))
        s_band = jnp.einsum('bqkgd,blkd->bkgql', qi, kb).astype(jnp.float32) * scale
        s_band = jnp.where(valid, s_band, NEG_INF)
        s_ctx = jnp.einsum('bqkgd,blkd->bkgql', qi, k_ctx).astype(jnp.float32) * scale
        p = softmax_sink(jnp.concatenate([s_ctx, s_band], axis=-1), sink)
        pc = p[..., :n_ctx].astype(v_ctx.dtype)
        pb = p[..., n_ctx:].astype(vb.dtype)
        return (jnp.einsum('bkgql,blkd->bqkgd', pc, v_ctx)
                + jnp.einsum('bkgql,blkd->bqkgd', pb, vb))

    out = lax.map(blk, (jnp.arange(nblk, dtype=jnp.int32), qb))
    return out.swapaxes(0, 1).reshape(B, S, -1)


def pool_mixer(u, pool_w, pool_scale):
    B, L, _ = u.shape
    ug = u.reshape(B, L, POOL_GROUPS, POOL_GROUP_DIM)
    csum = jnp.pad(jnp.cumsum(ug.astype(jnp.float32), axis=1), ((0, 0), (1, 0), (0, 0), (0, 0)))
    t = jnp.arange(L, dtype=jnp.int32)
    means = []
    for g, w in enumerate(POOL_WINDOWS):
        lo = jnp.clip(t - w // 2, 0, L)
        hi = jnp.clip(t + w - w // 2, 0, L)
        cg = csum[:, :, g]
        ssum = jnp.take(cg, hi, axis=1) - jnp.take(cg, lo, axis=1)
        means.append(ssum / (hi - lo).astype(jnp.float32)[None, :, None])
    pooled = jnp.stack(means, axis=2).astype(u.dtype) - ug
    mixed = jnp.einsum('blgc,gcd->blgd', pooled, pool_w).reshape(B, L, POOL_DIM)
    return mixed * pool_scale


def even_mixer(h_lat, h_ctx, w_in, q_g, k_g, pool_w, pool_scale, w_out, rope, need_ctx):
    splits = [POOL_DIM, POOL_DIM + B_Q_HEADS * HEAD_DIM, POOL_DIM + (B_Q_HEADS + B_KV_HEADS) * HEAD_DIM]

    def project(h):
        B, L, _ = h.shape
        u, q, k, v = jnp.split(h @ w_in, splits, axis=-1)
        q = rmsnorm(q.reshape(B, L, B_Q_HEADS, HEAD_DIM), q_g)
        k = rmsnorm(k.reshape(B, L, B_KV_HEADS, HEAD_DIM), k_g)
        v = v.reshape(B, L, B_KV_HEADS, HEAD_DIM)
        return u, q, k, v

    u_l, q_l, k_l, v_l = project(h_lat)
    u_c, q_c, k_c, v_c = project(h_ctx)
    q_l = apply_rope_2d(q_l, rope)
    k_l = apply_rope_2d(k_l, rope)
    k_all = jnp.concatenate([k_c, k_l], axis=1)
    v_all = jnp.concatenate([v_c, v_l], axis=1)
    a_l = global_attn_latent(split_gqa(q_l, B_KV_HEADS), k_all, v_all)
    y_lat = jnp.concatenate([pool_mixer(u_l, pool_w, pool_scale), a_l], axis=-1) @ w_out
    y_ctx = None
    if need_ctx:
        B, Lc = h_ctx.shape[:2]
        a_c = dense_gqa(split_gqa(q_c, B_KV_HEADS), k_c, v_c).reshape(B, Lc, -1)
        y_ctx = jnp.concatenate([pool_mixer(u_c, pool_w, pool_scale), a_c], axis=-1) @ w_out
    return y_lat, y_ctx


def odd_mixer(h_lat, h_ctx, w_in, sink, w_out, rope, need_ctx):
    splits = [C_Q_HEADS * HEAD_DIM, (C_Q_HEADS + C_KV_HEADS) * HEAD_DIM]

    def project(h):
        B, L, _ = h.shape
        q, k, v = jnp.split(h @ w_in, splits, axis=-1)
        return (q.reshape(B, L, C_Q_HEADS, HEAD_DIM), k.reshape(B, L, C_KV_HEADS, HEAD_DIM),
                v.reshape(B, L, C_KV_HEADS, HEAD_DIM))

    q_l, k_l, v_l = project(h_lat)
    q_c, k_c, v_c = project(h_ctx)
    q_l = apply_rope_2d(q_l, rope)
    k_l = apply_rope_2d(k_l, rope)
    sink_g = sink.reshape(C_KV_HEADS, C_Q_HEADS // C_KV_HEADS)
    a_l = window_attn_latent(split_gqa(q_l, C_KV_HEADS), k_l, v_l, k_c, v_c, sink_g)
    y_lat = a_l @ w_out
    y_ctx = None
    if need_ctx:
        B, Lc = h_ctx.shape[:2]
        y_ctx = dense_gqa(split_gqa(q_c, C_KV_HEADS), k_c, v_c, sink_g).reshape(B, Lc, -1) @ w_out
    return y_lat, y_ctx


def peer(h, w_q, keys1, keys2, u_tab, v_tab):
    T, D = h.shape
    half = PEER_DKEY // 2
    hc = h.reshape(T // PEER_CHUNK, PEER_CHUNK, D)

    def chunk(xc):
        q = (xc @ w_q).reshape(PEER_CHUNK, PEER_HEADS, PEER_DKEY).astype(jnp.float32)
        s1 = jnp.einsum('chd,nd->chn', q[..., :half], keys1.astype(jnp.float32))
        s2 = jnp.einsum('chd,nd->chn', q[..., half:], keys2.astype(jnp.float32))
        v1, i1 = lax.top_k(s1, PEER_TOPK)
        v2, i2 = lax.top_k(s2, PEER_TOPK)
        cand = (v1[..., :, None] + v2[..., None, :]).reshape(PEER_CHUNK, PEER_HEADS, PEER_TOPK * PEER_TOPK)
        sv, si = lax.top_k(cand, PEER_TOPK)
        e1 = jnp.take_along_axis(i1, si // PEER_TOPK, axis=-1)
        e2 = jnp.take_along_axis(i2, si % PEER_TOPK, axis=-1)
        idx = e1 * PEER_NKEYS + e2
        g = jax.nn.softmax(sv, axis=-1)
        u = u_tab[idx]
        a = jax.nn.gelu(jnp.einsum('chkd,cd->chk', u, xc).astype(jnp.float32), approximate=False)
        w = (g * a).astype(xc.dtype)
        return jnp.einsum('chk,chkd->cd', w, v_tab[idx])

    return lax.map(chunk, hc).reshape(T, D)


def setup_inputs(seed: int = 0) -> dict:
    key = jax.random.key(seed)
    ks = jax.random.split(key, 24)
    D = D_MODEL
    nrm = jax.random.normal
    f32 = jnp.float32
    return {
        "x": nrm(ks[0], (BATCH, SEQ, D), f32),
        "c": nrm(ks[1], (BATCH, D), f32),
        "ctx": nrm(ks[2], (BATCH, CTX_LEN, D), f32),
        "c_ctx": nrm(ks[3], (D,), f32),
        "ada_w": nrm(ks[4], (DEPTH, D, 6 * D), f32) * (0.5 * D ** -0.5),
        "ada_b": nrm(ks[5], (DEPTH, 6 * D), f32) * 0.02,
        "norm1_g": 1.0 + 0.05 * nrm(ks[6], (DEPTH, D), f32),
        "norm2_g": 1.0 + 0.05 * nrm(ks[7], (DEPTH, D), f32),
        "final_g": 1.0 + 0.05 * nrm(ks[8], (D,), f32),
        "e_w_in": nrm(ks[9], (N_EVEN, D, D_IN_EVEN), f32) * D ** -0.5,
        "e_q_norm_g": 1.0 + 0.05 * nrm(ks[10], (N_EVEN, HEAD_DIM), f32),
        "e_k_norm_g": 1.0 + 0.05 * nrm(ks[11], (N_EVEN, HEAD_DIM), f32),
        "e_pool_w": nrm(ks[12], (N_EVEN, POOL_GROUPS, POOL_GROUP_DIM, POOL_GROUP_DIM), f32) * POOL_GROUP_DIM ** -0.5,
        "e_pool_scale": 1.0 + 0.1 * nrm(ks[13], (N_EVEN, POOL_DIM), f32),
        "e_w_out": nrm(ks[14], (N_EVEN, D_MIX_EVEN, D), f32) * D_MIX_EVEN ** -0.5,
        "o_w_in": nrm(ks[15], (N_ODD, D, D_IN_ODD), f32) * D ** -0.5,
        "o_sink": nrm(ks[16], (N_ODD, C_Q_HEADS), f32) * 0.5,
        "o_w_out": nrm(ks[17], (N_ODD, D_MIX_ODD, D), f32) * D_MIX_ODD ** -0.5,
        "p_w_q": nrm(ks[18], (DEPTH, D, PEER_HEADS * PEER_DKEY), f32) * D ** -0.5,
        "p_keys1": nrm(ks[19], (DEPTH, PEER_NKEYS, PEER_DKEY // 2), f32) * (PEER_DKEY // 2) ** -0.5,
        "p_keys2": nrm(ks[20], (DEPTH, PEER_NKEYS, PEER_DKEY // 2), f32) * (PEER_DKEY // 2) ** -0.5,
        "p_u": nrm(ks[21], (DEPTH, PEER_N_EXPERTS, D), f32) * D ** -0.5,
        "p_v": nrm(ks[22], (DEPTH, PEER_N_EXPERTS, D), f32),
    }


def reference(x, c, ctx, c_ctx, ada_w, ada_b, norm1_g, norm2_g, final_g,
              e_w_in, e_q_norm_g, e_k_norm_g, e_pool_w, e_pool_scale, e_w_out,
              o_w_in, o_sink, o_w_out,
              p_w_q, p_keys1, p_keys2, p_u, p_v):
    B, S, D = x.shape
    ROWS = S // GRID_W
    rope = rope_2d_tables(ROWS)
    silu_c = jax.nn.silu(c)
    silu_cc = jax.nn.silu(c_ctx)
    for i in range(DEPTH):
        need_ctx = i < DEPTH - 1
        mod = silu_c @ ada_w[i] + ada_b[i]
        mod_c = silu_cc @ ada_w[i] + ada_b[i]
        sh1, sc1, g1, sh2, sc2, g2 = jnp.split(mod[:, None, :], 6, axis=-1)
        csh1, csc1, cg1, csh2, csc2, cg2 = jnp.split(mod_c, 6, axis=-1)

        h_lat = modulate(rmsnorm(x, norm1_g[i]), sh1, sc1)
        h_ctx = modulate(rmsnorm(ctx, norm1_g[i]), csh1, csc1)
        if i % 2 == 0:
            j = i // 2
            y_lat, y_ctx = even_mixer(h_lat, h_ctx, e_w_in[j], e_q_norm_g[j], e_k_norm_g[j],
                                      e_pool_w[j], e_pool_scale[j], e_w_out[j], rope, need_ctx)
        else:
            j = i // 2
            y_lat, y_ctx = odd_mixer(h_lat, h_ctx, o_w_in[j], o_sink[j], o_w_out[j], rope, need_ctx)
        x = x + g1 * y_lat
        h2 = modulate(rmsnorm(x, norm2_g[i]), sh2, sc2)
        if need_ctx:
            ctx = ctx + cg1 * y_ctx
            h2c = modulate(rmsnorm(ctx, norm2_g[i]), csh2, csc2)
            tokens = jnp.concatenate([h2.reshape(-1, D), h2c.reshape(-1, D)], axis=0)
            out = peer(tokens, p_w_q[i], p_keys1[i], p_keys2[i], p_u[i], p_v[i])
            x = x + g2 * out[:B * S].reshape(B, S, D)
            ctx = ctx + cg2 * out[B * S:].reshape(B, -1, D)
        else:
            out = peer(h2.reshape(-1, D), p_w_q[i], p_keys1[i], p_keys2[i], p_u[i], p_v[i])
            x = x + g2 * out.reshape(B, S, D)
    return rmsnorm(x, final_g)
```

```python
import functools
import math

import jax
import jax.numpy as jnp
import numpy as np
from jax import lax
from jax.experimental import pallas as pl
from jax.experimental.pallas import tpu as pltpu

F32 = jnp.float32
BF16 = jnp.bfloat16

GRID_W = 64
HEAD_DIM = 64
ROPE_THETA = 10000.0
EPS = 1e-6
MASK_VALUE = -1e30
POOL_GROUPS = 4
POOL_GROUP_DIM = 64
POOL_DIM = POOL_GROUPS * POOL_GROUP_DIM
POOL_WINDOWS = (2, 4, 8, 16)
POOL_HALO = 8
B_Q_HEADS = 12
B_KV_HEADS = 4
C_Q_HEADS = 16
C_KV_HEADS = 4
WINDOW = 128
PEER_HEADS = 8
PEER_NKEYS = 128
PEER_DKEY = 256
PEER_TOPK = 16
LOWEST = -3.0e38

LANES = 128
VMEM_LIMIT_BYTES = 56 * 1024 * 1024
ROW_TILE = 512
ATTN_Q_TILE = 512
ATTN_KV_TILE = 1024
WIN_Q_TILE = 256
WIN_BAND = WIN_Q_TILE + 2 * WINDOW
PEER_TOKENS = 256
PEER_EXPERT_BLOCK = 1024
POOL_CHUNK = 256


def _cparams(semantics):
    return pltpu.CompilerParams(dimension_semantics=semantics, vmem_limit_bytes=VMEM_LIMIT_BYTES)


def _dot(a, b):
    return jnp.dot(a, b, preferred_element_type=F32)


def _dot_nt(a, b):
    return lax.dot_general(a, b, (((1,), (1,)), ((), ())), preferred_element_type=F32)


def _split3(a):
    hi = a.astype(BF16)
    lo = (a - hi.astype(F32)).astype(BF16)
    return hi, lo


def _dot_f32ish(a, b):
    ah, al = _split3(a)
    bh, bl = _split3(b)
    return _dot(ah, bh) + (_dot(ah, bl) + _dot(al, bh))


def _norm_mod(x, gain, shift, scale):
    y = x * lax.rsqrt(jnp.mean(x * x, axis=-1, keepdims=True) + EPS)
    return (y * gain) * (1.0 + scale) + shift


def _mod_kernel(c_ref, w_ref, b_ref, o_ref):
    c = c_ref[...]
    act = c * (1.0 / (1.0 + jnp.exp(-c)))
    o_ref[0] = _dot_f32ish(act, w_ref[0]) + b_ref[0]


def _mod_vectors(cond, ada_w, ada_b):
    depth, d, d6 = ada_w.shape
    g8 = cond.shape[0]
    nblk = d6 // d
    return pl.pallas_call(
        _mod_kernel,
        grid=(depth, nblk),
        in_specs=[
            pl.BlockSpec((g8, d), lambda i, j: (0, 0)),
            pl.BlockSpec((1, d, d), lambda i, j: (i, 0, j)),
            pl.BlockSpec((1, 1, d), lambda i, j: (i, 0, j)),
        ],
        out_specs=pl.BlockSpec((1, g8, d), lambda i, j: (i, 0, j)),
        out_shape=jax.ShapeDtypeStruct((depth, g8, d6), F32),
        compiler_params=_cparams(("arbitrary", "arbitrary")),
        name="mod_vectors",
    )(cond, ada_w, ada_b.reshape(depth, 1, d6))


def _swap16(x):
    lane = lax.broadcasted_iota(jnp.int32, x.shape, 1)
    return jnp.where((lane & 16) == 0, pltpu.roll(x, LANES - 16, 1), pltpu.roll(x, 16, 1))


def _inproj_kernel(x_ref, mod_ref, g_ref, w_ref, cos_ref, sin_ref, *rest, n_pre, n_qk, qk_norm):
    if qk_norm:
        gsum_ref, gexp_ref, hg_ref, *outs = rest
    else:
        outs = rest
    x = x_ref[...]
    h = _norm_mod(x, g_ref[...], mod_ref[0, 0:1, :], mod_ref[0, 1:2, :])
    proj = _dot(h.astype(BF16), w_ref[...])
    qk = proj[:, n_pre:n_pre + n_qk]
    if qk_norm:
        ms = _dot((qk * qk).astype(BF16), gsum_ref[...])
        rinv = lax.rsqrt(ms + EPS)
        r_hi, r_lo = _split3(rinv)
        qk = qk * (_dot(r_hi, gexp_ref[...]) + _dot(r_lo, gexp_ref[...])) * hg_ref[...]
    cos = cos_ref[...]
    sin = sin_ref[...]
    blocks = []
    for j in range(n_qk // LANES):
        blk = qk[:, j * LANES:(j + 1) * LANES]
        blocks.append(blk * cos + _swap16(blk) * sin)
    if n_pre:
        u_ref, q_ref, k_ref, v_ref = outs
        u_ref[...] = proj[:, :n_pre]
    else:
        q_ref, k_ref, v_ref = outs
    n_q = q_ref.shape[1]
    scale = HEAD_DIM ** -0.5
    for j, blk in enumerate(blocks):
        lo = j * LANES
        if lo < n_q:
            q_ref[:, lo:lo + LANES] = (blk * scale).astype(BF16)
        else:
            k_ref[:, lo - n_q:lo - n_q + LANES] = blk.astype(BF16)
    v_ref[...] = proj[:, n_pre + n_qk:].astype(BF16)


def _in_projection(x_all, mod, gain, w_bf16, cos_t, sin_t, seq, n_groups, *, n_pre, n_q, n_kv, norm_consts):
    n, d = x_all.shape
    tm = ROW_TILE
    n_qk = n_q + n_kv
    d_in = w_bf16.shape[1]
    lat_tiles = (n_groups * seq) // tm
    seq_tiles = seq // tm

    def grp(i):
        return (jnp.minimum((i * tm) // seq, n_groups), 0, 0)

    def rope_idx(i):
        return (jnp.where(i < lat_tiles, i % seq_tiles, seq_tiles), 0)

    in_specs = [
        pl.BlockSpec((tm, d), lambda i: (i, 0)),
        pl.BlockSpec((1, 6, d), grp),
        pl.BlockSpec((1, d), lambda i: (0, 0)),
        pl.BlockSpec((d, d_in), lambda i: (0, 0)),
        pl.BlockSpec((tm, LANES), rope_idx),
        pl.BlockSpec((tm, LANES), rope_idx),
    ]
    args = [x_all, mod, gain, w_bf16, cos_t, sin_t]
    if norm_consts is not None:
        gsum, gexp, hg = norm_consts
        in_specs += [
            pl.BlockSpec(gsum.shape, lambda i: (0, 0)),
            pl.BlockSpec(gexp.shape, lambda i: (0, 0)),
            pl.BlockSpec(hg.shape, lambda i: (0, 0)),
        ]
        args += [gsum, gexp, hg]
    out_specs, out_shape = [], []
    if n_pre:
        out_specs.append(pl.BlockSpec((tm, n_pre), lambda i: (i, 0)))
        out_shape.append(jax.ShapeDtypeStruct((n, n_pre), F32))
    for width in (n_q, n_kv, n_kv):
        out_specs.append(pl.BlockSpec((tm, width), lambda i: (i, 0)))
        out_shape.append(jax.ShapeDtypeStruct((n, width), BF16))
    return pl.pallas_call(
        functools.partial(_inproj_kernel, n_pre=n_pre, n_qk=n_qk, qk_norm=norm_consts is not None),
        grid=(n // tm,),
        in_specs=in_specs,
        out_specs=out_specs,
        out_shape=out_shape,
        compiler_params=_cparams(("parallel",)),
        name="in_projection",
    )(*args)


def _head_query(q_ref, head, kv_head):
    blk = q_ref[:, (head // 2) * LANES:(head // 2 + 1) * LANES]
    if head % 2 != kv_head % 2:
        blk = jnp.concatenate([blk[:, HEAD_DIM:], blk[:, :HEAD_DIM]], axis=1)
    lane = lax.broadcasted_iota(jnp.int32, blk.shape, 1)
    keep = (lane >= HEAD_DIM) if kv_head % 2 else (lane < HEAD_DIM)
    return jnp.where(keep, blk, jnp.zeros_like(blk))


def _store_heads(o_ref, outs, kv_of_head):
    for p in range(len(outs) // 2):
        halves = []
        for h in (2 * p, 2 * p + 1):
            half = kv_of_head(h) % 2
            halves.append(outs[h][:, half * HEAD_DIM:(half + 1) * HEAD_DIM])
        o_ref[:, p * LANES:(p + 1) * LANES] = jnp.concatenate(halves, axis=1).astype(o_ref.dtype)


def _global_attn_kernel(q_ref, kc_ref, vc_ref, *rest, n_heads, group, n_lat_chunks):
    if n_lat_chunks:
        kl_ref, vl_ref, o_ref = rest
    else:
        (o_ref,) = rest
    tk = ATTN_KV_TILE
    outs = []
    for h in range(n_heads):
        j = h // group
        cb = slice((j // 2) * LANES, (j // 2 + 1) * LANES)
        qz = _head_query(q_ref, h, j)
        s = _dot_nt(qz, kc_ref[:, cb])
        m = jnp.max(s, axis=-1, keepdims=True)
        p = jnp.exp(s - m)
        l = jnp.sum(p, axis=-1, keepdims=True)
        o = _dot(p.astype(BF16), vc_ref[:, cb])

        if n_lat_chunks:
            def step(c, carry, qz=qz, cb=cb):
                m, l, o = carry
                rows = pl.ds(pl.multiple_of(c * tk, tk), tk)
                s = _dot_nt(qz, kl_ref[rows, cb])
                m_new = jnp.maximum(m, jnp.max(s, axis=-1, keepdims=True))
                alpha = jnp.exp(m - m_new)
                p = jnp.exp(s - m_new)
                l = alpha * l + jnp.sum(p, axis=-1, keepdims=True)
                o = alpha * o + _dot(p.astype(BF16), vl_ref[rows, cb])
                return m_new, l, o

            m, l, o = lax.fori_loop(0, n_lat_chunks, step, (m, l, o))
        outs.append(o / l)
    _store_heads(o_ref, outs, lambda h: h // group)


def _global_attention(q_all, k_all, v_all, batch, seq, ctx_len, *, latent):
    n_q = q_all.shape[1]
    n_kv = k_all.shape[1]
    n_heads = n_q // HEAD_DIM
    group = n_heads // (n_kv // HEAD_DIM)
    ctx_blk0 = (batch * seq) // ctx_len
    if latent:
        tq = ATTN_Q_TILE
        q_tiles = seq // tq
        q_idx = lambda b, i: (b * q_tiles + i, 0)
        out_rows = batch * seq
    else:
        tq = ctx_len
        q_tiles = 1
        q_idx = lambda b, i: (ctx_blk0 + b, 0)
        out_rows = batch * ctx_len
    in_specs = [
        pl.BlockSpec((tq, n_q), q_idx),
        pl.BlockSpec((ctx_len, n_kv), lambda b, i: (ctx_blk0 + b, 0)),
        pl.BlockSpec((ctx_len, n_kv), lambda b, i: (ctx_blk0 + b, 0)),
    ]
    args = [q_all, k_all, v_all]
    if latent:
        in_specs += [pl.BlockSpec((seq, n_kv), lambda b, i: (b, 0)),
                     pl.BlockSpec((seq, n_kv), lambda b, i: (b, 0))]
        args += [k_all, v_all]
    return pl.pallas_call(
        functools.partial(_global_attn_kernel, n_heads=n_heads, group=group,
                          n_lat_chunks=(seq // ATTN_KV_TILE) if latent else 0),
        grid=(batch, q_tiles),
        in_specs=in_specs,
        out_specs=pl.BlockSpec((tq, n_q), lambda b, i: (b * q_tiles + i, 0)),
        out_shape=jax.ShapeDtypeStruct((out_rows, n_q), BF16),
        compiler_params=_cparams(("parallel", "arbitrary")),
        name="global_attention" if latent else "context_attention",
    )(*args)


def _window_attn_kernel(sink_ref, q_ref, kc_ref, vc_ref, kl_ref, vl_ref, o_ref, *, n_heads, group, seq):
    tq = WIN_Q_TILE
    band = WIN_BAND
    start = pl.program_id(1) * tq
    k0 = jnp.clip(start - WINDOW, 0, seq - band)
    k0 = pl.multiple_of(k0, WINDOW)
    rows = pl.ds(k0, band)
    qpos = start + lax.broadcasted_iota(jnp.int32, (tq, band), 0)
    kpos = k0 + lax.broadcasted_iota(jnp.int32, (tq, band), 1)
    valid = jnp.abs(qpos - kpos) <= WINDOW
    outs = []
    for h in range(n_heads):
        j = h // group
        cb = slice((j // 2) * LANES, (j // 2 + 1) * LANES)
        qz = _head_query(q_ref, h, j)
        s_c = _dot_nt(qz, kc_ref[:, cb])
        s_b = jnp.where(valid, _dot_nt(qz, kl_ref[rows, cb]), MASK_VALUE)
        sink = sink_ref[h]
        m = jnp.maximum(jnp.maximum(jnp.max(s_c, axis=-1, keepdims=True),
                                    jnp.max(s_b, axis=-1, keepdims=True)), sink)
        e_c = jnp.exp(s_c - m)
        e_b = jnp.exp(s_b - m)
        denom = (jnp.sum(e_c, axis=-1, keepdims=True) + jnp.sum(e_b, axis=-1, keepdims=True)
                 + jnp.exp(sink - m))
        o = _dot(e_c.astype(BF16), vc_ref[:, cb]) + _dot(e_b.astype(BF16), vl_ref[rows, cb])
        outs.append(o / denom)
    _store_heads(o_ref, outs, lambda h: h // group)


def _window_attention(q_all, k_all, v_all, sink, batch, seq, ctx_len):
    n_q = q_all.shape[1]
    n_kv = k_all.shape[1]
    n_heads = n_q // HEAD_DIM
    group = n_heads // (n_kv // HEAD_DIM)
    tq = WIN_Q_TILE
    q_tiles = seq // tq
    ctx_blk0 = (batch * seq) // ctx_len
    return pl.pallas_call(
        functools.partial(_window_attn_kernel, n_heads=n_heads, group=group, seq=seq),
        grid=(batch, q_tiles),
        in_specs=[
            pl.BlockSpec(memory_space=pltpu.SMEM),
            pl.BlockSpec((tq, n_q), lambda b, i: (b * q_tiles + i, 0)),
            pl.BlockSpec((ctx_len, n_kv), lambda b, i: (ctx_blk0 + b, 0)),
            pl.BlockSpec((ctx_len, n_kv), lambda b, i: (ctx_blk0 + b, 0)),
            pl.BlockSpec((seq, n_kv), lambda b, i: (b, 0)),
            pl.BlockSpec((seq, n_kv), lambda b, i: (b, 0)),
        ],
        out_specs=pl.BlockSpec((tq, n_q), lambda b, i: (b * q_tiles + i, 0)),
        out_shape=jax.ShapeDtypeStruct((batch * seq, n_q), BF16),
        compiler_params=_cparams(("parallel", "arbitrary")),
        name="window_attention",
    )(sink, q_all, k_all, v_all, k_all, v_all)


def _pool_kernel(u_ref, w_ref, sc_ref, o_ref, pad_ref, *, length):
    halo = POOL_HALO
    ch = POOL_CHUNK
    zeros = jnp.zeros((halo, POOL_DIM), F32)
    pad_ref[0:halo, :] = zeros
    pad_ref[halo + length:halo + length + halo, :] = zeros
    pad_ref[halo:halo + length, :] = u_ref[...]
    lane = lax.broadcasted_iota(jnp.int32, (ch, POOL_DIM), 1)
    grp = lane // POOL_GROUP_DIM
    half_w = jnp.left_shift(1, grp)
    row0 = lax.broadcasted_iota(jnp.int32, (ch, POOL_DIM), 0)
    win_rows = ch + 2 * halo

    def chunk(c, carry):
        r0 = pl.multiple_of(c * ch, ch)
        win = pad_ref[pl.ds(r0, win_rows), :]
        acc = jnp.zeros((ch, POOL_DIM), F32)
        for dlt in range(-halo, halo):
            shifted = pltpu.roll(win, win_rows - (halo + dlt), 0)[:ch] if halo + dlt else win[:ch]
            inside = (half_w >= -dlt) if dlt < 0 else (half_w > dlt)
            acc = acc + jnp.where(inside, shifted, 0.0)
        t = row0 + r0
        hi = jnp.minimum(t + half_w, length)
        lo = jnp.maximum(t - half_w, 0)
        centre = win[halo:halo + ch]
        pooled = acc / (hi - lo).astype(F32) - centre
        mixed = _dot(pooled.astype(BF16), w_ref[...]) * sc_ref[...]
        o_ref[pl.ds(r0, ch), :] = mixed.astype(o_ref.dtype)
        return carry

    lax.fori_loop(0, length // ch, chunk, 0)


def _pool_mixer(u_all, w_blockdiag, scale, row0, n_seq, length):
    blk0 = row0 // length
    return pl.pallas_call(
        functools.partial(_pool_kernel, length=length),
        grid=(n_seq,),
        in_specs=[
            pl.BlockSpec((length, POOL_DIM), lambda s: (blk0 + s, 0)),
            pl.BlockSpec((POOL_DIM, POOL_DIM), lambda s: (0, 0)),
            pl.BlockSpec((1, POOL_DIM), lambda s: (0, 0)),
        ],
        out_specs=pl.BlockSpec((length, POOL_DIM), lambda s: (s, 0)),
        out_shape=jax.ShapeDtypeStruct((n_seq * length, POOL_DIM), BF16),
        scratch_shapes=[pltpu.VMEM((length + 2 * POOL_HALO, POOL_DIM), F32)],
        compiler_params=_cparams(("parallel",)),
        name="pool_mixer",
    )(u_all, w_blockdiag, scale)


def _outproj_kernel(x_ref, mod_ref, *rest, n_pool):
    if n_pool:
        m_ref, a_ref, w_ref, o_ref = rest
        y = _dot(m_ref[...], w_ref[:n_pool, :]) + _dot(a_ref[...], w_ref[n_pool:, :])
    else:
        a_ref, w_ref, o_ref = rest
        y = _dot(a_ref[...], w_ref[...])
    o_ref[...] = x_ref[...] + mod_ref[0, 2:3, :] * y


def _out_projection(x_all, mod, mixed, attn, w_bf16, seq, n_groups, n_rows):
    d = x_all.shape[1]
    tm = ROW_TILE
    n_pool = 0 if mixed is None else mixed.shape[1]
    n_att = attn.shape[1]

    def grp(i):
        return (jnp.minimum((i * tm) // seq, n_groups), 0, 0)

    in_specs = [pl.BlockSpec((tm, d), lambda i: (i, 0)), pl.BlockSpec((1, 6, d), grp)]
    args = [x_all, mod]
    if n_pool:
        in_specs.append(pl.BlockSpec((tm, n_pool), lambda i: (i, 0)))
        args.append(mixed)
    in_specs += [pl.BlockSpec((tm, n_att), lambda i: (i, 0)),
                 pl.BlockSpec((n_pool + n_att, d), lambda i: (0, 0))]
    args += [attn, w_bf16]
    return pl.pallas_call(
        functools.partial(_outproj_kernel, n_pool=n_pool),
        grid=(n_rows // tm,),
        in_specs=in_specs,
        out_specs=pl.BlockSpec((tm, d), lambda i: (i, 0)),
        out_shape=jax.ShapeDtypeStruct((n_rows, d), F32),
        compiler_params=_cparams(("parallel",)),
        name="out_projection",
    )(*args)


def _top16_rows(s, v_ref):
    iota = lax.broadcasted_iota(jnp.int32, s.shape, 0)
    for r in range(PEER_TOPK):
        m = jnp.max(s, axis=0, keepdims=True)
        v_ref[r:r + 1, :] = m
        first = jnp.min(jnp.where(s == m, iota, s.shape[0]), axis=0, keepdims=True)
        s = jnp.where(iota == first, LOWEST, s)


def _pair_candidates(v1, v2):
    def rows_lt(x, n):
        i = lax.broadcasted_iota(jnp.int32, x.shape, 0)
        return jnp.where(i < n, x, LOWEST)

    def rows_ge(x, n, upto=None):
        i = lax.broadcasted_iota(jnp.int32, x.shape, 0)
        ok = (i >= n) if upto is None else ((i >= n) & (i < upto))
        return jnp.where(ok, x, LOWEST)

    return jnp.concatenate([
        v1[0:1] + v2[0:16],
        v1[1:2] + v2[0:8],
        rows_lt(v1[2:3] + v2[0:8], 5),
        rows_lt(v1[3:4] + v2[0:8], 4),
        rows_ge(v2[0:1] + v1[0:16], 4),
        rows_ge(v2[1:2] + v1[0:8], 4),
        rows_ge(v2[2:3] + v1[0:8], 4, 5),
    ], axis=0)


def _kth_largest(cand, k):
    tau = jnp.full((1, cand.shape[1]), LOWEST, F32)
    taken = jnp.zeros((1, cand.shape[1]), F32)
    for _ in range(k):
        m = jnp.max(cand, axis=0, keepdims=True)
        eq = cand == m
        cnt = jnp.sum(jnp.where(eq, 1.0, 0.0), axis=0, keepdims=True)
        hit = (taken < k) & (taken + cnt >= k)
        tau = jnp.where(hit, m, tau)
        taken = taken + cnt
        cand = jnp.where(eq, LOWEST, cand)
    return tau


def _peer_kernel(x_ref, mod_ref, g_ref, wq_ref, k1_ref, k2_ref, u_ref, vt_ref, *rest, final_norm):
    if final_norm:
        fg_ref, o_ref, *scratch = rest
    else:
        o_ref, *scratch = rest
    hb_ref, s1_ref, s2_ref, p1_ref, p2_ref, tau_ref, v1_ref, v2_ref, ht_ref, gw_ref, acc_ref = scratch
    e = pl.program_id(1)
    n_blocks = pl.num_programs(1)
    eb = u_ref.shape[0]

    @pl.when(e == 0)
    def _select():
        x = x_ref[...]
        h2 = _norm_mod(x, g_ref[...], mod_ref[0, 3:4, :], mod_ref[0, 4:5, :])
        hb = h2.astype(BF16)
        hb_ref[...] = hb
        acc_ref[...] = jnp.zeros_like(acc_ref)

        def head(h, carry):
            qh = _dot(hb, wq_ref[h])
            half = PEER_DKEY // 2
            s1 = _dot_nt(k1_ref[...], qh[:, :half].astype(BF16))
            s2 = _dot_nt(k2_ref[...], qh[:, half:].astype(BF16))
            _top16_rows(s1, v1_ref)
            _top16_rows(s2, v2_ref)
            v1 = v1_ref[...]
            v2 = v2_ref[...]
            cand = _pair_candidates(v1, v2)
            tau = _kth_largest(cand, PEER_TOPK)
            top = v1[0:1] + v2[0:1]
            z = jnp.sum(jnp.where(cand >= tau, jnp.exp(cand - top), 0.0), axis=0, keepdims=True)
            s1_ref[h] = s1
            s2_ref[h] = s2
            p1_ref[h] = jnp.exp(s1 - v1[0:1]) / z
            p2_ref[h] = jnp.exp(s2 - v2[0:1])
            tau_ref[h] = jnp.broadcast_to(tau, tau_ref.shape[1:])
            return carry

        lax.fori_loop(0, PEER_HEADS, head, 0)

    ht_ref[...] = _dot_nt(u_ref[...], hb_ref[...])
    blocks = eb // PEER_NKEYS
    for i in range(blocks):
        a = e * blocks + i
        w = jnp.zeros((PEER_NKEYS, ht_ref.shape[1]), F32)
        for h in range(PEER_HEADS):
            s1_row = s1_ref[h, pl.ds(a, 1), :]
            p1_row = p1_ref[h, pl.ds(a, 1), :]
            sel = (s2_ref[h] + s1_row) >= tau_ref[h, 0:1, :]
            w = w + jnp.where(sel, p2_ref[h], 0.0) * p1_row
        pre = ht_ref[i * PEER_NKEYS:(i + 1) * PEER_NKEYS, :]
        act = 0.5 * pre * (1.0 + lax.erf(pre * (1.0 / math.sqrt(2.0))))
        gw_ref[i * PEER_NKEYS:(i + 1) * PEER_NKEYS, :] = (w * act).astype(BF16)
    acc_ref[...] += _dot(vt_ref[...], gw_ref[...])

    @pl.when(e == n_blocks - 1)
    def _finish():
        y = x_ref[...] + mod_ref[0, 5:6, :] * acc_ref[...].T
        if final_norm:
            y = y * lax.rsqrt(jnp.mean(y * y, axis=-1, keepdims=True) + EPS) * fg_ref[...]
        o_ref[...] = y


def _peer(x_all, mod, gain, wq_heads, keys1, keys2, u_bf16, vt_bf16, seq, n_groups, n_rows, final_gain):
    d = x_all.shape[1]
    t = PEER_TOKENS
    eb = PEER_EXPERT_BLOCK
    n_exp = u_bf16.shape[0]

    def grp(i, e):
        return (jnp.minimum((i * t) // seq, n_groups), 0, 0)

    in_specs = [
        pl.BlockSpec((t, d), lambda i, e: (i, 0)),
        pl.BlockSpec((1, 6, d), grp),
        pl.BlockSpec((1, d), lambda i, e: (0, 0)),
        pl.BlockSpec(wq_heads.shape, lambda i, e: (0, 0, 0)),
        pl.BlockSpec(keys1.shape, lambda i, e: (0, 0)),
        pl.BlockSpec(keys2.shape, lambda i, e: (0, 0)),
        pl.BlockSpec((eb, d), lambda i, e: (e, 0)),
        pl.BlockSpec((d, eb), lambda i, e: (0, e)),
    ]
    args = [x_all, mod, gain, wq_heads, keys1, keys2, u_bf16, vt_bf16]
    if final_gain is not None:
        in_specs.append(pl.BlockSpec((1, d), lambda i, e: (0, 0)))
        args.append(final_gain)
    per_head = (PEER_HEADS, PEER_NKEYS, t)
    return pl.pallas_call(
        functools.partial(_peer_kernel, final_norm=final_gain is not None),
        grid=(n_rows // t, n_exp // eb),
        in_specs=in_specs,
        out_specs=pl.BlockSpec((t, d), lambda i, e: (i, 0)),
        out_shape=jax.ShapeDtypeStruct((n_rows, d), F32),
        scratch_shapes=[
            pltpu.VMEM((t, d), BF16),
            pltpu.VMEM(per_head, F32),
            pltpu.VMEM(per_head, F32),
            pltpu.VMEM(per_head, F32),
            pltpu.VMEM(per_head, F32),
            pltpu.VMEM((PEER_HEADS, 8, t), F32),
            pltpu.VMEM((PEER_TOPK, t), F32),
            pltpu.VMEM((PEER_TOPK, t), F32),
            pltpu.VMEM((eb, t), F32),
            pltpu.VMEM((eb, t), BF16),
            pltpu.VMEM((d, t), F32),
        ],
        compiler_params=_cparams(("parallel", "arbitrary")),
        name="peer",
    )(*args)


def _rope_tables(seq, pad_rows):
    half = HEAD_DIM // 2
    t = jnp.arange(seq, dtype=jnp.int32)
    row = (t // GRID_W).astype(F32)
    col = (t % GRID_W).astype(F32)
    inv = 1.0 / (ROPE_THETA ** (jnp.arange(0, half, 2, dtype=F32) / half))
    ang_r = row[:, None] * inv[None, :]
    ang_c = col[:, None] * inv[None, :]
    cos_h = jnp.concatenate([jnp.cos(ang_r)] * 2 + [jnp.cos(ang_c)] * 2, axis=1)
    sin_h = jnp.concatenate([-jnp.sin(ang_r), jnp.sin(ang_r), -jnp.sin(ang_c), jnp.sin(ang_c)], axis=1)
    cos_t = jnp.concatenate([jnp.tile(cos_h, (1, LANES // HEAD_DIM)), jnp.ones((pad_rows, LANES), F32)], axis=0)
    sin_t = jnp.concatenate([jnp.tile(sin_h, (1, LANES // HEAD_DIM)), jnp.zeros((pad_rows, LANES), F32)], axis=0)
    return cos_t, sin_t


def _head_norm_consts(q_gain, k_gain, n_q_heads, n_kv_heads):
    n_heads = n_q_heads + n_kv_heads
    head_of_lane = np.arange(n_heads * HEAD_DIM) // HEAD_DIM
    onehot = (head_of_lane[:, None] == np.arange(LANES)[None, :]).astype(np.float32)
    gsum = jnp.asarray(onehot / HEAD_DIM, BF16)
    gexp = jnp.asarray(onehot.T, BF16)
    hg = jnp.concatenate([jnp.tile(q_gain, n_q_heads), jnp.tile(k_gain, n_kv_heads)])[None, :]
    return gsum, gexp, hg


def kernel(x, c, ctx, c_ctx, ada_w, ada_b, norm1_g, norm2_g, final_g, e_w_in, e_q_norm_g, e_k_norm_g,
           e_pool_w, e_pool_scale, e_w_out, o_w_in, o_sink, o_w_out, p_w_q, p_keys1, p_keys2, p_u, p_v):
    batch, seq, d = x.shape
    ctx_len = ctx.shape[1]
    depth = ada_w.shape[0]
    n_lat = batch * seq
    n_all = n_lat + batch * ctx_len

    x_all = jnp.concatenate([x.reshape(n_lat, d), ctx.reshape(batch * ctx_len, d)], axis=0)
    cond = jnp.concatenate([c, c_ctx[None, :], jnp.zeros((8 - (batch + 1) % 8, d), F32)], axis=0)
    mods = _mod_vectors(cond, ada_w, ada_b)
    mods = mods.reshape(depth, cond.shape[0], 6, d)
    cos_t, sin_t = _rope_tables(seq, ROW_TILE)

    for i in range(depth):
        last = i == depth - 1
        mod = mods[i]
        j = i // 2
        g1 = norm1_g[i][None, :]
        g2 = norm2_g[i][None, :]
        if i % 2 == 0:
            consts = _head_norm_consts(e_q_norm_g[j], e_k_norm_g[j], B_Q_HEADS, B_KV_HEADS)
            u, q, k, v = _in_projection(
                x_all, mod, g1, e_w_in[j].astype(BF16), cos_t, sin_t, seq, batch,
                n_pre=POOL_DIM, n_q=B_Q_HEADS * HEAD_DIM, n_kv=B_KV_HEADS * HEAD_DIM, norm_consts=consts)
            w_pool = jax.scipy.linalg.block_diag(*[e_pool_w[j, g] for g in range(POOL_GROUPS)]).astype(BF16)
            p_scale = e_pool_scale[j][None, :]
            attn = _global_attention(q, k, v, batch, seq, ctx_len, latent=True)
            mixed = _pool_mixer(u, w_pool, p_scale, 0, batch, seq)
            if not last:
                attn = jnp.concatenate(
                    [attn, _global_attention(q, k, v, batch, seq, ctx_len, latent=False)], axis=0)
                mixed = jnp.concatenate([mixed, _pool_mixer(u, w_pool, p_scale, n_lat, batch, ctx_len)], axis=0)
            w_out = e_w_out[j].astype(BF16)
        else:
            _q = C_Q_HEADS * HEAD_DIM
            q, k, v = _in_projection(
                x_all, mod, g1, o_w_in[j].astype(BF16), cos_t, sin_t, seq, batch,
                n_pre=0, n_q=_q, n_kv=C_KV_HEADS * HEAD_DIM, norm_consts=None)
            attn = _window_attention(q, k, v, o_sink[j], batch, seq, ctx_len)
            if not last:
                raise NotImplementedError("context stream after an odd layer")
            mixed = None
            w_out = o_w_out[j].astype(BF16)
        n_rows = n_lat if last else n_all
        x_mid = _out_projection(x_all, mod, mixed, attn, w_out, seq, batch, n_rows)
        wq_heads = p_w_q[i].astype(BF16).reshape(d, PEER_HEADS, PEER_DKEY).transpose(1, 0, 2)
        x_all = _peer(x_mid, mod, g2, wq_heads, p_keys1[i].astype(BF16), p_keys2[i].astype(BF16),
                      p_u[i].astype(BF16), p_v[i].astype(BF16).T, seq, batch, n_rows,
                      final_g[None, :] if last else None)
    return x_all[:n_lat].reshape(batch, seq, d)
```

```python
import functools
import math

import jax
import jax.numpy as jnp
import numpy as np
from jax import lax
from jax.experimental import pallas as pl
from jax.experimental.pallas import tpu as pltpu

F32 = jnp.float32
BF16 = jnp.bfloat16

GRID_W = 64
HEAD_DIM = 64
ROPE_THETA = 10000.0
EPS = 1e-6
MASK_VALUE = -1e30
POOL_GROUPS = 4
POOL_GROUP_DIM = 64
POOL_DIM = POOL_GROUPS * POOL_GROUP_DIM
POOL_WINDOWS = (2, 4, 8, 16)
POOL_HALO = 8
B_Q_HEADS = 12
B_KV_HEADS = 4
C_Q_HEADS = 16
C_KV_HEADS = 4
WINDOW = 128
PEER_HEADS = 8
PEER_NKEYS = 128
PEER_DKEY = 256
PEER_TOPK = 16
LOWEST = -3.0e38

LANES = 128
VMEM_LIMIT_BYTES = 56 * 1024 * 1024
ROW_TILE = 512
ATTN_Q_TILE = 512
ATTN_KV_TILE = 1024
WIN_Q_TILE = 256
WIN_BAND = WIN_Q_TILE + 2 * WINDOW
PEER_TOKENS = 512
PEER_EXPERT_BLOCK = 1024
POOL_CHUNK = 256


def _cparams(semantics, flags=None):
    return pltpu.CompilerParams(dimension_semantics=semantics, vmem_limit_bytes=VMEM_LIMIT_BYTES, flags=flags)


def _dot(a, b):
    return jnp.dot(a, b, preferred_element_type=F32)


def _dot_nt(a, b):
    return lax.dot_general(a, b, (((1,), (1,)), ((), ())), preferred_element_type=F32)


def _split3(a):
    hi = a.astype(BF16)
    lo = (a - hi.astype(F32)).astype(BF16)
    return hi, lo


def _dot_f32ish(a, b):
    ah, al = _split3(a)
    bh, bl = _split3(b)
    return _dot(ah, bh) + (_dot(ah, bl) + _dot(al, bh))


def _norm_mod(x, gain, shift, scale):
    y = x * lax.rsqrt(jnp.mean(x * x, axis=-1, keepdims=True) + EPS)
    return (y * gain) * (1.0 + scale) + shift


def _mod_kernel(c_ref, w_ref, b_ref, o_ref):
    c = c_ref[...]
    act = c * (1.0 / (1.0 + jnp.exp(-c)))
    o_ref[0] = _dot_f32ish(act, w_ref[0]) + b_ref[0]


def _mod_vectors(cond, ada_w, ada_b):
    depth, d, d6 = ada_w.shape
    g8 = cond.shape[0]
    nblk = d6 // d
    return pl.pallas_call(
        _mod_kernel,
        grid=(depth, nblk),
        in_specs=[
            pl.BlockSpec((g8, d), lambda i, j: (0, 0)),
            pl.BlockSpec((1, d, d), lambda i, j: (i, 0, j)),
            pl.BlockSpec((1, 1, d), lambda i, j: (i, 0, j)),
        ],
        out_specs=pl.BlockSpec((1, g8, d), lambda i, j: (i, 0, j)),
        out_shape=jax.ShapeDtypeStruct((depth, g8, d6), F32),
        compiler_params=_cparams(("arbitrary", "arbitrary")),
        name="mod_vectors",
    )(cond, ada_w, ada_b.reshape(depth, 1, d6))


def _swap16(x):
    lane = lax.broadcasted_iota(jnp.int32, x.shape, 1)
    return jnp.where((lane & 16) == 0, pltpu.roll(x, LANES - 16, 1), pltpu.roll(x, 16, 1))


def _inproj_kernel(x_ref, mod_ref, g_ref, w_ref, cos_ref, sin_ref, *rest, n_pre, n_qk, qk_norm):
    if qk_norm:
        gsum_ref, gexp_ref, hg_ref, *outs = rest
    else:
        outs = rest
    x = x_ref[...]
    h = _norm_mod(x, g_ref[...], mod_ref[0, 0:1, :], mod_ref[0, 1:2, :])
    proj = _dot(h.astype(BF16), w_ref[...])
    qk = proj[:, n_pre:n_pre + n_qk]
    if qk_norm:
        ms = _dot((qk * qk).astype(BF16), gsum_ref[...])
        rinv = lax.rsqrt(ms + EPS)
        r_hi, r_lo = _split3(rinv)
        qk = qk * (_dot(r_hi, gexp_ref[...]) + _dot(r_lo, gexp_ref[...])) * hg_ref[...]
    cos = cos_ref[...]
    sin = sin_ref[...]
    blocks = []
    for j in range(n_qk // LANES):
        blk = qk[:, j * LANES:(j + 1) * LANES]
        blocks.append(blk * cos + _swap16(blk) * sin)
    if n_pre:
        u_ref, q_ref, k_ref, v_ref = outs
        u_ref[...] = proj[:, :n_pre]
    else:
        q_ref, k_ref, v_ref = outs
    n_q = q_ref.shape[1]
    scale = HEAD_DIM ** -0.5
    for j, blk in enumerate(blocks):
        lo = j * LANES
        if lo < n_q:
            q_ref[:, lo:lo + LANES] = (blk * scale).astype(BF16)
        else:
            k_ref[:, lo - n_q:lo - n_q + LANES] = blk.astype(BF16)
    v_ref[...] = proj[:, n_pre + n_qk:].astype(BF16)


def _in_projection(x_all, mod, gain, w_bf16, cos_t, sin_t, seq, n_groups, *, n_pre, n_q, n_kv, norm_consts):
    n, d = x_all.shape
    tm = ROW_TILE
    n_qk = n_q + n_kv
    d_in = w_bf16.shape[1]
    lat_tiles = (n_groups * seq) // tm
    seq_tiles = seq // tm

    def grp(i):
        return (jnp.minimum((i * tm) // seq, n_groups), 0, 0)

    def rope_idx(i):
        return (jnp.where(i < lat_tiles, i % seq_tiles, seq_tiles), 0)

    in_specs = [
        pl.BlockSpec((tm, d), lambda i: (i, 0)),
        pl.BlockSpec((1, 6, d), grp),
        pl.BlockSpec((1, d), lambda i: (0, 0)),
        pl.BlockSpec((d, d_in), lambda i: (0, 0)),
        pl.BlockSpec((tm, LANES), rope_idx),
        pl.BlockSpec((tm, LANES), rope_idx),
    ]
    args = [x_all, mod, gain, w_bf16, cos_t, sin_t]
    if norm_consts is not None:
        gsum, gexp, hg = norm_consts
        in_specs += [
            pl.BlockSpec(gsum.shape, lambda i: (0, 0)),
            pl.BlockSpec(gexp.shape, lambda i: (0, 0)),
            pl.BlockSpec(hg.shape, lambda i: (0, 0)),
        ]
        args += [gsum, gexp, hg]
    out_specs, out_shape = [], []
    if n_pre:
        out_specs.append(pl.BlockSpec((tm, n_pre), lambda i: (i, 0)))
        out_shape.append(jax.ShapeDtypeStruct((n, n_pre), F32))
    for width in (n_q, n_kv, n_kv):
        out_specs.append(pl.BlockSpec((tm, width), lambda i: (i, 0)))
        out_shape.append(jax.ShapeDtypeStruct((n, width), BF16))
    return pl.pallas_call(
        functools.partial(_inproj_kernel, n_pre=n_pre, n_qk=n_qk, qk_norm=norm_consts is not None),
        grid=(n // tm,),
        in_specs=in_specs,
        out_specs=out_specs,
        out_shape=out_shape,
        compiler_params=_cparams(("parallel",)),
        name="in_projection",
    )(*args)


def _head_query(q_ref, head, kv_head):
    blk = q_ref[:, (head // 2) * LANES:(head // 2 + 1) * LANES]
    if head % 2 != kv_head % 2:
        blk = jnp.concatenate([blk[:, HEAD_DIM:], blk[:, :HEAD_DIM]], axis=1)
    lane = lax.broadcasted_iota(jnp.int32, blk.shape, 1)
    keep = (lane >= HEAD_DIM) if kv_head % 2 else (lane < HEAD_DIM)
    return jnp.where(keep, blk, jnp.zeros_like(blk))


def _store_heads(o_ref, outs, kv_of_head):
    for p in range(len(outs) // 2):
        halves = []
        for h in (2 * p, 2 * p + 1):
            half = kv_of_head(h) % 2
            halves.append(outs[h][:, half * HEAD_DIM:(half + 1) * HEAD_DIM])
        o_ref[:, p * LANES:(p + 1) * LANES] = jnp.concatenate(halves, axis=1).astype(o_ref.dtype)


def _global_attn_kernel(q_ref, kc_ref, vc_ref, *rest, n_heads, group, n_lat_chunks):
    if n_lat_chunks:
        kl_ref, vl_ref, o_ref = rest
    else:
        (o_ref,) = rest
    tk = ATTN_KV_TILE
    outs = []
    for h in range(n_heads):
        j = h // group
        cb = slice((j // 2) * LANES, (j // 2 + 1) * LANES)
        qz = _head_query(q_ref, h, j)
        s = _dot_nt(qz, kc_ref[:, cb])
        m = jnp.max(s, axis=-1, keepdims=True)
        p = jnp.exp(s - m)
        l = jnp.sum(p, axis=-1, keepdims=True)
        o = _dot(p.astype(BF16), vc_ref[:, cb])

        if n_lat_chunks:
            def step(c, carry, qz=qz, cb=cb):
                m, l, o = carry
                rows = pl.ds(pl.multiple_of(c * tk, tk), tk)
                s = _dot_nt(qz, kl_ref[rows, cb])
                m_new = jnp.maximum(m, jnp.max(s, axis=-1, keepdims=True))
                alpha = jnp.exp(m - m_new)
                p = jnp.exp(s - m_new)
                l = alpha * l + jnp.sum(p, axis=-1, keepdims=True)
                o = alpha * o + _dot(p.astype(BF16), vl_ref[rows, cb])
                return m_new, l, o

            m, l, o = lax.fori_loop(0, n_lat_chunks, step, (m, l, o))
        outs.append(o / l)
    _store_heads(o_ref, outs, lambda h: h // group)


def _global_attention(q_all, k_all, v_all, batch, seq, ctx_len, *, latent):
    n_q = q_all.shape[1]
    n_kv = k_all.shape[1]
    n_heads = n_q // HEAD_DIM
    group = n_heads // (n_kv // HEAD_DIM)
    ctx_blk0 = (batch * seq) // ctx_len
    if latent:
        tq = ATTN_Q_TILE
        q_tiles = seq // tq
        q_idx = lambda b, i: (b * q_tiles + i, 0)
        out_rows = batch * seq
    else:
        tq = ctx_len
        q_tiles = 1
        q_idx = lambda b, i: (ctx_blk0 + b, 0)
        out_rows = batch * ctx_len
    in_specs = [
        pl.BlockSpec((tq, n_q), q_idx),
        pl.BlockSpec((ctx_len, n_kv), lambda b, i: (ctx_blk0 + b, 0)),
        pl.BlockSpec((ctx_len, n_kv), lambda b, i: (ctx_blk0 + b, 0)),
    ]
    args = [q_all, k_all, v_all]
    if latent:
        in_specs += [pl.BlockSpec((seq, n_kv), lambda b, i: (b, 0)),
                     pl.BlockSpec((seq, n_kv), lambda b, i: (b, 0))]
        args += [k_all, v_all]
    return pl.pallas_call(
        functools.partial(_global_attn_kernel, n_heads=n_heads, group=group,
                          n_lat_chunks=(seq // ATTN_KV_TILE) if latent else 0),
        grid=(batch, q_tiles),
        in_specs=in_specs,
        out_specs=pl.BlockSpec((tq, n_q), lambda b, i: (b * q_tiles + i, 0)),
        out_shape=jax.ShapeDtypeStruct((out_rows, n_q), BF16),
        compiler_params=_cparams(("parallel", "arbitrary")),
        name="global_attention" if latent else "context_attention",
    )(*args)


def _window_attn_kernel(sink_ref, q_ref, kc_ref, vc_ref, kl_ref, vl_ref, o_ref, *, n_heads, group, seq):
    tq = WIN_Q_TILE
    band = WIN_BAND
    start = pl.program_id(1) * tq
    k0 = jnp.clip(start - WINDOW, 0, seq - band)
    k0 = pl.multiple_of(k0, WINDOW)
    rows = pl.ds(k0, band)
    qpos = start + lax.broadcasted_iota(jnp.int32, (tq, band), 0)
    kpos = k0 + lax.broadcasted_iota(jnp.int32, (tq, band), 1)
    valid = jnp.abs(qpos - kpos) <= WINDOW
    outs = []
    for h in range(n_heads):
        j = h // group
        cb = slice((j // 2) * LANES, (j // 2 + 1) * LANES)
        qz = _head_query(q_ref, h, j)
        s_c = _dot_nt(qz, kc_ref[:, cb])
        s_b = jnp.where(valid, _dot_nt(qz, kl_ref[rows, cb]), MASK_VALUE)
        sink = sink_ref[h]
        m = jnp.maximum(jnp.maximum(jnp.max(s_c, axis=-1, keepdims=True),
                                    jnp.max(s_b, axis=-1, keepdims=True)), sink)
        e_c = jnp.exp(s_c - m)
        e_b = jnp.exp(s_b - m)
        denom = (jnp.sum(e_c, axis=-1, keepdims=True) + jnp.sum(e_b, axis=-1, keepdims=True)
                 + jnp.exp(sink - m))
        o = _dot(e_c.astype(BF16), vc_ref[:, cb]) + _dot(e_b.astype(BF16), vl_ref[rows, cb])
        outs.append(o / denom)
    _store_heads(o_ref, outs, lambda h: h // group)


def _window_attention(q_all, k_all, v_all, sink, batch, seq, ctx_len):
    n_q = q_all.shape[1]
    n_kv = k_all.shape[1]
    n_heads = n_q // HEAD_DIM
    group = n_heads // (n_kv // HEAD_DIM)
    tq = WIN_Q_TILE
    q_tiles = seq // tq
    ctx_blk0 = (batch * seq) // ctx_len
    return pl.pallas_call(
        functools.partial(_window_attn_kernel, n_heads=n_heads, group=group, seq=seq),
        grid=(batch, q_tiles),
        in_specs=[
            pl.BlockSpec(memory_space=pltpu.SMEM),
            pl.BlockSpec((tq, n_q), lambda b, i: (b * q_tiles + i, 0)),
            pl.BlockSpec((ctx_len, n_kv), lambda b, i: (ctx_blk0 + b, 0)),
            pl.BlockSpec((ctx_len, n_kv), lambda b, i: (ctx_blk0 + b, 0)),
            pl.BlockSpec((seq, n_kv), lambda b, i: (b, 0)),
            pl.BlockSpec((seq, n_kv), lambda b, i: (b, 0)),
        ],
        out_specs=pl.BlockSpec((tq, n_q), lambda b, i: (b * q_tiles + i, 0)),
        out_shape=jax.ShapeDtypeStruct((batch * seq, n_q), BF16),
        compiler_params=_cparams(("parallel", "arbitrary")),
        name="window_attention",
    )(sink, q_all, k_all, v_all, k_all, v_all)


def _pool_kernel(u_ref, w_ref, sc_ref, o_ref, pad_ref, *, length):
    halo = POOL_HALO
    ch = POOL_CHUNK
    zeros = jnp.zeros((halo, POOL_DIM), F32)
    pad_ref[0:halo, :] = zeros
    pad_ref[halo + length:halo + length + halo, :] = zeros
    pad_ref[halo:halo + length, :] = u_ref[...]
    lane = lax.broadcasted_iota(jnp.int32, (ch, POOL_DIM), 1)
    grp = lane // POOL_GROUP_DIM
    half_w = jnp.left_shift(1, grp)
    row0 = lax.broadcasted_iota(jnp.int32, (ch, POOL_DIM), 0)
    win_rows = ch + 2 * halo

    def chunk(c, carry):
        r0 = pl.multiple_of(c * ch, ch)
        win = pad_ref[pl.ds(r0, win_rows), :]
        acc = jnp.zeros((ch, POOL_DIM), F32)
        for dlt in range(-halo, halo):
            shifted = pltpu.roll(win, win_rows - (halo + dlt), 0)[:ch] if halo + dlt else win[:ch]
            inside = (half_w >= -dlt) if dlt < 0 else (half_w > dlt)
            acc = acc + jnp.where(inside, shifted, 0.0)
        t = row0 + r0
        hi = jnp.minimum(t + half_w, length)
        lo = jnp.maximum(t - half_w, 0)
        centre = win[halo:halo + ch]
        pooled = acc / (hi - lo).astype(F32) - centre
        mixed = _dot(pooled.astype(BF16), w_ref[...]) * sc_ref[...]
        o_ref[pl.ds(r0, ch), :] = mixed.astype(o_ref.dtype)
        return carry

    lax.fori_loop(0, length // ch, chunk, 0)


def _pool_mixer(u_all, w_blockdiag, scale, row0, n_seq, length):
    blk0 = row0 // length
    return pl.pallas_call(
        functools.partial(_pool_kernel, length=length),
        grid=(n_seq,),
        in_specs=[
            pl.BlockSpec((length, POOL_DIM), lambda s: (blk0 + s, 0)),
            pl.BlockSpec((POOL_DIM, POOL_DIM), lambda s: (0, 0)),
            pl.BlockSpec((1, POOL_DIM), lambda s: (0, 0)),
        ],
        out_specs=pl.BlockSpec((length, POOL_DIM), lambda s: (s, 0)),
        out_shape=jax.ShapeDtypeStruct((n_seq * length, POOL_DIM), BF16),
        scratch_shapes=[pltpu.VMEM((length + 2 * POOL_HALO, POOL_DIM), F32)],
        compiler_params=_cparams(("parallel",)),
        name="pool_mixer",
    )(u_all, w_blockdiag, scale)


def _outproj_kernel(x_ref, mod_ref, *rest, n_pool):
    if n_pool:
        m_ref, a_ref, w_ref, o_ref = rest
        y = _dot(m_ref[...], w_ref[:n_pool, :]) + _dot(a_ref[...], w_ref[n_pool:, :])
    else:
        a_ref, w_ref, o_ref = rest
        y = _dot(a_ref[...], w_ref[...])
    o_ref[...] = x_ref[...] + mod_ref[0, 2:3, :] * y


def _out_projection(x_all, mod, mixed, attn, w_bf16, seq, n_groups, n_rows):
    d = x_all.shape[1]
    tm = ROW_TILE
    n_pool = 0 if mixed is None else mixed.shape[1]
    n_att = attn.shape[1]

    def grp(i):
        return (jnp.minimum((i * tm) // seq, n_groups), 0, 0)

    in_specs = [pl.BlockSpec((tm, d), lambda i: (i, 0)), pl.BlockSpec((1, 6, d), grp)]
    args = [x_all, mod]
    if n_pool:
        in_specs.append(pl.BlockSpec((tm, n_pool), lambda i: (i, 0)))
        args.append(mixed)
    in_specs += [pl.BlockSpec((tm, n_att), lambda i: (i, 0)),
                 pl.BlockSpec((n_pool + n_att, d), lambda i: (0, 0))]
    args += [attn, w_bf16]
    return pl.pallas_call(
        functools.partial(_outproj_kernel, n_pool=n_pool),
        grid=(n_rows // tm,),
        in_specs=in_specs,
        out_specs=pl.BlockSpec((tm, d), lambda i: (i, 0)),
        out_shape=jax.ShapeDtypeStruct((n_rows, d), F32),
        compiler_params=_cparams(("parallel",)),
        name="out_projection",
    )(*args)


def _oddeven_merge(lo, hi, r):
    step = r * 2
    if step < hi - lo:
        yield from _oddeven_merge(lo, hi, step)
        yield from _oddeven_merge(lo + r, hi, step)
        yield from [(i, i + r) for i in range(lo + r, hi - r, step)]
    else:
        yield (lo, lo + r)


def _oddeven_sort(lo, hi):
    if hi - lo >= 1:
        mid = lo + (hi - lo) // 2
        yield from _oddeven_sort(lo, mid)
        yield from _oddeven_sort(mid + 1, hi)
        yield from _oddeven_merge(lo, hi, 1)


_SORT16 = tuple(_oddeven_sort(0, 15))
_SORT8 = tuple(_oddeven_sort(0, 7))
_TOP_PAIRS = tuple((r, c) for r in range(PEER_TOPK) for c in range(PEER_TOPK) if (r + 1) * (c + 1) <= PEER_TOPK)


def _cmpx(xs, i, j):
    a, b = xs[i], xs[j]
    xs[i] = jnp.maximum(a, b)
    xs[j] = jnp.minimum(a, b)


def _sort_desc(xs, net):
    xs = list(xs)
    for i, j in net:
        _cmpx(xs, i, j)
    return xs


def _bitonic_merge_desc(xs):
    xs = list(xs)
    n = len(xs)
    d = n // 2
    while d:
        for i in range(n):
            if not i & d:
                _cmpx(xs, i, i + d)
        d //= 2
    return xs


def _merge_top(a, b):
    n = len(a)
    return _bitonic_merge_desc([jnp.maximum(a[i], b[n - 1 - i]) for i in range(n)])


def _top16_sorted(rows):
    w = _sort_desc(rows, _SORT16)
    for shift in (4, 2, 1):
        w = _merge_top(w, [pltpu.roll(x, shift, 0) for x in w])
    return w


def _pair_threshold(pair):
    low = jnp.full_like(pair[(0, 0)], LOWEST)
    row0 = [pair[(0, c)] for c in range(16)]
    col0 = [pair[(r, 0)] for r in range(1, 16)] + [low]
    t01 = _merge_top(row0, col0)
    g2 = _bitonic_merge_desc([pair[(1, c)] for c in range(1, 8)] + [low] * 3
                             + [pair[(r, 1)] for r in range(7, 1, -1)])
    rest = [pair[k] for k in ((2, 2), (2, 3), (2, 4), (3, 2), (4, 2), (3, 3))]
    g3 = _sort_desc(rest + [low, low], _SORT8) + [low] * 8
    t23 = _merge_top(g2, g3)
    tau = None
    for i in range(16):
        m = jnp.maximum(t01[i], t23[15 - i])
        tau = m if tau is None else jnp.minimum(tau, m)
    return tau


def _route_column(s1, s2):
    assert PEER_TOPK == 16 and PEER_NKEYS == 128
    rows1 = [s1[8 * g:8 * g + 8] for g in range(16)]
    rows2 = [s2[8 * g:8 * g + 8] for g in range(16)]
    v1 = _top16_sorted(rows1)
    v2 = _top16_sorted(rows2)
    pair = {(r, c): v1[r] + v2[c] for (r, c) in _TOP_PAIRS}
    tau = _pair_threshold(pair)
    e1 = [jnp.exp(v - v1[0]) for v in v1]
    e2 = [jnp.exp(v - v2[0]) for v in v2]
    z = None
    for (r, c) in _TOP_PAIRS:
        term = jnp.where(pair[(r, c)] >= tau, e1[r] * e2[c], 0.0)
        z = term if z is None else z + term
    inv_z = 1.0 / z
    n_hi = None
    for c in range(8, 16):
        one = jnp.where(pair[(0, c)] >= tau, 1.0, 0.0)
        n_hi = one if n_hi is None else n_hi + one
    cnt, p1n, rank2, p2 = [], [], [], []
    for g in range(16):
        a = rows1[g]
        n = jnp.where(a >= v1[0], n_hi, 0.0)
        for c in range(8):
            n = n + jnp.where(a + v2[c] >= tau, 1.0, 0.0)
        cnt.append(n)
        p1n.append(jnp.exp(a - v1[0]) * inv_z)
        b = rows2[g]
        k = jnp.where(v2[0] > b, 1.0, 0.0)
        for r in range(1, 16):
            k = k + jnp.where(v2[r] > b, 1.0, 0.0)
        rank2.append(k)
        p2.append(jnp.exp(b - v2[0]))
    cat = lambda xs: jnp.concatenate(xs, axis=0)
    return cat(cnt), cat(p1n), cat(rank2), cat(p2)


def _peer_kernel(x_ref, mod_ref, g_ref, wq_ref, k1_ref, k2_ref, u_ref, vt_ref, *rest, final_norm, n_blocks):
    if final_norm:
        fg_ref, o_ref, *scratch = rest
    else:
        o_ref, *scratch = rest
    hb_ref, cnt_ref, p1_ref, rank_ref, p2_ref, ht_ref, gw0_ref, gw1_ref, acc_ref = scratch
    s = pl.program_id(1)
    eb, t = ht_ref.shape
    blocks = eb // PEER_NKEYS
    half = PEER_DKEY // 2

    @pl.when(s == 0)
    def _route():
        x = x_ref[...]
        h2 = _norm_mod(x, g_ref[...], mod_ref[0, 3:4, :], mod_ref[0, 4:5, :])
        hb_ref[...] = h2.astype(BF16)
        acc_ref[...] = jnp.zeros_like(acc_ref)

        def head(h, carry):
            qh = _dot(hb_ref[...], wq_ref[h])
            s1 = _dot_nt(k1_ref[...], qh[:, :half].astype(BF16))
            s2 = _dot_nt(k2_ref[...], qh[:, half:].astype(BF16))
            for j in range(t // LANES):
                col = slice(j * LANES, (j + 1) * LANES)
                cnt, p1n, rank2, p2 = _route_column(s1[:, col], s2[:, col])
                cnt_ref[h, :, col] = cnt
                p1_ref[h, :, col] = p1n
                rank_ref[h, :, col] = rank2.astype(BF16)
                p2_ref[h, :, col] = p2.astype(BF16)
            return carry

        lax.fori_loop(0, PEER_HEADS, head, 0)

    def stage(parity, build, consume):
        gw_w, gw_r = (gw0_ref, gw1_ref) if parity == 0 else (gw1_ref, gw0_ref)
        if build:
            ht_ref[...] = _dot_nt(u_ref[...], hb_ref[...])
            reps = PEER_NKEYS // 16
            for i in range(blocks):
                a = s * blocks + i
                w = None
                for h in range(PEER_HEADS):
                    cnt16 = jnp.broadcast_to(cnt_ref[h, pl.ds(a, 1), :], (16, t)).astype(BF16)
                    p16 = jnp.broadcast_to(p1_ref[h, pl.ds(a, 1), :], (16, t)).astype(BF16)
                    sel = rank_ref[h] < jnp.concatenate([cnt16] * reps, axis=0)
                    term = (jnp.where(sel, p2_ref[h], jnp.zeros((), BF16))
                            * jnp.concatenate([p16] * reps, axis=0))
                    w = term if w is None else w + term
                pre = ht_ref[i * PEER_NKEYS:(i + 1) * PEER_NKEYS, :]
                act = 0.5 * pre * (1.0 + lax.erf(pre * (1.0 / math.sqrt(2.0))))
                gw_w[i * PEER_NKEYS:(i + 1) * PEER_NKEYS, :] = act.astype(BF16) * w
        if consume:
            acc_ref[...] += _dot(vt_ref[...], gw_r[...])

    pl.when(s == 0)(functools.partial(stage, 0, True, False))
    steady = (s >= 1) & (s < n_blocks)
    pl.when(steady & (s % 2 == 0))(functools.partial(stage, 0, True, True))
    pl.when(steady & (s % 2 == 1))(functools.partial(stage, 1, True, True))
    pl.when(s == n_blocks)(functools.partial(stage, n_blocks % 2, False, True))

    @pl.when(s == n_blocks)
    def _finish():
        y = x_ref[...] + mod_ref[0, 5:6, :] * acc_ref[...].T
        if final_norm:
            y = y * lax.rsqrt(jnp.mean(y * y, axis=-1, keepdims=True) + EPS) * fg_ref[...]
        o_ref[...] = y


def _peer(x_all, mod, gain, wq_heads, keys1, keys2, u_bf16, vt_bf16, seq, n_groups, n_rows, final_gain):
    d = x_all.shape[1]
    t = PEER_TOKENS
    eb = PEER_EXPERT_BLOCK
    n_blocks = u_bf16.shape[0] // eb

    def grp(i, e):
        return (jnp.minimum((i * t) // seq, n_groups), 0, 0)

    in_specs = [
        pl.BlockSpec((t, d), lambda i, e: (i, 0)),
        pl.BlockSpec((1, 6, d), grp),
        pl.BlockSpec((1, d), lambda i, e: (0, 0)),
        pl.BlockSpec(wq_heads.shape, lambda i, e: (0, 0, 0)),
        pl.BlockSpec(keys1.shape, lambda i, e: (0, 0)),
        pl.BlockSpec(keys2.shape, lambda i, e: (0, 0)),
        pl.BlockSpec((eb, d), lambda i, e: (jnp.minimum(e, n_blocks - 1), 0)),
        pl.BlockSpec((d, eb), lambda i, e: (0, jnp.maximum(e - 1, 0))),
    ]
    args = [x_all, mod, gain, wq_heads, keys1, keys2, u_bf16, vt_bf16]
    if final_gain is not None:
        in_specs.append(pl.BlockSpec((1, d), lambda i, e: (0, 0)))
        args.append(final_gain)
    per_head = (PEER_HEADS, PEER_NKEYS, t)
    return pl.pallas_call(
        functools.partial(_peer_kernel, final_norm=final_gain is not None, n_blocks=n_blocks),
        grid=(n_rows // t, n_blocks + 1),
        in_specs=in_specs,
        out_specs=pl.BlockSpec((t, d), lambda i, e: (i, 0)),
        out_shape=jax.ShapeDtypeStruct((n_rows, d), F32),
        scratch_shapes=[
            pltpu.VMEM((t, d), BF16),
            pltpu.VMEM(per_head, F32),
            pltpu.VMEM(per_head, F32),
            pltpu.VMEM(per_head, BF16),
            pltpu.VMEM(per_head, BF16),
            pltpu.VMEM((eb, t), F32),
            pltpu.VMEM((eb, t), BF16),
            pltpu.VMEM((eb, t), BF16),
            pltpu.VMEM((d, t), F32),
        ],
        compiler_params=_cparams(("parallel", "arbitrary")),
        name="peer",
    )(*args)


def _rope_tables(seq, pad_rows):
    half = HEAD_DIM // 2
    t = jnp.arange(seq, dtype=jnp.int32)
    row = (t // GRID_W).astype(F32)
    col = (t % GRID_W).astype(F32)
    inv = 1.0 / (ROPE_THETA ** (jnp.arange(0, half, 2, dtype=F32) / half))
    ang_r = row[:, None] * inv[None, :]
    ang_c = col[:, None] * inv[None, :]
    cos_h = jnp.concatenate([jnp.cos(ang_r)] * 2 + [jnp.cos(ang_c)] * 2, axis=1)
    sin_h = jnp.concatenate([-jnp.sin(ang_r), jnp.sin(ang_r), -jnp.sin(ang_c), jnp.sin(ang_c)], axis=1)
    cos_t = jnp.concatenate([jnp.tile(cos_h, (1, LANES // HEAD_DIM)), jnp.ones((pad_rows, LANES), F32)], axis=0)
    sin_t = jnp.concatenate([jnp.tile(sin_h, (1, LANES // HEAD_DIM)), jnp.zeros((pad_rows, LANES), F32)], axis=0)
    return cos_t, sin_t


def _head_norm_consts(q_gain, k_gain, n_q_heads, n_kv_heads):
    n_heads = n_q_heads + n_kv_heads
    head_of_lane = np.arange(n_heads * HEAD_DIM) // HEAD_DIM
    onehot = (head_of_lane[:, None] == np.arange(LANES)[None, :]).astype(np.float32)
    gsum = jnp.asarray(onehot / HEAD_DIM, BF16)
    gexp = jnp.asarray(onehot.T, BF16)
    hg = jnp.concatenate([jnp.tile(q_gain, n_q_heads), jnp.tile(k_gain, n_kv_heads)])[None, :]
    return gsum, gexp, hg


def kernel(x, c, ctx, c_ctx, ada_w, ada_b, norm1_g, norm2_g, final_g, e_w_in, e_q_norm_g, e_k_norm_g,
           e_pool_w, e_pool_scale, e_w_out, o_w_in, o_sink, o_w_out, p_w_q, p_keys1, p_keys2, p_u, p_v):
    batch, seq, d = x.shape
    ctx_len = ctx.shape[1]
    depth = ada_w.shape[0]
    n_lat = batch * seq
    n_all = n_lat + batch * ctx_len

    x_all = jnp.concatenate([x.reshape(n_lat, d), ctx.reshape(batch * ctx_len, d)], axis=0)
    cond = jnp.concatenate([c, c_ctx[None, :], jnp.zeros((8 - (batch + 1) % 8, d), F32)], axis=0)
    mods = _mod_vectors(cond, ada_w, ada_b)
    mods = mods.reshape(depth, cond.shape[0], 6, d)
    cos_t, sin_t = _rope_tables(seq, ROW_TILE)

    for i in range(depth):
        last = i == depth - 1
        mod = mods[i]
        j = i // 2
        g1 = norm1_g[i][None, :]
        g2 = norm2_g[i][None, :]
        if i % 2 == 0:
            consts = _head_norm_consts(e_q_norm_g[j], e_k_norm_g[j], B_Q_HEADS, B_KV_HEADS)
            u, q, k, v = _in_projection(
                x_all, mod, g1, e_w_in[j].astype(BF16), cos_t, sin_t, seq, batch,
                n_pre=POOL_DIM, n_q=B_Q_HEADS * HEAD_DIM, n_kv=B_KV_HEADS * HEAD_DIM, norm_consts=consts)
            w_pool = jax.scipy.linalg.block_diag(*[e_pool_w[j, g] for g in range(POOL_GROUPS)]).astype(BF16)
            p_scale = e_pool_scale[j][None, :]
            attn = _global_attention(q, k, v, batch, seq, ctx_len, latent=True)
            mixed = _pool_mixer(u, w_pool, p_scale, 0, batch, seq)
            if not last:
                attn = jnp.concatenate(
                    [attn, _global_attention(q, k, v, batch, seq, ctx_len, latent=False)], axis=0)
                mixed = jnp.concatenate([mixed, _pool_mixer(u, w_pool, p_scale, n_lat, batch, ctx_len)], axis=0)
            w_out = e_w_out[j].astype(BF16)
        else:
            _q = C_Q_HEADS * HEAD_DIM
            q, k, v = _in_projection(
                x_all, mod, g1, o_w_in[j].astype(BF16), cos_t, sin_t, seq, batch,
                n_pre=0, n_q=_q, n_kv=C_KV_HEADS * HEAD_DIM, norm_consts=None)
            attn = _window_attention(q, k, v, o_sink[j], batch, seq, ctx_len)
            if not last:
                raise NotImplementedError("context stream after an odd layer")
            mixed = None
            w_out = o_w_out[j].astype(BF16)
        n_rows = n_lat if last else n_all
        x_mid = _out_projection(x_all, mod, mixed, attn, w_out, seq, batch, n_rows)
        wq_heads = p_w_q[i].astype(BF16).reshape(d, PEER_HEADS, PEER_DKEY).transpose(1, 0, 2)
        x_all = _peer(x_mid, mod, g2, wq_heads, p_keys1[i].astype(BF16), p_keys2[i].astype(BF16),
                      p_u[i].astype(BF16), p_v[i].astype(BF16).T, seq, batch, n_rows,
                      final_g[None, :] if last else None)
    return x_all[:n_lat].reshape(batch, seq, d)
```

```python
import functools
import math

import jax
import jax.numpy as jnp
import numpy as np
from jax import lax
from jax.experimental import pallas as pl
from jax.experimental.pallas import tpu as pltpu

F32 = jnp.float32
BF16 = jnp.bfloat16

GRID_W = 64
HEAD_DIM = 64
ROPE_THETA = 10000.0
EPS = 1e-6
MASK_VALUE = -1e30
POOL_GROUPS = 4
POOL_GROUP_DIM = 64
POOL_DIM = POOL_GROUPS * POOL_GROUP_DIM
POOL_WINDOWS = (2, 4, 8, 16)
POOL_HALO = 8
B_Q_HEADS = 12
B_KV_HEADS = 4
C_Q_HEADS = 16
C_KV_HEADS = 4
WINDOW = 128
PEER_HEADS = 8
PEER_NKEYS = 128
PEER_DKEY = 256
PEER_TOPK = 16
LOWEST = -3.0e38
SAFE_LOGIT_BOUND = 40.0
NORM_BOUND_MARGIN = 1.05

LANES = 128
VMEM_LIMIT_BYTES = 56 * 1024 * 1024
ROW_TILE = 512
ATTN_Q_TILE = 512
ATTN_KV_TILE = 1024
WIN_Q_TILE = 256
WIN_BAND = WIN_Q_TILE + 2 * WINDOW
PEER_TOKENS = 512
PEER_EXPERT_BLOCK = 1024
POOL_CHUNK = 256


def _cparams(semantics, flags=None):
    return pltpu.CompilerParams(dimension_semantics=semantics, vmem_limit_bytes=VMEM_LIMIT_BYTES, flags=flags)


def _dot(a, b):
    return jnp.dot(a, b, preferred_element_type=F32)


def _dot_nt(a, b):
    return lax.dot_general(a, b, (((1,), (1,)), ((), ())), preferred_element_type=F32)


def _split3(a):
    hi = a.astype(BF16)
    lo = (a - hi.astype(F32)).astype(BF16)
    return hi, lo


def _dot_f32ish(a, b):
    ah, al = _split3(a)
    bh, bl = _split3(b)
    return _dot(ah, bh) + (_dot(ah, bl) + _dot(al, bh))


def _norm_mod(x, gain, shift, scale):
    y = x * lax.rsqrt(jnp.mean(x * x, axis=-1, keepdims=True) + EPS)
    return (y * gain) * (1.0 + scale) + shift


def _mod_kernel(c_ref, w_ref, b_ref, o_ref):
    c = c_ref[...]
    act = c * (1.0 / (1.0 + jnp.exp(-c)))
    o_ref[0] = _dot_f32ish(act, w_ref[0]) + b_ref[0]


def _mod_vectors(cond, ada_w, ada_b):
    depth, d, d6 = ada_w.shape
    g8 = cond.shape[0]
    nblk = d6 // d
    return pl.pallas_call(
        _mod_kernel,
        grid=(depth, nblk),
        in_specs=[
            pl.BlockSpec((g8, d), lambda i, j: (0, 0)),
            pl.BlockSpec((1, d, d), lambda i, j: (i, 0, j)),
            pl.BlockSpec((1, 1, d), lambda i, j: (i, 0, j)),
        ],
        out_specs=pl.BlockSpec((1, g8, d), lambda i, j: (i, 0, j)),
        out_shape=jax.ShapeDtypeStruct((depth, g8, d6), F32),
        compiler_params=_cparams(("arbitrary", "arbitrary")),
        name="mod_vectors",
    )(cond, ada_w, ada_b.reshape(depth, 1, d6))


def _swap16(x):
    lane = lax.broadcasted_iota(jnp.int32, x.shape, 1)
    return jnp.where((lane & 16) == 0, pltpu.roll(x, LANES - 16, 1), pltpu.roll(x, 16, 1))


def _inproj_kernel(x_ref, mod_ref, g_ref, w_ref, cos_ref, sin_ref, *rest, n_pre, n_qk, qk_norm):
    if qk_norm:
        gsum_ref, gexp_ref, hg_ref, *outs = rest
    else:
        outs = rest
    x = x_ref[...]
    h = _norm_mod(x, g_ref[...], mod_ref[0, 0:1, :], mod_ref[0, 1:2, :])
    proj = _dot(h.astype(BF16), w_ref[...])
    qk = proj[:, n_pre:n_pre + n_qk]
    if qk_norm:
        ms = _dot((qk * qk).astype(BF16), gsum_ref[...])
        rinv = lax.rsqrt(ms + EPS)
        r_hi, r_lo = _split3(rinv)
        qk = qk * (_dot(r_hi, gexp_ref[...]) + _dot(r_lo, gexp_ref[...])) * hg_ref[...]
    cos = cos_ref[...]
    sin = sin_ref[...]
    blocks = []
    for j in range(n_qk // LANES):
        blk = qk[:, j * LANES:(j + 1) * LANES]
        blocks.append(blk * cos + _swap16(blk) * sin)
    if n_pre:
        u_ref, q_ref, k_ref, v_ref = outs
        u_ref[...] = proj[:, :n_pre]
    else:
        q_ref, k_ref, v_ref = outs
    n_q = q_ref.shape[1]
    scale = HEAD_DIM ** -0.5
    for j, blk in enumerate(blocks):
        lo = j * LANES
        if lo < n_q:
            q_ref[:, lo:lo + LANES] = (blk * scale).astype(BF16)
        else:
            k_ref[:, lo - n_q:lo - n_q + LANES] = blk.astype(BF16)
    v_ref[...] = proj[:, n_pre + n_qk:].astype(BF16)


def _in_projection(x_all, mod, gain, w_bf16, cos_t, sin_t, seq, n_groups, *, n_pre, n_q, n_kv, norm_consts):
    n, d = x_all.shape
    tm = ROW_TILE
    n_qk = n_q + n_kv
    d_in = w_bf16.shape[1]
    lat_tiles = (n_groups * seq) // tm
    seq_tiles = seq // tm

    def grp(i):
        return (jnp.minimum((i * tm) // seq, n_groups), 0, 0)

    def rope_idx(i):
        return (jnp.where(i < lat_tiles, i % seq_tiles, seq_tiles), 0)

    in_specs = [
        pl.BlockSpec((tm, d), lambda i: (i, 0)),
        pl.BlockSpec((1, 6, d), grp),
        pl.BlockSpec((1, d), lambda i: (0, 0)),
        pl.BlockSpec((d, d_in), lambda i: (0, 0)),
        pl.BlockSpec((tm, LANES), rope_idx),
        pl.BlockSpec((tm, LANES), rope_idx),
    ]
    args = [x_all, mod, gain, w_bf16, cos_t, sin_t]
    if norm_consts is not None:
        gsum, gexp, hg = norm_consts
        in_specs += [
            pl.BlockSpec(gsum.shape, lambda i: (0, 0)),
            pl.BlockSpec(gexp.shape, lambda i: (0, 0)),
            pl.BlockSpec(hg.shape, lambda i: (0, 0)),
        ]
        args += [gsum, gexp, hg]
    out_specs, out_shape = [], []
    if n_pre:
        out_specs.append(pl.BlockSpec((tm, n_pre), lambda i: (i, 0)))
        out_shape.append(jax.ShapeDtypeStruct((n, n_pre), F32))
    for width in (n_q, n_kv, n_kv):
        out_specs.append(pl.BlockSpec((tm, width), lambda i: (i, 0)))
        out_shape.append(jax.ShapeDtypeStruct((n, width), BF16))
    return pl.pallas_call(
        functools.partial(_inproj_kernel, n_pre=n_pre, n_qk=n_qk, qk_norm=norm_consts is not None),
        grid=(n // tm,),
        in_specs=in_specs,
        out_specs=out_specs,
        out_shape=out_shape,
        compiler_params=_cparams(("parallel",)),
        name="in_projection",
    )(*args)


def _head_query(q_ref, head, kv_head):
    blk = q_ref[:, (head // 2) * LANES:(head // 2 + 1) * LANES]
    if head % 2 != kv_head % 2:
        blk = jnp.concatenate([blk[:, HEAD_DIM:], blk[:, :HEAD_DIM]], axis=1)
    lane = lax.broadcasted_iota(jnp.int32, blk.shape, 1)
    keep = (lane >= HEAD_DIM) if kv_head % 2 else (lane < HEAD_DIM)
    return jnp.where(keep, blk, jnp.zeros_like(blk))


def _store_heads(o_ref, outs, kv_of_head):
    for p in range(len(outs) // 2):
        halves = []
        for h in (2 * p, 2 * p + 1):
            half = kv_of_head(h) % 2
            halves.append(outs[h][:, half * HEAD_DIM:(half + 1) * HEAD_DIM])
        o_ref[:, p * LANES:(p + 1) * LANES] = jnp.concatenate(halves, axis=1).astype(o_ref.dtype)


def _global_attn_kernel(bound_ref, q_ref, kc_ref, vc_ref, *rest, n_heads, group, n_lat_chunks):
    if n_lat_chunks:
        kl_ref, vl_ref, o_ref = rest
    else:
        (o_ref,) = rest
    tk = ATTN_KV_TILE

    def attend(bounded):
        outs = []
        for j in range(n_heads // group):
            cb = slice((j // 2) * LANES, (j // 2 + 1) * LANES)
            qzs = [_head_query(q_ref, h, j) for h in range(j * group, (j + 1) * group)]
            kc = kc_ref[:, cb]
            vc = vc_ref[:, cb]
            state = []
            for qz in qzs:
                s = _dot_nt(qz, kc)
                m = jnp.zeros((s.shape[0], 1), F32) if bounded else jnp.max(s, axis=-1, keepdims=True)
                p = jnp.exp(s) if bounded else jnp.exp(s - m)
                state.append((m, jnp.sum(p, axis=-1, keepdims=True), _dot(p.astype(BF16), vc)))

            if n_lat_chunks:
                def step(c, carry, qzs=qzs, cb=cb):
                    rows = pl.ds(pl.multiple_of(c * tk, tk), tk)
                    k = kl_ref[rows, cb]
                    v = vl_ref[rows, cb]
                    new = []
                    for qz, (m, l, o) in zip(qzs, carry):
                        s = _dot_nt(qz, k)
                        if bounded:
                            p = jnp.exp(s)
                            l = l + jnp.sum(p, axis=-1, keepdims=True)
                            o = o + _dot(p.astype(BF16), v)
                        else:
                            m_new = jnp.maximum(m, jnp.max(s, axis=-1, keepdims=True))
                            alpha = jnp.exp(m - m_new)
                            p = jnp.exp(s - m_new)
                            l = alpha * l + jnp.sum(p, axis=-1, keepdims=True)
                            o = alpha * o + _dot(p.astype(BF16), v)
                            m = m_new
                        new.append((m, l, o))
                    return tuple(new)

                state = lax.fori_loop(0, n_lat_chunks, step, tuple(state))
            outs.extend(o / l for (_, l, o) in state)
        _store_heads(o_ref, outs, lambda h: h // group)

    small = bound_ref[0] <= SAFE_LOGIT_BOUND
    pl.when(small)(functools.partial(attend, True))
    pl.when(jnp.logical_not(small))(functools.partial(attend, False))


def _global_attention(logit_bound, q_all, k_all, v_all, batch, seq, ctx_len, *, latent):
    n_q = q_all.shape[1]
    n_kv = k_all.shape[1]
    n_heads = n_q // HEAD_DIM
    group = n_heads // (n_kv // HEAD_DIM)
    ctx_blk0 = (batch * seq) // ctx_len
    if latent:
        tq = ATTN_Q_TILE
        q_tiles = seq // tq
        q_idx = lambda b, i: (b * q_tiles + i, 0)
        out_rows = batch * seq
    else:
        tq = ctx_len
        q_tiles = 1
        q_idx = lambda b, i: (ctx_blk0 + b, 0)
        out_rows = batch * ctx_len
    in_specs = [
        pl.BlockSpec(memory_space=pltpu.SMEM),
        pl.BlockSpec((tq, n_q), q_idx),
        pl.BlockSpec((ctx_len, n_kv), lambda b, i: (ctx_blk0 + b, 0)),
        pl.BlockSpec((ctx_len, n_kv), lambda b, i: (ctx_blk0 + b, 0)),
    ]
    args = [logit_bound, q_all, k_all, v_all]
    if latent:
        in_specs += [pl.BlockSpec((seq, n_kv), lambda b, i: (b, 0)),
                     pl.BlockSpec((seq, n_kv), lambda b, i: (b, 0))]
        args += [k_all, v_all]
    return pl.pallas_call(
        functools.partial(_global_attn_kernel, n_heads=n_heads, group=group,
                          n_lat_chunks=(seq // ATTN_KV_TILE) if latent else 0),
        grid=(batch, q_tiles),
        in_specs=in_specs,
        out_specs=pl.BlockSpec((tq, n_q), lambda b, i: (b * q_tiles + i, 0)),
        out_shape=jax.ShapeDtypeStruct((out_rows, n_q), BF16),
        compiler_params=_cparams(("parallel", "arbitrary")),
        name="global_attention" if latent else "context_attention",
    )(*args)


def _window_attn_kernel(sink_ref, q_ref, kc_ref, vc_ref, kl_ref, vl_ref, o_ref, *, n_heads, group, seq):
    tq = WIN_Q_TILE
    band = WIN_BAND
    start = pl.program_id(1) * tq
    k0 = jnp.clip(start - WINDOW, 0, seq - band)
    k0 = pl.multiple_of(k0, WINDOW)
    rows = pl.ds(k0, band)
    qpos = start + lax.broadcasted_iota(jnp.int32, (tq, band), 0)
    kpos = k0 + lax.broadcasted_iota(jnp.int32, (tq, band), 1)
    valid = jnp.abs(qpos - kpos) <= WINDOW
    outs = []
    for h in range(n_heads):
        j = h // group
        cb = slice((j // 2) * LANES, (j // 2 + 1) * LANES)
        qz = _head_query(q_ref, h, j)
        s_c = _dot_nt(qz, kc_ref[:, cb])
        s_b = jnp.where(valid, _dot_nt(qz, kl_ref[rows, cb]), MASK_VALUE)
        sink = sink_ref[h]
        m = jnp.maximum(jnp.maximum(jnp.max(s_c, axis=-1, keepdims=True),
                                    jnp.max(s_b, axis=-1, keepdims=True)), sink)
        e_c = jnp.exp(s_c - m)
        e_b = jnp.exp(s_b - m)
        denom = (jnp.sum(e_c, axis=-1, keepdims=True) + jnp.sum(e_b, axis=-1, keepdims=True)
                 + jnp.exp(sink - m))
        o = _dot(e_c.astype(BF16), vc_ref[:, cb]) + _dot(e_b.astype(BF16), vl_ref[rows, cb])
        outs.append(o / denom)
    _store_heads(o_ref, outs, lambda h: h // group)


def _window_attention(q_all, k_all, v_all, sink, batch, seq, ctx_len):
    n_q = q_all.shape[1]
    n_kv = k_all.shape[1]
    n_heads = n_q // HEAD_DIM
    group = n_heads // (n_kv // HEAD_DIM)
    tq = WIN_Q_TILE
    q_tiles = seq // tq
    ctx_blk0 = (batch * seq) // ctx_len
    return pl.pallas_call(
        functools.partial(_window_attn_kernel, n_heads=n_heads, group=group, seq=seq),
        grid=(batch, q_tiles),
        in_specs=[
            pl.BlockSpec(memory_space=pltpu.SMEM),
            pl.BlockSpec((tq, n_q), lambda b, i: (b * q_tiles + i, 0)),
            pl.BlockSpec((ctx_len, n_kv), lambda b, i: (ctx_blk0 + b, 0)),
            pl.BlockSpec((ctx_len, n_kv), lambda b, i: (ctx_blk0 + b, 0)),
            pl.BlockSpec((seq, n_kv), lambda b, i: (b, 0)),
            pl.BlockSpec((seq, n_kv), lambda b, i: (b, 0)),
        ],
        out_specs=pl.BlockSpec((tq, n_q), lambda b, i: (b * q_tiles + i, 0)),
        out_shape=jax.ShapeDtypeStruct((batch * seq, n_q), BF16),
        compiler_params=_cparams(("parallel", "arbitrary")),
        name="window_attention",
    )(sink, q_all, k_all, v_all, k_all, v_all)


def _pool_kernel(u_ref, w_ref, sc_ref, o_ref, pad_ref, *, length):
    halo = POOL_HALO
    ch = POOL_CHUNK
    zeros = jnp.zeros((halo, POOL_DIM), F32)
    pad_ref[0:halo, :] = zeros
    pad_ref[halo + length:halo + length + halo, :] = zeros
    pad_ref[halo:halo + length, :] = u_ref[...]
    lane = lax.broadcasted_iota(jnp.int32, (ch, POOL_DIM), 1)
    grp = lane // POOL_GROUP_DIM
    half_w = jnp.left_shift(1, grp)
    row0 = lax.broadcasted_iota(jnp.int32, (ch, POOL_DIM), 0)
    win_rows = ch + 2 * halo

    def chunk(c, carry):
        r0 = pl.multiple_of(c * ch, ch)
        win = pad_ref[pl.ds(r0, win_rows), :]
        acc = jnp.zeros((ch, POOL_DIM), F32)
        for dlt in range(-halo, halo):
            shifted = pltpu.roll(win, win_rows - (halo + dlt), 0)[:ch] if halo + dlt else win[:ch]
            inside = (half_w >= -dlt) if dlt < 0 else (half_w > dlt)
            acc = acc + jnp.where(inside, shifted, 0.0)
        t = row0 + r0
        hi = jnp.minimum(t + half_w, length)
        lo = jnp.maximum(t - half_w, 0)
        centre = win[halo:halo + ch]
        pooled = acc / (hi - lo).astype(F32) - centre
        mixed = _dot(pooled.astype(BF16), w_ref[...]) * sc_ref[...]
        o_ref[pl.ds(r0, ch), :] = mixed.astype(o_ref.dtype)
        return carry

    lax.fori_loop(0, length // ch, chunk, 0)


def _pool_mixer(u_all, w_blockdiag, scale, row0, n_seq, length):
    blk0 = row0 // length
    return pl.pallas_call(
        functools.partial(_pool_kernel, length=length),
        grid=(n_seq,),
        in_specs=[
            pl.BlockSpec((length, POOL_DIM), lambda s: (blk0 + s, 0)),
            pl.BlockSpec((POOL_DIM, POOL_DIM), lambda s: (0, 0)),
            pl.BlockSpec((1, POOL_DIM), lambda s: (0, 0)),
        ],
        out_specs=pl.BlockSpec((length, POOL_DIM), lambda s: (s, 0)),
        out_shape=jax.ShapeDtypeStruct((n_seq * length, POOL_DIM), BF16),
        scratch_shapes=[pltpu.VMEM((length + 2 * POOL_HALO, POOL_DIM), F32)],
        compiler_params=_cparams(("parallel",)),
        name="pool_mixer",
    )(u_all, w_blockdiag, scale)


def _outproj_kernel(x_ref, mod_ref, *rest, n_pool):
    if n_pool:
        m_ref, a_ref, w_ref, o_ref = rest
        y = _dot(m_ref[...], w_ref[:n_pool, :]) + _dot(a_ref[...], w_ref[n_pool:, :])
    else:
        a_ref, w_ref, o_ref = rest
        y = _dot(a_ref[...], w_ref[...])
    o_ref[...] = x_ref[...] + mod_ref[0, 2:3, :] * y


def _out_projection(x_all, mod, mixed, attn, w_bf16, seq, n_groups, n_rows):
    d = x_all.shape[1]
    tm = ROW_TILE
    n_pool = 0 if mixed is None else mixed.shape[1]
    n_att = attn.shape[1]

    def grp(i):
        return (jnp.minimum((i * tm) // seq, n_groups), 0, 0)

    in_specs = [pl.BlockSpec((tm, d), lambda i: (i, 0)), pl.BlockSpec((1, 6, d), grp)]
    args = [x_all, mod]
    if n_pool:
        in_specs.append(pl.BlockSpec((tm, n_pool), lambda i: (i, 0)))
        args.append(mixed)
    in_specs += [pl.BlockSpec((tm, n_att), lambda i: (i, 0)),
                 pl.BlockSpec((n_pool + n_att, d), lambda i: (0, 0))]
    args += [attn, w_bf16]
    return pl.pallas_call(
        functools.partial(_outproj_kernel, n_pool=n_pool),
        grid=(n_rows // tm,),
        in_specs=in_specs,
        out_specs=pl.BlockSpec((tm, d), lambda i: (i, 0)),
        out_shape=jax.ShapeDtypeStruct((n_rows, d), F32),
        compiler_params=_cparams(("parallel",)),
        name="out_projection",
    )(*args)


def _oddeven_merge(lo, hi, r):
    step = r * 2
    if step < hi - lo:
        yield from _oddeven_merge(lo, hi, step)
        yield from _oddeven_merge(lo + r, hi, step)
        yield from [(i, i + r) for i in range(lo + r, hi - r, step)]
    else:
        yield (lo, lo + r)


def _oddeven_sort(lo, hi):
    if hi - lo >= 1:
        mid = lo + (hi - lo) // 2
        yield from _oddeven_sort(lo, mid)
        yield from _oddeven_sort(mid + 1, hi)
        yield from _oddeven_merge(lo, hi, 1)


_SORT16 = tuple(_oddeven_sort(0, 15))
_SORT8 = tuple(_oddeven_sort(0, 7))
_TOP_PAIRS = tuple((r, c) for r in range(PEER_TOPK) for c in range(PEER_TOPK) if (r + 1) * (c + 1) <= PEER_TOPK)


def _cmpx(xs, i, j):
    a, b = xs[i], xs[j]
    xs[i] = jnp.maximum(a, b)
    xs[j] = jnp.minimum(a, b)


def _sort_desc(xs, net):
    xs = list(xs)
    for i, j in net:
        _cmpx(xs, i, j)
    return xs


def _bitonic_merge_desc(xs):
    xs = list(xs)
    n = len(xs)
    d = n // 2
    while d:
        for i in range(n):
            if not i & d:
                _cmpx(xs, i, i + d)
        d //= 2
    return xs


def _merge_top(a, b):
    n = len(a)
    return _bitonic_merge_desc([jnp.maximum(a[i], b[n - 1 - i]) for i in range(n)])


def _top16_sorted(rows):
    w = _sort_desc(rows, _SORT16)
    for shift in (4, 2, 1):
        w = _merge_top(w, [pltpu.roll(x, shift, 0) for x in w])
    return w


def _pair_threshold(pair):
    low = jnp.full_like(pair[(0, 0)], LOWEST)
    row0 = [pair[(0, c)] for c in range(16)]
    col0 = [pair[(r, 0)] for r in range(1, 16)] + [low]
    t01 = _merge_top(row0, col0)
    g2 = _bitonic_merge_desc([pair[(1, c)] for c in range(1, 8)] + [low] * 3
                             + [pair[(r, 1)] for r in range(7, 1, -1)])
    rest = [pair[k] for k in ((2, 2), (2, 3), (2, 4), (3, 2), (4, 2), (3, 3))]
    g3 = _sort_desc(rest + [low, low], _SORT8) + [low] * 8
    t23 = _merge_top(g2, g3)
    tau = None
    for i in range(16):
        m = jnp.maximum(t01[i], t23[15 - i])
        tau = m if tau is None else jnp.minimum(tau, m)
    return tau


def _route_column(s1, s2):
    assert PEER_TOPK == 16 and PEER_NKEYS == 128
    rows1 = [s1[8 * g:8 * g + 8] for g in range(16)]
    rows2 = [s2[8 * g:8 * g + 8] for g in range(16)]
    v1 = _top16_sorted(rows1)
    v2 = _top16_sorted(rows2)
    pair = {(r, c): v1[r] + v2[c] for (r, c) in _TOP_PAIRS}
    tau = _pair_threshold(pair)
    e1 = [jnp.exp(v - v1[0]) for v in v1]
    e2 = [jnp.exp(v - v2[0]) for v in v2]
    z = None
    for (r, c) in _TOP_PAIRS:
        term = jnp.where(pair[(r, c)] >= tau, e1[r] * e2[c], 0.0)
        z = term if z is None else z + term
    inv_z = 1.0 / z
    n_hi = None
    for c in range(8, 16):
        one = jnp.where(pair[(0, c)] >= tau, 1.0, 0.0)
        n_hi = one if n_hi is None else n_hi + one
    cnt, p1n, rank2, p2 = [], [], [], []
    for g in range(16):
        a = rows1[g]
        n = jnp.where(a >= v1[0], n_hi, 0.0)
        for c in range(8):
            n = n + jnp.where(a + v2[c] >= tau, 1.0, 0.0)
        cnt.append(n)
        p1n.append(jnp.exp(a - v1[0]) * inv_z)
        b = rows2[g]
        k = jnp.where(v2[0] > b, 1.0, 0.0)
        for r in range(1, 16):
            k = k + jnp.where(v2[r] > b, 1.0, 0.0)
        rank2.append(k)
        p2.append(jnp.exp(b - v2[0]))
    cat = lambda xs: jnp.concatenate(xs, axis=0)
    return cat(cnt), cat(p1n), cat(rank2), cat(p2)


def _peer_kernel(x_ref, mod_ref, g_ref, wq_ref, k1_ref, k2_ref, u_ref, vt_ref, *rest, final_norm, n_blocks):
    if final_norm:
        fg_ref, o_ref, *scratch = rest
    else:
        o_ref, *scratch = rest
    hb_ref, cnt_ref, p1_ref, rank_ref, p2_ref, ht_ref, gw0_ref, gw1_ref, acc_ref = scratch
    s = pl.program_id(1)
    eb, t = ht_ref.shape
    blocks = eb // PEER_NKEYS
    half = PEER_DKEY // 2

    @pl.when(s == 0)
    def _route():
        x = x_ref[...]
        h2 = _norm_mod(x, g_ref[...], mod_ref[0, 3:4, :], mod_ref[0, 4:5, :])
        hb_ref[...] = h2.astype(BF16)
        acc_ref[...] = jnp.zeros_like(acc_ref)

        def head(h, carry):
            qh = _dot(hb_ref[...], wq_ref[h])
            s1 = _dot_nt(k1_ref[...], qh[:, :half].astype(BF16))
            s2 = _dot_nt(k2_ref[...], qh[:, half:].astype(BF16))
            for j in range(t // LANES):
                col = slice(j * LANES, (j + 1) * LANES)
                cnt, p1n, rank2, p2 = _route_column(s1[:, col], s2[:, col])
                cnt_ref[h, :, col] = cnt
                p1_ref[h, :, col] = p1n
                rank_ref[h, :, col] = rank2.astype(BF16)
                p2_ref[h, :, col] = p2.astype(BF16)
            return carry

        lax.fori_loop(0, PEER_HEADS, head, 0)

    def stage(parity, build, consume):
        gw_w, gw_r = (gw0_ref, gw1_ref) if parity == 0 else (gw1_ref, gw0_ref)
        if build:
            ht_ref[...] = _dot_nt(u_ref[...], hb_ref[...])
            reps = PEER_NKEYS // 16
            for i in range(blocks):
                a = s * blocks + i
                w = None
                for h in range(PEER_HEADS):
                    cnt16 = jnp.broadcast_to(cnt_ref[h, pl.ds(a, 1), :], (16, t)).astype(BF16)
                    p16 = jnp.broadcast_to(p1_ref[h, pl.ds(a, 1), :], (16, t)).astype(BF16)
                    sel = rank_ref[h] < jnp.concatenate([cnt16] * reps, axis=0)
                    term = (jnp.where(sel, p2_ref[h], jnp.zeros((), BF16))
                            * jnp.concatenate([p16] * reps, axis=0))
                    w = term if w is None else w + term
                pre = ht_ref[i * PEER_NKEYS:(i + 1) * PEER_NKEYS, :]
                act = 0.5 * pre * (1.0 + lax.erf(pre * (1.0 / math.sqrt(2.0))))
                gw_w[i * PEER_NKEYS:(i + 1) * PEER_NKEYS, :] = act.astype(BF16) * w
        if consume:
            acc_ref[...] += _dot(vt_ref[...], gw_r[...])

    pl.when(s == 0)(functools.partial(stage, 0, True, False))
    steady = (s >= 1) & (s < n_blocks)
    pl.when(steady & (s % 2 == 0))(functools.partial(stage, 0, True, True))
    pl.when(steady & (s % 2 == 1))(functools.partial(stage, 1, True, True))
    pl.when(s == n_blocks)(functools.partial(stage, n_blocks % 2, False, True))

    @pl.when(s == n_blocks)
    def _finish():
        y = x_ref[...] + mod_ref[0, 5:6, :] * acc_ref[...].T
        if final_norm:
            y = y * lax.rsqrt(jnp.mean(y * y, axis=-1, keepdims=True) + EPS) * fg_ref[...]
        o_ref[...] = y


def _peer(x_all, mod, gain, wq_heads, keys1, keys2, u_bf16, vt_bf16, seq, n_groups, n_rows, final_gain):
    d = x_all.shape[1]
    t = PEER_TOKENS
    eb = PEER_EXPERT_BLOCK
    n_blocks = u_bf16.shape[0] // eb

    def grp(i, e):
        return (jnp.minimum((i * t) // seq, n_groups), 0, 0)

    in_specs = [
        pl.BlockSpec((t, d), lambda i, e: (i, 0)),
        pl.BlockSpec((1, 6, d), grp),
        pl.BlockSpec((1, d), lambda i, e: (0, 0)),
        pl.BlockSpec(wq_heads.shape, lambda i, e: (0, 0, 0)),
        pl.BlockSpec(keys1.shape, lambda i, e: (0, 0)),
        pl.BlockSpec(keys2.shape, lambda i, e: (0, 0)),
        pl.BlockSpec((eb, d), lambda i, e: (jnp.minimum(e, n_blocks - 1), 0)),
        pl.BlockSpec((d, eb), lambda i, e: (0, jnp.maximum(e - 1, 0))),
    ]
    args = [x_all, mod, gain, wq_heads, keys1, keys2, u_bf16, vt_bf16]
    if final_gain is not None:
        in_specs.append(pl.BlockSpec((1, d), lambda i, e: (0, 0)))
        args.append(final_gain)
    per_head = (PEER_HEADS, PEER_NKEYS, t)
    return pl.pallas_call(
        functools.partial(_peer_kernel, final_norm=final_gain is not None, n_blocks=n_blocks),
        grid=(n_rows // t, n_blocks + 1),
        in_specs=in_specs,
        out_specs=pl.BlockSpec((t, d), lambda i, e: (i, 0)),
        out_shape=jax.ShapeDtypeStruct((n_rows, d), F32),
        scratch_shapes=[
            pltpu.VMEM((t, d), BF16),
            pltpu.VMEM(per_head, F32),
            pltpu.VMEM(per_head, F32),
            pltpu.VMEM(per_head, BF16),
            pltpu.VMEM(per_head, BF16),
            pltpu.VMEM((eb, t), F32),
            pltpu.VMEM((eb, t), BF16),
            pltpu.VMEM((eb, t), BF16),
            pltpu.VMEM((d, t), F32),
        ],
        compiler_params=_cparams(("parallel", "arbitrary")),
        name="peer",
    )(*args)


def _rope_tables(seq, pad_rows):
    half = HEAD_DIM // 2
    t = jnp.arange(seq, dtype=jnp.int32)
    row = (t // GRID_W).astype(F32)
    col = (t % GRID_W).astype(F32)
    inv = 1.0 / (ROPE_THETA ** (jnp.arange(0, half, 2, dtype=F32) / half))
    ang_r = row[:, None] * inv[None, :]
    ang_c = col[:, None] * inv[None, :]
    cos_h = jnp.concatenate([jnp.cos(ang_r)] * 2 + [jnp.cos(ang_c)] * 2, axis=1)
    sin_h = jnp.concatenate([-jnp.sin(ang_r), jnp.sin(ang_r), -jnp.sin(ang_c), jnp.sin(ang_c)], axis=1)
    cos_t = jnp.concatenate([jnp.tile(cos_h, (1, LANES // HEAD_DIM)), jnp.ones((pad_rows, LANES), F32)], axis=0)
    sin_t = jnp.concatenate([jnp.tile(sin_h, (1, LANES // HEAD_DIM)), jnp.zeros((pad_rows, LANES), F32)], axis=0)
    return cos_t, sin_t


def _head_norm_consts(q_gain, k_gain, n_q_heads, n_kv_heads):
    n_heads = n_q_heads + n_kv_heads
    head_of_lane = np.arange(n_heads * HEAD_DIM) // HEAD_DIM
    onehot = (head_of_lane[:, None] == np.arange(LANES)[None, :]).astype(np.float32)
    gsum = jnp.asarray(onehot / HEAD_DIM, BF16)
    gexp = jnp.asarray(onehot.T, BF16)
    hg = jnp.concatenate([jnp.tile(q_gain, n_q_heads), jnp.tile(k_gain, n_kv_heads)])[None, :]
    return gsum, gexp, hg


def kernel(x, c, ctx, c_ctx, ada_w, ada_b, norm1_g, norm2_g, final_g, e_w_in, e_q_norm_g, e_k_norm_g,
           e_pool_w, e_pool_scale, e_w_out, o_w_in, o_sink, o_w_out, p_w_q, p_keys1, p_keys2, p_u, p_v):
    batch, seq, d = x.shape
    ctx_len = ctx.shape[1]
    depth = ada_w.shape[0]
    n_lat = batch * seq
    n_all = n_lat + batch * ctx_len

    x_all = jnp.concatenate([x.reshape(n_lat, d), ctx.reshape(batch * ctx_len, d)], axis=0)
    cond = jnp.concatenate([c, c_ctx[None, :], jnp.zeros((8 - (batch + 1) % 8, d), F32)], axis=0)
    mods = _mod_vectors(cond, ada_w, ada_b)
    mods = mods.reshape(depth, cond.shape[0], 6, d)
    cos_t, sin_t = _rope_tables(seq, ROW_TILE)

    for i in range(depth):
        last = i == depth - 1
        mod = mods[i]
        j = i // 2
        g1 = norm1_g[i][None, :]
        g2 = norm2_g[i][None, :]
        if i % 2 == 0:
            consts = _head_norm_consts(e_q_norm_g[j], e_k_norm_g[j], B_Q_HEADS, B_KV_HEADS)
            u, q, k, v = _in_projection(
                x_all, mod, g1, e_w_in[j].astype(BF16), cos_t, sin_t, seq, batch,
                n_pre=POOL_DIM, n_q=B_Q_HEADS * HEAD_DIM, n_kv=B_KV_HEADS * HEAD_DIM, norm_consts=consts)
            w_pool = jax.scipy.linalg.block_diag(*[e_pool_w[j, g] for g in range(POOL_GROUPS)]).astype(BF16)
            p_scale = e_pool_scale[j][None, :]
            logit_bound = (NORM_BOUND_MARGIN * math.sqrt(HEAD_DIM) * jnp.max(jnp.abs(e_q_norm_g[j]))
                           * jnp.max(jnp.abs(e_k_norm_g[j]))).reshape(1)
            attn = _global_attention(logit_bound, q, k, v, batch, seq, ctx_len, latent=True)
            mixed = _pool_mixer(u, w_pool, p_scale, 0, batch, seq)
            if not last:
                attn = jnp.concatenate(
                    [attn, _global_attention(logit_bound, q, k, v, batch, seq, ctx_len, latent=False)], axis=0)
                mixed = jnp.concatenate([mixed, _pool_mixer(u, w_pool, p_scale, n_lat, batch, ctx_len)], axis=0)
            w_out = e_w_out[j].astype(BF16)
        else:
            _q = C_Q_HEADS * HEAD_DIM
            q, k, v = _in_projection(
                x_all, mod, g1, o_w_in[j].astype(BF16), cos_t, sin_t, seq, batch,
                n_pre=0, n_q=_q, n_kv=C_KV_HEADS * HEAD_DIM, norm_consts=None)
            attn = _window_attention(q, k, v, o_sink[j], batch, seq, ctx_len)
            if not last:
                raise NotImplementedError("context stream after an odd layer")
            mixed = None
            w_out = o_w_out[j].astype(BF16)
        n_rows = n_lat if last else n_all
        x_mid = _out_projection(x_all, mod, mixed, attn, w_out, seq, batch, n_rows)
        wq_heads = p_w_q[i].astype(BF16).reshape(d, PEER_HEADS, PEER_DKEY).transpose(1, 0, 2)
        x_all = _peer(x_mid, mod, g2, wq_heads, p_keys1[i].astype(BF16), p_keys2[i].astype(BF16),
                      p_u[i].astype(BF16), p_v[i].astype(BF16).T, seq, batch, n_rows,
                      final_g[None, :] if last else None)
    return x_all[:n_lat].reshape(batch, seq, d)
```

```python
import functools
import math

import jax
import jax.numpy as jnp
import numpy as np
from jax import lax
from jax.experimental import pallas as pl
from jax.experimental.pallas import tpu as pltpu

F32 = jnp.float32
BF16 = jnp.bfloat16

GRID_W = 64
HEAD_DIM = 64
ROPE_THETA = 10000.0
EPS = 1e-6
MASK_VALUE = -1e30
POOL_GROUPS = 4
POOL_GROUP_DIM = 64
POOL_DIM = POOL_GROUPS * POOL_GROUP_DIM
POOL_WINDOWS = (2, 4, 8, 16)
POOL_HALO = 8
B_Q_HEADS = 12
B_KV_HEADS = 4
C_Q_HEADS = 16
C_KV_HEADS = 4
WINDOW = 128
PEER_HEADS = 8
PEER_NKEYS = 128
PEER_DKEY = 256
PEER_TOPK = 16
LOWEST = -3.0e38
SAFE_LOGIT_BOUND = 40.0
NORM_BOUND_MARGIN = 1.05

LANES = 128
VMEM_LIMIT_BYTES = 56 * 1024 * 1024
ROW_TILE = 512
ATTN_Q_TILE = 512
ATTN_KV_TILE = 1024
WIN_Q_TILE = 256
WIN_BAND = WIN_Q_TILE + 2 * WINDOW
PEER_TOKENS = 512
PEER_EXPERT_BLOCK = 1024
POOL_CHUNK = 256


def _cparams(semantics, flags=None):
    return pltpu.CompilerParams(dimension_semantics=semantics, vmem_limit_bytes=VMEM_LIMIT_BYTES, flags=flags)


def _dot(a, b):
    return jnp.dot(a, b, preferred_element_type=F32)


def _dot_nt(a, b):
    return lax.dot_general(a, b, (((1,), (1,)), ((), ())), preferred_element_type=F32)


def _split3(a):
    hi = a.astype(BF16)
    lo = (a - hi.astype(F32)).astype(BF16)
    return hi, lo


def _dot_f32ish(a, b):
    ah, al = _split3(a)
    bh, bl = _split3(b)
    return _dot(ah, bh) + (_dot(ah, bl) + _dot(al, bh))


def _norm_mod(x, gain, shift, scale):
    y = x * lax.rsqrt(jnp.mean(x * x, axis=-1, keepdims=True) + EPS)
    return (y * gain) * (1.0 + scale) + shift


def _mod_kernel(c_ref, w_ref, b_ref, o_ref):
    c = c_ref[...]
    act = c * (1.0 / (1.0 + jnp.exp(-c)))
    o_ref[0] = _dot_f32ish(act, w_ref[0]) + b_ref[0]


def _mod_vectors(cond, ada_w, ada_b):
    depth, d, d6 = ada_w.shape
    g8 = cond.shape[0]
    nblk = d6 // d
    return pl.pallas_call(
        _mod_kernel,
        grid=(depth, nblk),
        in_specs=[
            pl.BlockSpec((g8, d), lambda i, j: (0, 0)),
            pl.BlockSpec((1, d, d), lambda i, j: (i, 0, j)),
            pl.BlockSpec((1, 1, d), lambda i, j: (i, 0, j)),
        ],
        out_specs=pl.BlockSpec((1, g8, d), lambda i, j: (i, 0, j)),
        out_shape=jax.ShapeDtypeStruct((depth, g8, d6), F32),
        compiler_params=_cparams(("arbitrary", "arbitrary")),
        name="mod_vectors",
    )(cond, ada_w, ada_b.reshape(depth, 1, d6))


def _swap16(x):
    lane = lax.broadcasted_iota(jnp.int32, x.shape, 1)
    return jnp.where((lane & 16) == 0, pltpu.roll(x, LANES - 16, 1), pltpu.roll(x, 16, 1))


def _inproj_kernel(x_ref, mod_ref, g_ref, w_ref, cos_ref, sin_ref, *rest, n_pre, n_qk, qk_norm):
    if qk_norm:
        gsum_ref, gexp_ref, hg_ref, *outs = rest
    else:
        outs = rest
    x = x_ref[...]
    h = _norm_mod(x, g_ref[...], mod_ref[0, 0:1, :], mod_ref[0, 1:2, :])
    proj = _dot(h.astype(BF16), w_ref[...])
    qk = proj[:, n_pre:n_pre + n_qk]
    if qk_norm:
        ms = _dot((qk * qk).astype(BF16), gsum_ref[...])
        rinv = lax.rsqrt(ms + EPS)
        r_hi, r_lo = _split3(rinv)
        qk = qk * (_dot(r_hi, gexp_ref[...]) + _dot(r_lo, gexp_ref[...])) * hg_ref[...]
    cos = cos_ref[...]
    sin = sin_ref[...]
    blocks = []
    for j in range(n_qk // LANES):
        blk = qk[:, j * LANES:(j + 1) * LANES]
        blocks.append(blk * cos + _swap16(blk) * sin)
    if n_pre:
        u_ref, q_ref, k_ref, v_ref = outs
        u_ref[...] = proj[:, :n_pre]
    else:
        q_ref, k_ref, v_ref = outs
    n_q = q_ref.shape[1]
    scale = HEAD_DIM ** -0.5
    for j, blk in enumerate(blocks):
        lo = j * LANES
        if lo < n_q:
            q_ref[:, lo:lo + LANES] = (blk * scale).astype(BF16)
        else:
            k_ref[:, lo - n_q:lo - n_q + LANES] = blk.astype(BF16)
    v_ref[...] = proj[:, n_pre + n_qk:].astype(BF16)


def _in_projection(x_all, mod, gain, w_bf16, cos_t, sin_t, seq, n_groups, *, n_pre, n_q, n_kv, norm_consts):
    n, d = x_all.shape
    tm = ROW_TILE
    n_qk = n_q + n_kv
    d_in = w_bf16.shape[1]
    lat_tiles = (n_groups * seq) // tm
    seq_tiles = seq // tm

    def grp(i):
        return (jnp.minimum((i * tm) // seq, n_groups), 0, 0)

    def rope_idx(i):
        return (jnp.where(i < lat_tiles, i % seq_tiles, seq_tiles), 0)

    in_specs = [
        pl.BlockSpec((tm, d), lambda i: (i, 0)),
        pl.BlockSpec((1, 6, d), grp),
        pl.BlockSpec((1, d), lambda i: (0, 0)),
        pl.BlockSpec((d, d_in), lambda i: (0, 0)),
        pl.BlockSpec((tm, LANES), rope_idx),
        pl.BlockSpec((tm, LANES), rope_idx),
    ]
    args = [x_all, mod, gain, w_bf16, cos_t, sin_t]
    if norm_consts is not None:
        gsum, gexp, hg = norm_consts
        in_specs += [
            pl.BlockSpec(gsum.shape, lambda i: (0, 0)),
            pl.BlockSpec(gexp.shape, lambda i: (0, 0)),
            pl.BlockSpec(hg.shape, lambda i: (0, 0)),
        ]
        args += [gsum, gexp, hg]
    out_specs, out_shape = [], []
    if n_pre:
        out_specs.append(pl.BlockSpec((tm, n_pre), lambda i: (i, 0)))
        out_shape.append(jax.ShapeDtypeStruct((n, n_pre), F32))
    for width in (n_q, n_kv, n_kv):
        out_specs.append(pl.BlockSpec((tm, width), lambda i: (i, 0)))
        out_shape.append(jax.ShapeDtypeStruct((n, width), BF16))
    return pl.pallas_call(
        functools.partial(_inproj_kernel, n_pre=n_pre, n_qk=n_qk, qk_norm=norm_consts is not None),
        grid=(n // tm,),
        in_specs=in_specs,
        out_specs=out_specs,
        out_shape=out_shape,
        compiler_params=_cparams(("parallel",)),
        name="in_projection",
    )(*args)


def _head_query(q_ref, head, kv_head):
    blk = q_ref[:, (head // 2) * LANES:(head // 2 + 1) * LANES]
    if head % 2 != kv_head % 2:
        blk = jnp.concatenate([blk[:, HEAD_DIM:], blk[:, :HEAD_DIM]], axis=1)
    lane = lax.broadcasted_iota(jnp.int32, blk.shape, 1)
    keep = (lane >= HEAD_DIM) if kv_head % 2 else (lane < HEAD_DIM)
    return jnp.where(keep, blk, jnp.zeros_like(blk))


def _store_heads(o_ref, outs, kv_of_head):
    for p in range(len(outs) // 2):
        halves = []
        for h in (2 * p, 2 * p + 1):
            half = kv_of_head(h) % 2
            halves.append(outs[h][:, half * HEAD_DIM:(half + 1) * HEAD_DIM])
        o_ref[:, p * LANES:(p + 1) * LANES] = jnp.concatenate(halves, axis=1).astype(o_ref.dtype)


def _global_attn_kernel(bound_ref, q_ref, kc_ref, vc_ref, *rest, n_heads, group, n_lat_chunks):
    if n_lat_chunks:
        kl_ref, vl_ref, o_ref = rest
    else:
        (o_ref,) = rest
    tk = ATTN_KV_TILE

    def attend(bounded):
        outs = []
        for j in range(n_heads // group):
            cb = slice((j // 2) * LANES, (j // 2 + 1) * LANES)
            qzs = [_head_query(q_ref, h, j) for h in range(j * group, (j + 1) * group)]
            kc = kc_ref[:, cb]
            vc = vc_ref[:, cb]
            state = []
            for qz in qzs:
                s = _dot_nt(qz, kc)
                m = jnp.zeros((s.shape[0], 1), F32) if bounded else jnp.max(s, axis=-1, keepdims=True)
                p = jnp.exp(s) if bounded else jnp.exp(s - m)
                state.append((m, jnp.sum(p, axis=-1, keepdims=True), _dot(p.astype(BF16), vc)))

            if n_lat_chunks:
                def step(c, carry, qzs=qzs, cb=cb):
                    rows = pl.ds(pl.multiple_of(c * tk, tk), tk)
                    k = kl_ref[rows, cb]
                    v = vl_ref[rows, cb]
                    new = []
                    for qz, (m, l, o) in zip(qzs, carry):
                        s = _dot_nt(qz, k)
                        if bounded:
                            p = jnp.exp(s)
                            l = l + jnp.sum(p, axis=-1, keepdims=True)
                            o = o + _dot(p.astype(BF16), v)
                        else:
                            m_new = jnp.maximum(m, jnp.max(s, axis=-1, keepdims=True))
                            alpha = jnp.exp(m - m_new)
                            p = jnp.exp(s - m_new)
                            l = alpha * l + jnp.sum(p, axis=-1, keepdims=True)
                            o = alpha * o + _dot(p.astype(BF16), v)
                            m = m_new
                        new.append((m, l, o))
                    return tuple(new)

                state = lax.fori_loop(0, n_lat_chunks, step, tuple(state))
            outs.extend(o / l for (_, l, o) in state)
        _store_heads(o_ref, outs, lambda h: h // group)

    small = bound_ref[0] <= SAFE_LOGIT_BOUND
    pl.when(small)(functools.partial(attend, True))
    pl.when(jnp.logical_not(small))(functools.partial(attend, False))


def _global_attention(logit_bound, q_all, k_all, v_all, batch, seq, ctx_len, *, latent):
    n_q = q_all.shape[1]
    n_kv = k_all.shape[1]
    n_heads = n_q // HEAD_DIM
    group = n_heads // (n_kv // HEAD_DIM)
    ctx_blk0 = (batch * seq) // ctx_len
    if latent:
        tq = ATTN_Q_TILE
        q_tiles = seq // tq
        q_idx = lambda b, i: (b * q_tiles + i, 0)
        out_rows = batch * seq
    else:
        tq = ctx_len
        q_tiles = 1
        q_idx = lambda b, i: (ctx_blk0 + b, 0)
        out_rows = batch * ctx_len
    in_specs = [
        pl.BlockSpec(memory_space=pltpu.SMEM),
        pl.BlockSpec((tq, n_q), q_idx),
        pl.BlockSpec((ctx_len, n_kv), lambda b, i: (ctx_blk0 + b, 0)),
        pl.BlockSpec((ctx_len, n_kv), lambda b, i: (ctx_blk0 + b, 0)),
    ]
    args = [logit_bound, q_all, k_all, v_all]
    if latent:
        in_specs += [pl.BlockSpec((seq, n_kv), lambda b, i: (b, 0)),
                     pl.BlockSpec((seq, n_kv), lambda b, i: (b, 0))]
        args += [k_all, v_all]
    return pl.pallas_call(
        functools.partial(_global_attn_kernel, n_heads=n_heads, group=group,
                          n_lat_chunks=(seq // ATTN_KV_TILE) if latent else 0),
        grid=(batch, q_tiles),
        in_specs=in_specs,
        out_specs=pl.BlockSpec((tq, n_q), lambda b, i: (b * q_tiles + i, 0)),
        out_shape=jax.ShapeDtypeStruct((out_rows, n_q), BF16),
        compiler_params=_cparams(("parallel", "arbitrary")),
        name="global_attention" if latent else "context_attention",
    )(*args)


def _window_attn_kernel(sink_ref, q_ref, kc_ref, vc_ref, kl_ref, vl_ref, o_ref, *, n_heads, group, seq):
    tq = WIN_Q_TILE
    band = WIN_BAND
    start = pl.program_id(1) * tq
    k0 = jnp.clip(start - WINDOW, 0, seq - band)
    k0 = pl.multiple_of(k0, WINDOW)
    rows = pl.ds(k0, band)
    qpos = start + lax.broadcasted_iota(jnp.int32, (tq, band), 0)
    kpos = k0 + lax.broadcasted_iota(jnp.int32, (tq, band), 1)
    valid = jnp.abs(qpos - kpos) <= WINDOW
    outs = []
    for h in range(n_heads):
        j = h // group
        cb = slice((j // 2) * LANES, (j // 2 + 1) * LANES)
        qz = _head_query(q_ref, h, j)
        s_c = _dot_nt(qz, kc_ref[:, cb])
        s_b = jnp.where(valid, _dot_nt(qz, kl_ref[rows, cb]), MASK_VALUE)
        sink = sink_ref[h]
        m = jnp.maximum(jnp.maximum(jnp.max(s_c, axis=-1, keepdims=True),
                                    jnp.max(s_b, axis=-1, keepdims=True)), sink)
        e_c = jnp.exp(s_c - m)
        e_b = jnp.exp(s_b - m)
        denom = (jnp.sum(e_c, axis=-1, keepdims=True) + jnp.sum(e_b, axis=-1, keepdims=True)
                 + jnp.exp(sink - m))
        o = _dot(e_c.astype(BF16), vc_ref[:, cb]) + _dot(e_b.astype(BF16), vl_ref[rows, cb])
        outs.append(o / denom)
    _store_heads(o_ref, outs, lambda h: h // group)


def _window_attention(q_all, k_all, v_all, sink, batch, seq, ctx_len):
    n_q = q_all.shape[1]
    n_kv = k_all.shape[1]
    n_heads = n_q // HEAD_DIM
    group = n_heads // (n_kv // HEAD_DIM)
    tq = WIN_Q_TILE
    q_tiles = seq // tq
    ctx_blk0 = (batch * seq) // ctx_len
    return pl.pallas_call(
        functools.partial(_window_attn_kernel, n_heads=n_heads, group=group, seq=seq),
        grid=(batch, q_tiles),
        in_specs=[
            pl.BlockSpec(memory_space=pltpu.SMEM),
            pl.BlockSpec((tq, n_q), lambda b, i: (b * q_tiles + i, 0)),
            pl.BlockSpec((ctx_len, n_kv), lambda b, i: (ctx_blk0 + b, 0)),
            pl.BlockSpec((ctx_len, n_kv), lambda b, i: (ctx_blk0 + b, 0)),
            pl.BlockSpec((seq, n_kv), lambda b, i: (b, 0)),
            pl.BlockSpec((seq, n_kv), lambda b, i: (b, 0)),
        ],
        out_specs=pl.BlockSpec((tq, n_q), lambda b, i: (b * q_tiles + i, 0)),
        out_shape=jax.ShapeDtypeStruct((batch * seq, n_q), BF16),
        compiler_params=_cparams(("parallel", "arbitrary")),
        name="window_attention",
    )(sink, q_all, k_all, v_all, k_all, v_all)


def _pool_kernel(u_ref, w_ref, sc_ref, o_ref, pad_ref, *, length):
    halo = POOL_HALO
    ch = POOL_CHUNK
    zeros = jnp.zeros((halo, POOL_DIM), F32)
    pad_ref[0:halo, :] = zeros
    pad_ref[halo + length:halo + length + halo, :] = zeros
    pad_ref[halo:halo + length, :] = u_ref[...]
    lane = lax.broadcasted_iota(jnp.int32, (ch, POOL_DIM), 1)
    grp = lane // POOL_GROUP_DIM
    half_w = jnp.left_shift(1, grp)
    row0 = lax.broadcasted_iota(jnp.int32, (ch, POOL_DIM), 0)
    win_rows = ch + 2 * halo

    def chunk(c, carry):
        r0 = pl.multiple_of(c * ch, ch)
        win = pad_ref[pl.ds(r0, win_rows), :]
        acc = jnp.zeros((ch, POOL_DIM), F32)
        for dlt in range(-halo, halo):
            shifted = pltpu.roll(win, win_rows - (halo + dlt), 0)[:ch] if halo + dlt else win[:ch]
            inside = (half_w >= -dlt) if dlt < 0 else (half_w > dlt)
            acc = acc + jnp.where(inside, shifted, 0.0)
        t = row0 + r0
        hi = jnp.minimum(t + half_w, length)
        lo = jnp.maximum(t - half_w, 0)
        centre = win[halo:halo + ch]
        pooled = acc / (hi - lo).astype(F32) - centre
        mixed = _dot(pooled.astype(BF16), w_ref[...]) * sc_ref[...]
        o_ref[pl.ds(r0, ch), :] = mixed.astype(o_ref.dtype)
        return carry

    lax.fori_loop(0, length // ch, chunk, 0)


def _pool_mixer(u_all, w_blockdiag, scale, row0, n_seq, length):
    blk0 = row0 // length
    return pl.pallas_call(
        functools.partial(_pool_kernel, length=length),
        grid=(n_seq,),
        in_specs=[
            pl.BlockSpec((length, POOL_DIM), lambda s: (blk0 + s, 0)),
            pl.BlockSpec((POOL_DIM, POOL_DIM), lambda s: (0, 0)),
            pl.BlockSpec((1, POOL_DIM), lambda s: (0, 0)),
        ],
        out_specs=pl.BlockSpec((length, POOL_DIM), lambda s: (s, 0)),
        out_shape=jax.ShapeDtypeStruct((n_seq * length, POOL_DIM), BF16),
        scratch_shapes=[pltpu.VMEM((length + 2 * POOL_HALO, POOL_DIM), F32)],
        compiler_params=_cparams(("parallel",)),
        name="pool_mixer",
    )(u_all, w_blockdiag, scale)


def _outproj_kernel(x_ref, mod_ref, *rest, n_pool):
    if n_pool:
        m_ref, a_ref, w_ref, o_ref = rest
        y = _dot(m_ref[...], w_ref[:n_pool, :]) + _dot(a_ref[...], w_ref[n_pool:, :])
    else:
        a_ref, w_ref, o_ref = rest
        y = _dot(a_ref[...], w_ref[...])
    o_ref[...] = x_ref[...] + mod_ref[0, 2:3, :] * y


def _out_projection(x_all, mod, mixed, attn, w_bf16, seq, n_groups, n_rows):
    d = x_all.shape[1]
    tm = ROW_TILE
    n_pool = 0 if mixed is None else mixed.shape[1]
    n_att = attn.shape[1]

    def grp(i):
        return (jnp.minimum((i * tm) // seq, n_groups), 0, 0)

    in_specs = [pl.BlockSpec((tm, d), lambda i: (i, 0)), pl.BlockSpec((1, 6, d), grp)]
    args = [x_all, mod]
    if n_pool:
        in_specs.append(pl.BlockSpec((tm, n_pool), lambda i: (i, 0)))
        args.append(mixed)
    in_specs += [pl.BlockSpec((tm, n_att), lambda i: (i, 0)),
                 pl.BlockSpec((n_pool + n_att, d), lambda i: (0, 0))]
    args += [attn, w_bf16]
    return pl.pallas_call(
        functools.partial(_outproj_kernel, n_pool=n_pool),
        grid=(n_rows // tm,),
        in_specs=in_specs,
        out_specs=pl.BlockSpec((tm, d), lambda i: (i, 0)),
        out_shape=jax.ShapeDtypeStruct((n_rows, d), F32),
        compiler_params=_cparams(("parallel",)),
        name="out_projection",
    )(*args)


def _oddeven_merge(lo, hi, r):
    step = r * 2
    if step < hi - lo:
        yield from _oddeven_merge(lo, hi, step)
        yield from _oddeven_merge(lo + r, hi, step)
        yield from [(i, i + r) for i in range(lo + r, hi - r, step)]
    else:
        yield (lo, lo + r)


def _oddeven_sort(lo, hi):
    if hi - lo >= 1:
        mid = lo + (hi - lo) // 2
        yield from _oddeven_sort(lo, mid)
        yield from _oddeven_sort(mid + 1, hi)
        yield from _oddeven_merge(lo, hi, 1)


_SORT16 = tuple(_oddeven_sort(0, 15))
_SORT8 = tuple(_oddeven_sort(0, 7))
_TOP_PAIRS = tuple((r, c) for r in range(PEER_TOPK) for c in range(PEER_TOPK) if (r + 1) * (c + 1) <= PEER_TOPK)


def _cmpx(xs, i, j):
    a, b = xs[i], xs[j]
    xs[i] = jnp.maximum(a, b)
    xs[j] = jnp.minimum(a, b)


def _sort_desc(xs, net):
    xs = list(xs)
    for i, j in net:
        _cmpx(xs, i, j)
    return xs


def _bitonic_merge_desc(xs):
    xs = list(xs)
    n = len(xs)
    d = n // 2
    while d:
        for i in range(n):
            if not i & d:
                _cmpx(xs, i, i + d)
        d //= 2
    return xs


def _merge_top(a, b):
    n = len(a)
    return _bitonic_merge_desc([jnp.maximum(a[i], b[n - 1 - i]) for i in range(n)])


def _top16_sorted(rows):
    w = _sort_desc(rows, _SORT16)
    for shift in (4, 2, 1):
        w = _merge_top(w, [pltpu.roll(x, shift, 0) for x in w])
    return w


def _pair_threshold(pair):
    low = jnp.full_like(pair[(0, 0)], LOWEST)
    row0 = [pair[(0, c)] for c in range(16)]
    col0 = [pair[(r, 0)] for r in range(1, 16)] + [low]
    t01 = _merge_top(row0, col0)
    g2 = _bitonic_merge_desc([pair[(1, c)] for c in range(1, 8)] + [low] * 3
                             + [pair[(r, 1)] for r in range(7, 1, -1)])
    rest = [pair[k] for k in ((2, 2), (2, 3), (2, 4), (3, 2), (4, 2), (3, 3))]
    g3 = _sort_desc(rest + [low, low], _SORT8) + [low] * 8
    t23 = _merge_top(g2, g3)
    tau = None
    for i in range(16):
        m = jnp.maximum(t01[i], t23[15 - i])
        tau = m if tau is None else jnp.minimum(tau, m)
    return tau


def _route_column(s1, s2):
    assert PEER_TOPK == 16 and PEER_NKEYS == 128
    rows1 = [s1[8 * g:8 * g + 8] for g in range(16)]
    rows2 = [s2[8 * g:8 * g + 8] for g in range(16)]
    v1 = _top16_sorted(rows1)
    v2 = _top16_sorted(rows2)
    pair = {(r, c): v1[r] + v2[c] for (r, c) in _TOP_PAIRS}
    tau = _pair_threshold(pair)
    e1 = [jnp.exp(v - v1[0]) for v in v1]
    e2 = [jnp.exp(v - v2[0]) for v in v2]
    z = None
    for (r, c) in _TOP_PAIRS:
        term = jnp.where(pair[(r, c)] >= tau, e1[r] * e2[c], 0.0)
        z = term if z is None else z + term
    inv_z = (1.0 / math.sqrt(2.0)) / z
    n_top = jnp.zeros_like(tau)
    for c in range(16):
        n_top = jnp.where(pair[(0, c)] >= tau, c + 1.0, n_top)
    cnt, p1n, rank2, p2 = [], [], [], []
    for g in range(16):
        a = rows1[g]
        n = jnp.zeros_like(a)
        for c in range(8):
            n = jnp.where(a + v2[c] >= tau, c + 1.0, n)
        cnt.append(jnp.where(a >= v1[0], n_top, n))
        p1n.append(jnp.exp(a - v1[0]) * inv_z)
        b = rows2[g]
        k = jnp.zeros_like(b)
        for r in range(16):
            k = jnp.where(v2[r] > b, r + 1.0, k)
        rank2.append(k)
        p2.append(jnp.exp(b - v2[0]))
    cat = lambda xs: jnp.concatenate(xs, axis=0)
    return cat(cnt), cat(p1n), cat(rank2), cat(p2)


def _peer_kernel(x_ref, mod_ref, g_ref, wq_ref, k1_ref, k2_ref, u_ref, vt_ref, *rest, final_norm, n_blocks):
    if final_norm:
        fg_ref, o_ref, *scratch = rest
    else:
        o_ref, *scratch = rest
    hb_ref, cnt_ref, p1_ref, rank_ref, p2_ref, ht_ref, gw0_ref, gw1_ref, acc_ref = scratch
    s = pl.program_id(1)
    eb, t = ht_ref.shape
    blocks = eb // PEER_NKEYS
    half = PEER_DKEY // 2

    @pl.when(s == 0)
    def _route():
        x = x_ref[...]
        h2 = _norm_mod(x, g_ref[...], mod_ref[0, 3:4, :], mod_ref[0, 4:5, :])
        hb_ref[...] = h2.astype(BF16)
        acc_ref[...] = jnp.zeros_like(acc_ref)

        def head(h, carry):
            qh = _dot(hb_ref[...], wq_ref[h])
            s1 = _dot_nt(k1_ref[...], qh[:, :half].astype(BF16))
            s2 = _dot_nt(k2_ref[...], qh[:, half:].astype(BF16))
            for j in range(t // LANES):
                col = slice(j * LANES, (j + 1) * LANES)
                cnt, p1n, rank2, p2 = _route_column(s1[:, col], s2[:, col])
                cnt_ref[h, :, col] = cnt
                p1_ref[h, :, col] = p1n
                rank_ref[h, :, col] = rank2.astype(BF16)
                p2_ref[h, :, col] = p2.astype(BF16)
            return carry

        lax.fori_loop(0, PEER_HEADS, head, 0)

    def stage(parity, build, consume):
        gw_w, gw_r = (gw0_ref, gw1_ref) if parity == 0 else (gw1_ref, gw0_ref)
        if build:
            ht_ref[...] = _dot_nt(u_ref[...], hb_ref[...])
            reps = PEER_NKEYS // 16
            for i in range(blocks):
                a = s * blocks + i
                w = None
                for h in range(PEER_HEADS):
                    cnt16 = jnp.broadcast_to(cnt_ref[h, pl.ds(a, 1), :], (16, t)).astype(BF16)
                    p16 = jnp.broadcast_to(p1_ref[h, pl.ds(a, 1), :], (16, t)).astype(BF16)
                    sel = rank_ref[h] < jnp.concatenate([cnt16] * reps, axis=0)
                    term = (jnp.where(sel, p2_ref[h], jnp.zeros((), BF16))
                            * jnp.concatenate([p16] * reps, axis=0))
                    w = term if w is None else w + term
                y = ht_ref[i * PEER_NKEYS:(i + 1) * PEER_NKEYS, :]
                act = y * (1.0 + lax.erf(y))
                gw_w[i * PEER_NKEYS:(i + 1) * PEER_NKEYS, :] = act.astype(BF16) * w
        if consume:
            acc_ref[...] += _dot(vt_ref[...], gw_r[...])

    pl.when(s == 0)(functools.partial(stage, 0, True, False))
    steady = (s >= 1) & (s < n_blocks)
    pl.when(steady & (s % 2 == 0))(functools.partial(stage, 0, True, True))
    pl.when(steady & (s % 2 == 1))(functools.partial(stage, 1, True, True))
    pl.when(s == n_blocks)(functools.partial(stage, n_blocks % 2, False, True))

    @pl.when(s == n_blocks)
    def _finish():
        y = x_ref[...] + mod_ref[0, 5:6, :] * acc_ref[...].T
        if final_norm:
            y = y * lax.rsqrt(jnp.mean(y * y, axis=-1, keepdims=True) + EPS) * fg_ref[...]
        o_ref[...] = y


def _peer(x_all, mod, gain, wq_heads, keys1, keys2, u_bf16, vt_bf16, seq, n_groups, n_rows, final_gain):
    d = x_all.shape[1]
    t = PEER_TOKENS
    eb = PEER_EXPERT_BLOCK
    n_blocks = u_bf16.shape[0] // eb

    def grp(i, e):
        return (jnp.minimum((i * t) // seq, n_groups), 0, 0)

    in_specs = [
        pl.BlockSpec((t, d), lambda i, e: (i, 0)),
        pl.BlockSpec((1, 6, d), grp),
        pl.BlockSpec((1, d), lambda i, e: (0, 0)),
        pl.BlockSpec(wq_heads.shape, lambda i, e: (0, 0, 0)),
        pl.BlockSpec(keys1.shape, lambda i, e: (0, 0)),
        pl.BlockSpec(keys2.shape, lambda i, e: (0, 0)),
        pl.BlockSpec((eb, d), lambda i, e: (jnp.minimum(e, n_blocks - 1), 0)),
        pl.BlockSpec((d, eb), lambda i, e: (0, jnp.maximum(e - 1, 0))),
    ]
    args = [x_all, mod, gain, wq_heads, keys1, keys2, u_bf16, vt_bf16]
    if final_gain is not None:
        in_specs.append(pl.BlockSpec((1, d), lambda i, e: (0, 0)))
        args.append(final_gain)
    per_head = (PEER_HEADS, PEER_NKEYS, t)
    return pl.pallas_call(
        functools.partial(_peer_kernel, final_norm=final_gain is not None, n_blocks=n_blocks),
        grid=(n_rows // t, n_blocks + 1),
        in_specs=in_specs,
        out_specs=pl.BlockSpec((t, d), lambda i, e: (i, 0)),
        out_shape=jax.ShapeDtypeStruct((n_rows, d), F32),
        scratch_shapes=[
            pltpu.VMEM((t, d), BF16),
            pltpu.VMEM(per_head, F32),
            pltpu.VMEM(per_head, F32),
            pltpu.VMEM(per_head, BF16),
            pltpu.VMEM(per_head, BF16),
            pltpu.VMEM((eb, t), F32),
            pltpu.VMEM((eb, t), BF16),
            pltpu.VMEM((eb, t), BF16),
            pltpu.VMEM((d, t), F32),
        ],
        compiler_params=_cparams(("parallel", "arbitrary")),
        name="peer",
    )(*args)


def _rope_tables(seq, pad_rows):
    half = HEAD_DIM // 2
    t = jnp.arange(seq, dtype=jnp.int32)
    row = (t // GRID_W).astype(F32)
    col = (t % GRID_W).astype(F32)
    inv = 1.0 / (ROPE_THETA ** (jnp.arange(0, half, 2, dtype=F32) / half))
    ang_r = row[:, None] * inv[None, :]
    ang_c = col[:, None] * inv[None, :]
    cos_h = jnp.concatenate([jnp.cos(ang_r)] * 2 + [jnp.cos(ang_c)] * 2, axis=1)
    sin_h = jnp.concatenate([-jnp.sin(ang_r), jnp.sin(ang_r), -jnp.sin(ang_c), jnp.sin(ang_c)], axis=1)
    cos_t = jnp.concatenate([jnp.tile(cos_h, (1, LANES // HEAD_DIM)), jnp.ones((pad_rows, LANES), F32)], axis=0)
    sin_t = jnp.concatenate([jnp.tile(sin_h, (1, LANES // HEAD_DIM)), jnp.zeros((pad_rows, LANES), F32)], axis=0)
    return cos_t, sin_t


def _head_norm_consts(q_gain, k_gain, n_q_heads, n_kv_heads):
    n_heads = n_q_heads + n_kv_heads
    head_of_lane = np.arange(n_heads * HEAD_DIM) // HEAD_DIM
    onehot = (head_of_lane[:, None] == np.arange(LANES)[None, :]).astype(np.float32)
    gsum = jnp.asarray(onehot / HEAD_DIM, BF16)
    gexp = jnp.asarray(onehot.T, BF16)
    hg = jnp.concatenate([jnp.tile(q_gain, n_q_heads), jnp.tile(k_gain, n_kv_heads)])[None, :]
    return gsum, gexp, hg


def kernel(x, c, ctx, c_ctx, ada_w, ada_b, norm1_g, norm2_g, final_g, e_w_in, e_q_norm_g, e_k_norm_g,
           e_pool_w, e_pool_scale, e_w_out, o_w_in, o_sink, o_w_out, p_w_q, p_keys1, p_keys2, p_u, p_v):
    batch, seq, d = x.shape
    ctx_len = ctx.shape[1]
    depth = ada_w.shape[0]
    n_lat = batch * seq
    n_all = n_lat + batch * ctx_len

    x_all = jnp.concatenate([x.reshape(n_lat, d), ctx.reshape(batch * ctx_len, d)], axis=0)
    cond = jnp.concatenate([c, c_ctx[None, :], jnp.zeros((8 - (batch + 1) % 8, d), F32)], axis=0)
    mods = _mod_vectors(cond, ada_w, ada_b)
    mods = mods.reshape(depth, cond.shape[0], 6, d)
    cos_t, sin_t = _rope_tables(seq, ROW_TILE)

    for i in range(depth):
        last = i == depth - 1
        mod = mods[i]
        j = i // 2
        g1 = norm1_g[i][None, :]
        g2 = norm2_g[i][None, :]
        if i % 2 == 0:
            consts = _head_norm_consts(e_q_norm_g[j], e_k_norm_g[j], B_Q_HEADS, B_KV_HEADS)
            u, q, k, v = _in_projection(
                x_all, mod, g1, e_w_in[j].astype(BF16), cos_t, sin_t, seq, batch,
                n_pre=POOL_DIM, n_q=B_Q_HEADS * HEAD_DIM, n_kv=B_KV_HEADS * HEAD_DIM, norm_consts=consts)
            w_pool = jax.scipy.linalg.block_diag(*[e_pool_w[j, g] for g in range(POOL_GROUPS)]).astype(BF16)
            p_scale = e_pool_scale[j][None, :]
            logit_bound = (NORM_BOUND_MARGIN * math.sqrt(HEAD_DIM) * jnp.max(jnp.abs(e_q_norm_g[j]))
                           * jnp.max(jnp.abs(e_k_norm_g[j]))).reshape(1)
            attn = _global_attention(logit_bound, q, k, v, batch, seq, ctx_len, latent=True)
            mixed = _pool_mixer(u, w_pool, p_scale, 0, batch, seq)
            if not last:
                attn = jnp.concatenate(
                    [attn, _global_attention(logit_bound, q, k, v, batch, seq, ctx_len, latent=False)], axis=0)
                mixed = jnp.concatenate([mixed, _pool_mixer(u, w_pool, p_scale, n_lat, batch, ctx_len)], axis=0)
            w_out = e_w_out[j].astype(BF16)
        else:
            _q = C_Q_HEADS * HEAD_DIM
            q, k, v = _in_projection(
                x_all, mod, g1, o_w_in[j].astype(BF16), cos_t, sin_t, seq, batch,
                n_pre=0, n_q=_q, n_kv=C_KV_HEADS * HEAD_DIM, norm_consts=None)
            attn = _window_attention(q, k, v, o_sink[j], batch, seq, ctx_len)
            if not last:
                raise NotImplementedError("context stream after an odd layer")
            mixed = None
            w_out = o_w_out[j].astype(BF16)
        n_rows = n_lat if last else n_all
        x_mid = _out_projection(x_all, mod, mixed, attn, w_out, seq, batch, n_rows)
        wq_heads = p_w_q[i].astype(BF16).reshape(d, PEER_HEADS, PEER_DKEY).transpose(1, 0, 2)
        x_all = _peer(x_mid, mod, g2, wq_heads, p_keys1[i].astype(BF16), p_keys2[i].astype(BF16),
                      (p_u[i] * (1.0 / math.sqrt(2.0))).astype(BF16), p_v[i].astype(BF16).T, seq, batch, n_rows,
                      final_g[None, :] if last else None)
    return x_all[:n_lat].reshape(batch, seq, d)
```

```python
import functools
import math

import jax
import jax.numpy as jnp
import numpy as np
from jax import lax
from jax.experimental import pallas as pl
from jax.experimental.pallas import tpu as pltpu

F32 = jnp.float32
BF16 = jnp.bfloat16

GRID_W = 64
HEAD_DIM = 64
ROPE_THETA = 10000.0
EPS = 1e-6
MASK_VALUE = -1e30
POOL_GROUPS = 4
POOL_GROUP_DIM = 64
POOL_DIM = POOL_GROUPS * POOL_GROUP_DIM
POOL_WINDOWS = (2, 4, 8, 16)
POOL_HALO = 8
B_Q_HEADS = 12
B_KV_HEADS = 4
C_Q_HEADS = 16
C_KV_HEADS = 4
WINDOW = 128
PEER_HEADS = 8
PEER_NKEYS = 128
PEER_DKEY = 256
PEER_TOPK = 16
LOWEST = -3.0e38
SAFE_LOGIT_BOUND = 40.0
NORM_BOUND_MARGIN = 1.05

LANES = 128
VMEM_LIMIT_BYTES = 56 * 1024 * 1024
ROW_TILE = 512
ATTN_Q_TILE = 512
ATTN_KV_TILE = 1024
WIN_Q_TILE = 256
WIN_BAND = WIN_Q_TILE + 2 * WINDOW
PEER_TOKENS = 512
PEER_EXPERT_BLOCK = 1024
POOL_CHUNK = 256


def _cparams(semantics, flags=None):
    return pltpu.CompilerParams(dimension_semantics=semantics, vmem_limit_bytes=VMEM_LIMIT_BYTES, flags=flags)


def _dot(a, b):
    return jnp.dot(a, b, preferred_element_type=F32)


def _dot_nt(a, b):
    return lax.dot_general(a, b, (((1,), (1,)), ((), ())), preferred_element_type=F32)


def _split3(a):
    hi = a.astype(BF16)
    lo = (a - hi.astype(F32)).astype(BF16)
    return hi, lo


def _dot_f32ish(a, b):
    ah, al = _split3(a)
    bh, bl = _split3(b)
    return _dot(ah, bh) + (_dot(ah, bl) + _dot(al, bh))


def _norm_mod(x, gain, shift, scale):
    y = x * lax.rsqrt(jnp.mean(x * x, axis=-1, keepdims=True) + EPS)
    return (y * gain) * (1.0 + scale) + shift


def _mod_kernel(c_ref, w_ref, b_ref, o_ref):
    c = c_ref[...]
    act = c * (1.0 / (1.0 + jnp.exp(-c)))
    o_ref[0] = _dot_f32ish(act, w_ref[0]) + b_ref[0]


def _mod_vectors(cond, ada_w, ada_b):
    depth, d, d6 = ada_w.shape
    g8 = cond.shape[0]
    nblk = d6 // d
    return pl.pallas_call(
        _mod_kernel,
        grid=(depth, nblk),
        in_specs=[
            pl.BlockSpec((g8, d), lambda i, j: (0, 0)),
            pl.BlockSpec((1, d, d), lambda i, j: (i, 0, j)),
            pl.BlockSpec((1, 1, d), lambda i, j: (i, 0, j)),
        ],
        out_specs=pl.BlockSpec((1, g8, d), lambda i, j: (i, 0, j)),
        out_shape=jax.ShapeDtypeStruct((depth, g8, d6), F32),
        compiler_params=_cparams(("arbitrary", "arbitrary")),
        name="mod_vectors",
    )(cond, ada_w, ada_b.reshape(depth, 1, d6))


def _swap16(x):
    lane = lax.broadcasted_iota(jnp.int32, x.shape, 1)
    return jnp.where((lane & 16) == 0, pltpu.roll(x, LANES - 16, 1), pltpu.roll(x, 16, 1))


def _inproj_kernel(x_ref, mod_ref, g_ref, w_ref, cos_ref, sin_ref, *rest, n_pre, n_qk, qk_norm):
    if qk_norm:
        gsum_ref, gexp_ref, hg_ref, *outs = rest
    else:
        outs = rest
    x = x_ref[...]
    h = _norm_mod(x, g_ref[...], mod_ref[0, 0:1, :], mod_ref[0, 1:2, :])
    proj = _dot(h.astype(BF16), w_ref[...])
    qk = proj[:, n_pre:n_pre + n_qk]
    if qk_norm:
        ms = _dot((qk * qk).astype(BF16), gsum_ref[...])
        rinv = lax.rsqrt(ms + EPS)
        r_hi, r_lo = _split3(rinv)
        qk = qk * (_dot(r_hi, gexp_ref[...]) + _dot(r_lo, gexp_ref[...])) * hg_ref[...]
    cos = cos_ref[...]
    sin = sin_ref[...]
    blocks = []
    for j in range(n_qk // LANES):
        blk = qk[:, j * LANES:(j + 1) * LANES]
        blocks.append(blk * cos + _swap16(blk) * sin)
    if n_pre:
        u_ref, q_ref, k_ref, v_ref = outs
        u_ref[...] = proj[:, :n_pre]
    else:
        q_ref, k_ref, v_ref = outs
    n_q = q_ref.shape[1]
    scale = HEAD_DIM ** -0.5
    for j, blk in enumerate(blocks):
        lo = j * LANES
        if lo < n_q:
            q_ref[:, lo:lo + LANES] = (blk * scale).astype(BF16)
        else:
            k_ref[:, lo - n_q:lo - n_q + LANES] = blk.astype(BF16)
    v_ref[...] = proj[:, n_pre + n_qk:].astype(BF16)


def _in_projection(x_all, mod, gain, w_bf16, cos_t, sin_t, seq, n_groups, *, n_pre, n_q, n_kv, norm_consts):
    n, d = x_all.shape
    tm = ROW_TILE
    n_qk = n_q + n_kv
    d_in = w_bf16.shape[1]
    lat_tiles = (n_groups * seq) // tm
    seq_tiles = seq // tm

    def grp(i):
        return (jnp.minimum((i * tm) // seq, n_groups), 0, 0)

    def rope_idx(i):
        return (jnp.where(i < lat_tiles, i % seq_tiles, seq_tiles), 0)

    in_specs = [
        pl.BlockSpec((tm, d), lambda i: (i, 0)),
        pl.BlockSpec((1, 6, d), grp),
        pl.BlockSpec((1, d), lambda i: (0, 0)),
        pl.BlockSpec((d, d_in), lambda i: (0, 0)),
        pl.BlockSpec((tm, LANES), rope_idx),
        pl.BlockSpec((tm, LANES), rope_idx),
    ]
    args = [x_all, mod, gain, w_bf16, cos_t, sin_t]
    if norm_consts is not None:
        gsum, gexp, hg = norm_consts
        in_specs += [
            pl.BlockSpec(gsum.shape, lambda i: (0, 0)),
            pl.BlockSpec(gexp.shape, lambda i: (0, 0)),
            pl.BlockSpec(hg.shape, lambda i: (0, 0)),
        ]
        args += [gsum, gexp, hg]
    out_specs, out_shape = [], []
    if n_pre:
        out_specs.append(pl.BlockSpec((tm, n_pre), lambda i: (i, 0)))
        out_shape.append(jax.ShapeDtypeStruct((n, n_pre), F32))
    for width in (n_q, n_kv, n_kv):
        out_specs.append(pl.BlockSpec((tm, width), lambda i: (i, 0)))
        out_shape.append(jax.ShapeDtypeStruct((n, width), BF16))
    return pl.pallas_call(
        functools.partial(_inproj_kernel, n_pre=n_pre, n_qk=n_qk, qk_norm=norm_consts is not None),
        grid=(n // tm,),
        in_specs=in_specs,
        out_specs=out_specs,
        out_shape=out_shape,
        compiler_params=_cparams(("parallel",)),
        name="in_projection",
    )(*args)


def _head_query(q_ref, head, kv_head):
    blk = q_ref[:, (head // 2) * LANES:(head // 2 + 1) * LANES]
    if head % 2 != kv_head % 2:
        blk = jnp.concatenate([blk[:, HEAD_DIM:], blk[:, :HEAD_DIM]], axis=1)
    lane = lax.broadcasted_iota(jnp.int32, blk.shape, 1)
    keep = (lane >= HEAD_DIM) if kv_head % 2 else (lane < HEAD_DIM)
    return jnp.where(keep, blk, jnp.zeros_like(blk))


def _store_heads(o_ref, outs, kv_of_head):
    for p in range(len(outs) // 2):
        halves = []
        for h in (2 * p, 2 * p + 1):
            half = kv_of_head(h) % 2
            halves.append(outs[h][:, half * HEAD_DIM:(half + 1) * HEAD_DIM])
        o_ref[:, p * LANES:(p + 1) * LANES] = jnp.concatenate(halves, axis=1).astype(o_ref.dtype)


def _global_attn_kernel(bound_ref, q_ref, kc_ref, vc_ref, *rest, n_heads, group, n_lat_chunks):
    if n_lat_chunks:
        kl_ref, vl_ref, o_ref = rest
    else:
        (o_ref,) = rest
    tk = ATTN_KV_TILE

    def attend(bounded):
        outs = []
        for j in range(n_heads // group):
            cb = slice((j // 2) * LANES, (j // 2 + 1) * LANES)
            qzs = [_head_query(q_ref, h, j) for h in range(j * group, (j + 1) * group)]
            kc = kc_ref[:, cb]
            vc = vc_ref[:, cb]
            state = []
            for qz in qzs:
                s = _dot_nt(qz, kc)
                m = jnp.zeros((s.shape[0], 1), F32) if bounded else jnp.max(s, axis=-1, keepdims=True)
                p = jnp.exp(s) if bounded else jnp.exp(s - m)
                state.append((m, jnp.sum(p, axis=-1, keepdims=True), _dot(p.astype(BF16), vc)))

            if n_lat_chunks:
                def step(c, carry, qzs=qzs, cb=cb):
                    rows = pl.ds(pl.multiple_of(c * tk, tk), tk)
                    k = kl_ref[rows, cb]
                    v = vl_ref[rows, cb]
                    new = []
                    for qz, (m, l, o) in zip(qzs, carry):
                        s = _dot_nt(qz, k)
                        if bounded:
                            p = jnp.exp(s)
                            l = l + jnp.sum(p, axis=-1, keepdims=True)
                            o = o + _dot(p.astype(BF16), v)
                        else:
                            m_new = jnp.maximum(m, jnp.max(s, axis=-1, keepdims=True))
                            alpha = jnp.exp(m - m_new)
                            p = jnp.exp(s - m_new)
                            l = alpha * l + jnp.sum(p, axis=-1, keepdims=True)
                            o = alpha * o + _dot(p.astype(BF16), v)
                            m = m_new
                        new.append((m, l, o))
                    return tuple(new)

                state = lax.fori_loop(0, n_lat_chunks, step, tuple(state))
            outs.extend(o / l for (_, l, o) in state)
        _store_heads(o_ref, outs, lambda h: h // group)

    small = bound_ref[0] <= SAFE_LOGIT_BOUND
    pl.when(small)(functools.partial(attend, True))
    pl.when(jnp.logical_not(small))(functools.partial(attend, False))


def _global_attention(logit_bound, q_all, k_all, v_all, batch, seq, ctx_len, *, latent):
    n_q = q_all.shape[1]
    n_kv = k_all.shape[1]
    n_heads = n_q // HEAD_DIM
    group = n_heads // (n_kv // HEAD_DIM)
    ctx_blk0 = (batch * seq) // ctx_len
    if latent:
        tq = ATTN_Q_TILE
        q_tiles = seq // tq
        q_idx = lambda b, i: (b * q_tiles + i, 0)
        out_rows = batch * seq
    else:
        tq = ctx_len
        q_tiles = 1
        q_idx = lambda b, i: (ctx_blk0 + b, 0)
        out_rows = batch * ctx_len
    in_specs = [
        pl.BlockSpec(memory_space=pltpu.SMEM),
        pl.BlockSpec((tq, n_q), q_idx),
        pl.BlockSpec((ctx_len, n_kv), lambda b, i: (ctx_blk0 + b, 0)),
        pl.BlockSpec((ctx_len, n_kv), lambda b, i: (ctx_blk0 + b, 0)),
    ]
    args = [logit_bound, q_all, k_all, v_all]
    if latent:
        in_specs += [pl.BlockSpec((seq, n_kv), lambda b, i: (b, 0)),
                     pl.BlockSpec((seq, n_kv), lambda b, i: (b, 0))]
        args += [k_all, v_all]
    return pl.pallas_call(
        functools.partial(_global_attn_kernel, n_heads=n_heads, group=group,
                          n_lat_chunks=(seq // ATTN_KV_TILE) if latent else 0),
        grid=(batch, q_tiles),
        in_specs=in_specs,
        out_specs=pl.BlockSpec((tq, n_q), lambda b, i: (b * q_tiles + i, 0)),
        out_shape=jax.ShapeDtypeStruct((out_rows, n_q), BF16),
        compiler_params=_cparams(("parallel", "arbitrary")),
        name="global_attention" if latent else "context_attention",
    )(*args)


def _window_attn_kernel(sink_ref, q_ref, kc_ref, vc_ref, kl_ref, vl_ref, o_ref, *, n_heads, group, seq):
    tq = WIN_Q_TILE
    band = WIN_BAND
    start = pl.program_id(1) * tq
    k0 = jnp.clip(start - WINDOW, 0, seq - band)
    k0 = pl.multiple_of(k0, WINDOW)
    rows = pl.ds(k0, band)
    qpos = start + lax.broadcasted_iota(jnp.int32, (tq, band), 0)
    kpos = k0 + lax.broadcasted_iota(jnp.int32, (tq, band), 1)
    valid = jnp.abs(qpos - kpos) <= WINDOW
    outs = []
    for h in range(n_heads):
        j = h // group
        cb = slice((j // 2) * LANES, (j // 2 + 1) * LANES)
        qz = _head_query(q_ref, h, j)
        s_c = _dot_nt(qz, kc_ref[:, cb])
        s_b = jnp.where(valid, _dot_nt(qz, kl_ref[rows, cb]), MASK_VALUE)
        sink = sink_ref[h]
        m = jnp.maximum(jnp.maximum(jnp.max(s_c, axis=-1, keepdims=True),
                                    jnp.max(s_b, axis=-1, keepdims=True)), sink)
        e_c = jnp.exp(s_c - m)
        e_b = jnp.exp(s_b - m)
        denom = (jnp.sum(e_c, axis=-1, keepdims=True) + jnp.sum(e_b, axis=-1, keepdims=True)
                 + jnp.exp(sink - m))
        o = _dot(e_c.astype(BF16), vc_ref[:, cb]) + _dot(e_b.astype(BF16), vl_ref[rows, cb])
        outs.append(o / denom)
    _store_heads(o_ref, outs, lambda h: h // group)


def _window_attention(q_all, k_all, v_all, sink, batch, seq, ctx_len):
    n_q = q_all.shape[1]
    n_kv = k_all.shape[1]
    n_heads = n_q // HEAD_DIM
    group = n_heads // (n_kv // HEAD_DIM)
    tq = WIN_Q_TILE
    q_tiles = seq // tq
    ctx_blk0 = (batch * seq) // ctx_len
    return pl.pallas_call(
        functools.partial(_window_attn_kernel, n_heads=n_heads, group=group, seq=seq),
        grid=(batch, q_tiles),
        in_specs=[
            pl.BlockSpec(memory_space=pltpu.SMEM),
            pl.BlockSpec((tq, n_q), lambda b, i: (b * q_tiles + i, 0)),
            pl.BlockSpec((ctx_len, n_kv), lambda b, i: (ctx_blk0 + b, 0)),
            pl.BlockSpec((ctx_len, n_kv), lambda b, i: (ctx_blk0 + b, 0)),
            pl.BlockSpec((seq, n_kv), lambda b, i: (b, 0)),
            pl.BlockSpec((seq, n_kv), lambda b, i: (b, 0)),
        ],
        out_specs=pl.BlockSpec((tq, n_q), lambda b, i: (b * q_tiles + i, 0)),
        out_shape=jax.ShapeDtypeStruct((batch * seq, n_q), BF16),
        compiler_params=_cparams(("parallel", "arbitrary")),
        name="window_attention",
    )(sink, q_all, k_all, v_all, k_all, v_all)


def _pool_kernel(u_ref, w_ref, sc_ref, o_ref, pad_ref, *, length):
    halo = POOL_HALO
    ch = POOL_CHUNK
    zeros = jnp.zeros((halo, POOL_DIM), F32)
    pad_ref[0:halo, :] = zeros
    pad_ref[halo + length:halo + length + halo, :] = zeros
    pad_ref[halo:halo + length, :] = u_ref[...]
    lane = lax.broadcasted_iota(jnp.int32, (ch, POOL_DIM), 1)
    grp = lane // POOL_GROUP_DIM
    half_w = jnp.left_shift(1, grp)
    row0 = lax.broadcasted_iota(jnp.int32, (ch, POOL_DIM), 0)
    win_rows = ch + 2 * halo

    def chunk(c, carry):
        r0 = pl.multiple_of(c * ch, ch)
        win = pad_ref[pl.ds(r0, win_rows), :]
        acc = jnp.zeros((ch, POOL_DIM), F32)
        for dlt in range(-halo, halo):
            shifted = pltpu.roll(win, win_rows - (halo + dlt), 0)[:ch] if halo + dlt else win[:ch]
            inside = (half_w >= -dlt) if dlt < 0 else (half_w > dlt)
            acc = acc + jnp.where(inside, shifted, 0.0)
        t = row0 + r0
        hi = jnp.minimum(t + half_w, length)
        lo = jnp.maximum(t - half_w, 0)
        centre = win[halo:halo + ch]
        pooled = acc / (hi - lo).astype(F32) - centre
        mixed = _dot(pooled.astype(BF16), w_ref[...]) * sc_ref[...]
        o_ref[pl.ds(r0, ch), :] = mixed.astype(o_ref.dtype)
        return carry

    lax.fori_loop(0, length // ch, chunk, 0)


def _pool_mixer(u_all, w_blockdiag, scale, row0, n_seq, length):
    blk0 = row0 // length
    return pl.pallas_call(
        functools.partial(_pool_kernel, length=length),
        grid=(n_seq,),
        in_specs=[
            pl.BlockSpec((length, POOL_DIM), lambda s: (blk0 + s, 0)),
            pl.BlockSpec((POOL_DIM, POOL_DIM), lambda s: (0, 0)),
            pl.BlockSpec((1, POOL_DIM), lambda s: (0, 0)),
        ],
        out_specs=pl.BlockSpec((length, POOL_DIM), lambda s: (s, 0)),
        out_shape=jax.ShapeDtypeStruct((n_seq * length, POOL_DIM), BF16),
        scratch_shapes=[pltpu.VMEM((length + 2 * POOL_HALO, POOL_DIM), F32)],
        compiler_params=_cparams(("parallel",)),
        name="pool_mixer",
    )(u_all, w_blockdiag, scale)


def _outproj_kernel(x_ref, mod_ref, *rest, n_pool):
    if n_pool:
        m_ref, a_ref, w_ref, o_ref = rest
        y = _dot(m_ref[...], w_ref[:n_pool, :]) + _dot(a_ref[...], w_ref[n_pool:, :])
    else:
        a_ref, w_ref, o_ref = rest
        y = _dot(a_ref[...], w_ref[...])
    o_ref[...] = x_ref[...] + mod_ref[0, 2:3, :] * y


def _out_projection(x_all, mod, mixed, attn, w_bf16, seq, n_groups, n_rows):
    d = x_all.shape[1]
    tm = ROW_TILE
    n_pool = 0 if mixed is None else mixed.shape[1]
    n_att = attn.shape[1]

    def grp(i):
        return (jnp.minimum((i * tm) // seq, n_groups), 0, 0)

    in_specs = [pl.BlockSpec((tm, d), lambda i: (i, 0)), pl.BlockSpec((1, 6, d), grp)]
    args = [x_all, mod]
    if n_pool:
        in_specs.append(pl.BlockSpec((tm, n_pool), lambda i: (i, 0)))
        args.append(mixed)
    in_specs += [pl.BlockSpec((tm, n_att), lambda i: (i, 0)),
                 pl.BlockSpec((n_pool + n_att, d), lambda i: (0, 0))]
    args += [attn, w_bf16]
    return pl.pallas_call(
        functools.partial(_outproj_kernel, n_pool=n_pool),
        grid=(n_rows // tm,),
        in_specs=in_specs,
        out_specs=pl.BlockSpec((tm, d), lambda i: (i, 0)),
        out_shape=jax.ShapeDtypeStruct((n_rows, d), F32),
        compiler_params=_cparams(("parallel",)),
        name="out_projection",
    )(*args)


def _oddeven_merge(lo, hi, r):
    step = r * 2
    if step < hi - lo:
        yield from _oddeven_merge(lo, hi, step)
        yield from _oddeven_merge(lo + r, hi, step)
        yield from [(i, i + r) for i in range(lo + r, hi - r, step)]
    else:
        yield (lo, lo + r)


def _oddeven_sort(lo, hi):
    if hi - lo >= 1:
        mid = lo + (hi - lo) // 2
        yield from _oddeven_sort(lo, mid)
        yield from _oddeven_sort(mid + 1, hi)
        yield from _oddeven_merge(lo, hi, 1)


_SORT16 = tuple(_oddeven_sort(0, 15))
_SORT8 = tuple(_oddeven_sort(0, 7))
_TOP_PAIRS = tuple((r, c) for r in range(PEER_TOPK) for c in range(PEER_TOPK) if (r + 1) * (c + 1) <= PEER_TOPK)


def _cmpx(xs, i, j):
    a, b = xs[i], xs[j]
    xs[i] = jnp.maximum(a, b)
    xs[j] = jnp.minimum(a, b)


def _sort_desc(xs, net):
    xs = list(xs)
    for i, j in net:
        _cmpx(xs, i, j)
    return xs


def _bitonic_merge_desc(xs):
    xs = list(xs)
    n = len(xs)
    d = n // 2
    while d:
        for i in range(n):
            if not i & d:
                _cmpx(xs, i, i + d)
        d //= 2
    return xs


def _merge_top(a, b):
    n = len(a)
    return _bitonic_merge_desc([jnp.maximum(a[i], b[n - 1 - i]) for i in range(n)])


def _top16_sorted(rows):
    w = _sort_desc(rows, _SORT16)
    for shift in (4, 2, 1):
        w = _merge_top(w, [pltpu.roll(x, shift, 0) for x in w])
    return w


def _pair_threshold(pair):
    low = jnp.full_like(pair[(0, 0)], LOWEST)
    row0 = [pair[(0, c)] for c in range(16)]
    col0 = [pair[(r, 0)] for r in range(1, 16)] + [low]
    t01 = _merge_top(row0, col0)
    g2 = _bitonic_merge_desc([pair[(1, c)] for c in range(1, 8)] + [low] * 3
                             + [pair[(r, 1)] for r in range(7, 1, -1)])
    rest = [pair[k] for k in ((2, 2), (2, 3), (2, 4), (3, 2), (4, 2), (3, 3))]
    g3 = _sort_desc(rest + [low, low], _SORT8) + [low] * 8
    t23 = _merge_top(g2, g3)
    tau = None
    for i in range(16):
        m = jnp.maximum(t01[i], t23[15 - i])
        tau = m if tau is None else jnp.minimum(tau, m)
    return tau


def _dup_bf16_words(x):
    bits = pltpu.bitcast(x.astype(BF16).astype(F32), jnp.uint32)
    return bits | (bits >> 16)


def _route_tile(s1, s2):
    assert PEER_TOPK == 16 and PEER_NKEYS == 128
    n_col = s1.shape[1] // LANES
    assert 8 % n_col == 0
    span = 8 // n_col
    cols = []
    for j in range(n_col):
        c1 = s1[:, j * LANES:(j + 1) * LANES]
        c2 = s2[:, j * LANES:(j + 1) * LANES]
        rows1 = [c1[8 * g:8 * g + 8] for g in range(16)]
        rows2 = [c2[8 * g:8 * g + 8] for g in range(16)]
        cols.append((rows1, rows2, _top16_sorted(rows1), _top16_sorted(rows2)))

    sub = lax.broadcasted_iota(jnp.int32, (8, LANES), 0)

    def pack(vals):
        out = vals[-1]
        for j in range(n_col - 2, -1, -1):
            out = jnp.where(sub < (j + 1) * span, vals[j], out)
        return out

    def unpack(x, j):
        return jnp.broadcast_to(x[j * span:j * span + 1, :], (8, LANES))

    v1 = [pack([c[2][r] for c in cols]) for r in range(16)]
    v2 = [pack([c[3][r] for c in cols]) for r in range(16)]
    pair = {(r, c): v1[r] + v2[c] for (r, c) in _TOP_PAIRS}
    tau = _pair_threshold(pair)
    e1 = [jnp.exp(v - v1[0]) for v in v1]
    e2 = [jnp.exp(v - v2[0]) for v in v2]
    z = None
    for (r, c) in _TOP_PAIRS:
        term = jnp.where(pair[(r, c)] >= tau, e1[r] * e2[c], 0.0)
        z = term if z is None else z + term
    inv_z = (1.0 / math.sqrt(2.0)) / z
    n_top = jnp.zeros_like(tau)
    for c in range(16):
        n_top = jnp.where(pair[(0, c)] >= tau, c + 1.0, n_top)

    out = [[], [], [], []]
    for j, (rows1, rows2, t1, t2) in enumerate(cols):
        tau_j, inv_z_j, n_top_j = unpack(tau, j), unpack(inv_z, j), unpack(n_top, j)
        cnt, p1n, rank2, p2 = [], [], [], []
        for g in range(16):
            a = rows1[g]
            n = jnp.zeros_like(a)
            for c in range(8):
                n = jnp.where(a + t2[c] >= tau_j, c + 1.0, n)
            cnt.append(_dup_bf16_words(jnp.where(a >= t1[0], n_top_j, n)))
            p1n.append(_dup_bf16_words(jnp.exp(a - t1[0]) * inv_z_j))
            b = rows2[g]
            k = jnp.zeros_like(b)
            for r in range(16):
                k = jnp.where(t2[r] > b, r + 1.0, k)
            rank2.append(k)
            p2.append(jnp.exp(b - t2[0]))
        for dst, parts in zip(out, (cnt, p1n, rank2, p2)):
            dst.append(jnp.concatenate(parts, axis=0))
    return tuple(jnp.concatenate(parts, axis=1) for parts in out)


def _peer_kernel(x_ref, mod_ref, g_ref, wq_ref, k1_ref, k2_ref, u_ref, vt_ref, *rest, final_norm, n_blocks):
    if final_norm:
        fg_ref, o_ref, *scratch = rest
    else:
        o_ref, *scratch = rest
    hb_ref, cnt_ref, p1_ref, rank_ref, p2_ref, ht_ref, gw0_ref, gw1_ref, acc_ref = scratch
    s = pl.program_id(1)
    eb, t = ht_ref.shape
    blocks = eb // PEER_NKEYS
    half = PEER_DKEY // 2

    @pl.when(s == 0)
    def _route():
        x = x_ref[...]
        h2 = _norm_mod(x, g_ref[...], mod_ref[0, 3:4, :], mod_ref[0, 4:5, :])
        hb_ref[...] = h2.astype(BF16)
        acc_ref[...] = jnp.zeros_like(acc_ref)

        def head(h, carry):
            qh = _dot(hb_ref[...], wq_ref[h])
            s1 = _dot_nt(k1_ref[...], qh[:, :half].astype(BF16))
            s2 = _dot_nt(k2_ref[...], qh[:, half:].astype(BF16))
            cnt, p1n, rank2, p2 = _route_tile(s1, s2)
            cnt_ref[h] = cnt
            p1_ref[h] = p1n
            rank_ref[h] = rank2.astype(BF16)
            p2_ref[h] = p2.astype(BF16)
            return carry

        lax.fori_loop(0, PEER_HEADS, head, 0)

    def stage(parity, build, consume):
        gw_w, gw_r = (gw0_ref, gw1_ref) if parity == 0 else (gw1_ref, gw0_ref)
        if build:
            ht_ref[...] = _dot_nt(u_ref[...], hb_ref[...])
            reps = PEER_NKEYS // 16
            for i in range(blocks):
                a = s * blocks + i
                w = None
                for h in range(PEER_HEADS):
                    cnt16 = pltpu.bitcast(jnp.broadcast_to(cnt_ref[h, pl.ds(a, 1), :], (8, t)), BF16)
                    p16 = pltpu.bitcast(jnp.broadcast_to(p1_ref[h, pl.ds(a, 1), :], (8, t)), BF16)
                    sel = rank_ref[h] < jnp.concatenate([cnt16] * reps, axis=0)
                    term = (jnp.where(sel, p2_ref[h], jnp.zeros((), BF16))
                            * jnp.concatenate([p16] * reps, axis=0))
                    w = term if w is None else w + term
                y = ht_ref[i * PEER_NKEYS:(i + 1) * PEER_NKEYS, :]
                act = y * (1.0 + lax.erf(y))
                gw_w[i * PEER_NKEYS:(i + 1) * PEER_NKEYS, :] = act.astype(BF16) * w
        if consume:
            acc_ref[...] += _dot(vt_ref[...], gw_r[...])

    pl.when(s == 0)(functools.partial(stage, 0, True, False))
    steady = (s >= 1) & (s < n_blocks)
    pl.when(steady & (s % 2 == 0))(functools.partial(stage, 0, True, True))
    pl.when(steady & (s % 2 == 1))(functools.partial(stage, 1, True, True))
    pl.when(s == n_blocks)(functools.partial(stage, n_blocks % 2, False, True))

    @pl.when(s == n_blocks)
    def _finish():
        y = x_ref[...] + mod_ref[0, 5:6, :] * acc_ref[...].T
        if final_norm:
            y = y * lax.rsqrt(jnp.mean(y * y, axis=-1, keepdims=True) + EPS) * fg_ref[...]
        o_ref[...] = y


def _peer(x_all, mod, gain, wq_heads, keys1, keys2, u_bf16, vt_bf16, seq, n_groups, n_rows, final_gain):
    d = x_all.shape[1]
    t = PEER_TOKENS
    eb = PEER_EXPERT_BLOCK
    n_blocks = u_bf16.shape[0] // eb

    def grp(i, e):
        return (jnp.minimum((i * t) // seq, n_groups), 0, 0)

    in_specs = [
        pl.BlockSpec((t, d), lambda i, e: (i, 0)),
        pl.BlockSpec((1, 6, d), grp),
        pl.BlockSpec((1, d), lambda i, e: (0, 0)),
        pl.BlockSpec(wq_heads.shape, lambda i, e: (0, 0, 0)),
        pl.BlockSpec(keys1.shape, lambda i, e: (0, 0)),
        pl.BlockSpec(keys2.shape, lambda i, e: (0, 0)),
        pl.BlockSpec((eb, d), lambda i, e: (jnp.minimum(e, n_blocks - 1), 0)),
        pl.BlockSpec((d, eb), lambda i, e: (0, jnp.maximum(e - 1, 0))),
    ]
    args = [x_all, mod, gain, wq_heads, keys1, keys2, u_bf16, vt_bf16]
    if final_gain is not None:
        in_specs.append(pl.BlockSpec((1, d), lambda i, e: (0, 0)))
        args.append(final_gain)
    per_head = (PEER_HEADS, PEER_NKEYS, t)
    return pl.pallas_call(
        functools.partial(_peer_kernel, final_norm=final_gain is not None, n_blocks=n_blocks),
        grid=(n_rows // t, n_blocks + 1),
        in_specs=in_specs,
        out_specs=pl.BlockSpec((t, d), lambda i, e: (i, 0)),
        out_shape=jax.ShapeDtypeStruct((n_rows, d), F32),
        scratch_shapes=[
            pltpu.VMEM((t, d), BF16),
            pltpu.VMEM(per_head, jnp.uint32),
            pltpu.VMEM(per_head, jnp.uint32),
            pltpu.VMEM(per_head, BF16),
            pltpu.VMEM(per_head, BF16),
            pltpu.VMEM((eb, t), F32),
            pltpu.VMEM((eb, t), BF16),
            pltpu.VMEM((eb, t), BF16),
            pltpu.VMEM((d, t), F32),
        ],
        compiler_params=_cparams(("parallel", "arbitrary")),
        name="peer",
    )(*args)


def _rope_tables(seq, pad_rows):
    half = HEAD_DIM // 2
    t = jnp.arange(seq, dtype=jnp.int32)
    row = (t // GRID_W).astype(F32)
    col = (t % GRID_W).astype(F32)
    inv = 1.0 / (ROPE_THETA ** (jnp.arange(0, half, 2, dtype=F32) / half))
    ang_r = row[:, None] * inv[None, :]
    ang_c = col[:, None] * inv[None, :]
    cos_h = jnp.concatenate([jnp.cos(ang_r)] * 2 + [jnp.cos(ang_c)] * 2, axis=1)
    sin_h = jnp.concatenate([-jnp.sin(ang_r), jnp.sin(ang_r), -jnp.sin(ang_c), jnp.sin(ang_c)], axis=1)
    cos_t = jnp.concatenate([jnp.tile(cos_h, (1, LANES // HEAD_DIM)), jnp.ones((pad_rows, LANES), F32)], axis=0)
    sin_t = jnp.concatenate([jnp.tile(sin_h, (1, LANES // HEAD_DIM)), jnp.zeros((pad_rows, LANES), F32)], axis=0)
    return cos_t, sin_t


def _head_norm_consts(q_gain, k_gain, n_q_heads, n_kv_heads):
    n_heads = n_q_heads + n_kv_heads
    head_of_lane = np.arange(n_heads * HEAD_DIM) // HEAD_DIM
    onehot = (head_of_lane[:, None] == np.arange(LANES)[None, :]).astype(np.float32)
    gsum = jnp.asarray(onehot / HEAD_DIM, BF16)
    gexp = jnp.asarray(onehot.T, BF16)
    hg = jnp.concatenate([jnp.tile(q_gain, n_q_heads), jnp.tile(k_gain, n_kv_heads)])[None, :]
    return gsum, gexp, hg


def kernel(x, c, ctx, c_ctx, ada_w, ada_b, norm1_g, norm2_g, final_g, e_w_in, e_q_norm_g, e_k_norm_g,
           e_pool_w, e_pool_scale, e_w_out, o_w_in, o_sink, o_w_out, p_w_q, p_keys1, p_keys2, p_u, p_v):
    batch, seq, d = x.shape
    ctx_len = ctx.shape[1]
    depth = ada_w.shape[0]
    n_lat = batch * seq
    n_all = n_lat + batch * ctx_len

    x_all = jnp.concatenate([x.reshape(n_lat, d), ctx.reshape(batch * ctx_len, d)], axis=0)
    cond = jnp.concatenate([c, c_ctx[None, :], jnp.zeros((8 - (batch + 1) % 8, d), F32)], axis=0)
    mods = _mod_vectors(cond, ada_w, ada_b)
    mods = mods.reshape(depth, cond.shape[0], 6, d)
    cos_t, sin_t = _rope_tables(seq, ROW_TILE)

    for i in range(depth):
        last = i == depth - 1
        mod = mods[i]
        j = i // 2
        g1 = norm1_g[i][None, :]
        g2 = norm2_g[i][None, :]
        if i % 2 == 0:
            consts = _head_norm_consts(e_q_norm_g[j], e_k_norm_g[j], B_Q_HEADS, B_KV_HEADS)
            u, q, k, v = _in_projection(
                x_all, mod, g1, e_w_in[j].astype(BF16), cos_t, sin_t, seq, batch,
                n_pre=POOL_DIM, n_q=B_Q_HEADS * HEAD_DIM, n_kv=B_KV_HEADS * HEAD_DIM, norm_consts=consts)
            w_pool = jax.scipy.linalg.block_diag(*[e_pool_w[j, g] for g in range(POOL_GROUPS)]).astype(BF16)
            p_scale = e_pool_scale[j][None, :]
            logit_bound = (NORM_BOUND_MARGIN * math.sqrt(HEAD_DIM) * jnp.max(jnp.abs(e_q_norm_g[j]))
                           * jnp.max(jnp.abs(e_k_norm_g[j]))).reshape(1)
            attn = _global_attention(logit_bound, q, k, v, batch, seq, ctx_len, latent=True)
            mixed = _pool_mixer(u, w_pool, p_scale, 0, batch, seq)
            if not last:
                attn = jnp.concatenate(
                    [attn, _global_attention(logit_bound, q, k, v, batch, seq, ctx_len, latent=False)], axis=0)
                mixed = jnp.concatenate([mixed, _pool_mixer(u, w_pool, p_scale, n_lat, batch, ctx_len)], axis=0)
            w_out = e_w_out[j].astype(BF16)
        else:
            _q = C_Q_HEADS * HEAD_DIM
            q, k, v = _in_projection(
                x_all, mod, g1, o_w_in[j].astype(BF16), cos_t, sin_t, seq, batch,
                n_pre=0, n_q=_q, n_kv=C_KV_HEADS * HEAD_DIM, norm_consts=None)
            attn = _window_attention(q, k, v, o_sink[j], batch, seq, ctx_len)
            if not last:
                raise NotImplementedError("context stream after an odd layer")
            mixed = None
            w_out = o_w_out[j].astype(BF16)
        n_rows = n_lat if last else n_all
        x_mid = _out_projection(x_all, mod, mixed, attn, w_out, seq, batch, n_rows)
        wq_heads = p_w_q[i].astype(BF16).reshape(d, PEER_HEADS, PEER_DKEY).transpose(1, 0, 2)
        x_all = _peer(x_mid, mod, g2, wq_heads, p_keys1[i].astype(BF16), p_keys2[i].astype(BF16),
                      (p_u[i] * (1.0 / math.sqrt(2.0))).astype(BF16), p_v[i].astype(BF16).T, seq, batch, n_rows,
                      final_g[None, :] if last else None)
    return x_all[:n_lat].reshape(batch, seq, d)
```

```python
import functools
import math

import jax
import jax.numpy as jnp
import numpy as np
from jax import lax
from jax.experimental import pallas as pl
from jax.experimental.pallas import tpu as pltpu

F32 = jnp.float32
BF16 = jnp.bfloat16

GRID_W = 64
HEAD_DIM = 64
ROPE_THETA = 10000.0
EPS = 1e-6
MASK_VALUE = -1e30
POOL_GROUPS = 4
POOL_GROUP_DIM = 64
POOL_DIM = POOL_GROUPS * POOL_GROUP_DIM
POOL_WINDOWS = (2, 4, 8, 16)
POOL_HALO = 8
B_Q_HEADS = 12
B_KV_HEADS = 4
C_Q_HEADS = 16
C_KV_HEADS = 4
WINDOW = 128
PEER_HEADS = 8
PEER_NKEYS = 128
PEER_DKEY = 256
PEER_TOPK = 16
LOWEST = -3.0e38
SAFE_LOGIT_BOUND = 40.0
NORM_BOUND_MARGIN = 1.05

LANES = 128
VMEM_LIMIT_BYTES = 56 * 1024 * 1024
ROW_TILE = 512
ATTN_Q_TILE = 512
ATTN_KV_TILE = 1024
WIN_Q_TILE = 256
WIN_BAND = WIN_Q_TILE + 2 * WINDOW
PEER_TOKENS = 512
PEER_EXPERT_BLOCK = 1024
POOL_CHUNK = 256


def _cparams(semantics, flags=None):
    return pltpu.CompilerParams(dimension_semantics=semantics, vmem_limit_bytes=VMEM_LIMIT_BYTES, flags=flags)


def _dot(a, b):
    return jnp.dot(a, b, preferred_element_type=F32)


def _dot_nt(a, b):
    return lax.dot_general(a, b, (((1,), (1,)), ((), ())), preferred_element_type=F32)


def _split3(a):
    hi = a.astype(BF16)
    lo = (a - hi.astype(F32)).astype(BF16)
    return hi, lo


def _dot_f32ish(a, b):
    ah, al = _split3(a)
    bh, bl = _split3(b)
    return _dot(ah, bh) + (_dot(ah, bl) + _dot(al, bh))


def _norm_mod(x, gain, shift, scale):
    y = x * lax.rsqrt(jnp.mean(x * x, axis=-1, keepdims=True) + EPS)
    return (y * gain) * (1.0 + scale) + shift


def _mod_kernel(c_ref, w_ref, b_ref, o_ref):
    c = c_ref[...]
    act = c * (1.0 / (1.0 + jnp.exp(-c)))
    o_ref[0] = _dot_f32ish(act, w_ref[0]) + b_ref[0]


def _mod_vectors(cond, ada_w, ada_b):
    depth, d, d6 = ada_w.shape
    g8 = cond.shape[0]
    nblk = d6 // d
    return pl.pallas_call(
        _mod_kernel,
        grid=(depth, nblk),
        in_specs=[
            pl.BlockSpec((g8, d), lambda i, j: (0, 0)),
            pl.BlockSpec((1, d, d), lambda i, j: (i, 0, j)),
            pl.BlockSpec((1, 1, d), lambda i, j: (i, 0, j)),
        ],
        out_specs=pl.BlockSpec((1, g8, d), lambda i, j: (i, 0, j)),
        out_shape=jax.ShapeDtypeStruct((depth, g8, d6), F32),
        compiler_params=_cparams(("arbitrary", "arbitrary")),
        name="mod_vectors",
    )(cond, ada_w, ada_b.reshape(depth, 1, d6))


def _swap16(x):
    lane = lax.broadcasted_iota(jnp.int32, x.shape, 1)
    return jnp.where((lane & 16) == 0, pltpu.roll(x, LANES - 16, 1), pltpu.roll(x, 16, 1))


def _inproj_kernel(x_ref, mod_ref, g_ref, w_ref, cos_ref, sin_ref, *rest, n_pre, n_qk, qk_norm):
    if qk_norm:
        gsum_ref, gexp_ref, hg_ref, *outs = rest
    else:
        outs = rest
    x = x_ref[...]
    h = _norm_mod(x, g_ref[...], mod_ref[0, 0:1, :], mod_ref[0, 1:2, :])
    proj = _dot(h.astype(BF16), w_ref[...])
    qk = proj[:, n_pre:n_pre + n_qk]
    if qk_norm:
        ms = _dot((qk * qk).astype(BF16), gsum_ref[...])
        rinv = lax.rsqrt(ms + EPS)
        r_hi, r_lo = _split3(rinv)
        qk = qk * (_dot(r_hi, gexp_ref[...]) + _dot(r_lo, gexp_ref[...])) * hg_ref[...]
    cos = cos_ref[...]
    sin = sin_ref[...]
    blocks = []
    for j in range(n_qk // LANES):
        blk = qk[:, j * LANES:(j + 1) * LANES]
        blocks.append(blk * cos + _swap16(blk) * sin)
    if n_pre:
        u_ref, q_ref, k_ref, v_ref = outs
        u_ref[...] = proj[:, :n_pre]
    else:
        q_ref, k_ref, v_ref = outs
    n_q = q_ref.shape[1]
    scale = HEAD_DIM ** -0.5
    for j, blk in enumerate(blocks):
        lo = j * LANES
        if lo < n_q:
            q_ref[:, lo:lo + LANES] = (blk * scale).astype(BF16)
        else:
            k_ref[:, lo - n_q:lo - n_q + LANES] = blk.astype(BF16)
    v_ref[...] = proj[:, n_pre + n_qk:].astype(BF16)


def _in_projection(x_all, mod, gain, w_bf16, cos_t, sin_t, seq, n_groups, *, n_pre, n_q, n_kv, norm_consts):
    n, d = x_all.shape
    tm = ROW_TILE
    n_qk = n_q + n_kv
    d_in = w_bf16.shape[1]
    lat_tiles = (n_groups * seq) // tm
    seq_tiles = seq // tm

    def grp(i):
        return (jnp.minimum((i * tm) // seq, n_groups), 0, 0)

    def rope_idx(i):
        return (jnp.where(i < lat_tiles, i % seq_tiles, seq_tiles), 0)

    in_specs = [
        pl.BlockSpec((tm, d), lambda i: (i, 0)),
        pl.BlockSpec((1, 6, d), grp),
        pl.BlockSpec((1, d), lambda i: (0, 0)),
        pl.BlockSpec((d, d_in), lambda i: (0, 0)),
        pl.BlockSpec((tm, LANES), rope_idx),
        pl.BlockSpec((tm, LANES), rope_idx),
    ]
    args = [x_all, mod, gain, w_bf16, cos_t, sin_t]
    if norm_consts is not None:
        gsum, gexp, hg = norm_consts
        in_specs += [
            pl.BlockSpec(gsum.shape, lambda i: (0, 0)),
            pl.BlockSpec(gexp.shape, lambda i: (0, 0)),
            pl.BlockSpec(hg.shape, lambda i: (0, 0)),
        ]
        args += [gsum, gexp, hg]
    out_specs, out_shape = [], []
    if n_pre:
        out_specs.append(pl.BlockSpec((tm, n_pre), lambda i: (i, 0)))
        out_shape.append(jax.ShapeDtypeStruct((n, n_pre), F32))
    for width in (n_q, n_kv, n_kv):
        out_specs.append(pl.BlockSpec((tm, width), lambda i: (i, 0)))
        out_shape.append(jax.ShapeDtypeStruct((n, width), BF16))
    return pl.pallas_call(
        functools.partial(_inproj_kernel, n_pre=n_pre, n_qk=n_qk, qk_norm=norm_consts is not None),
        grid=(n // tm,),
        in_specs=in_specs,
        out_specs=out_specs,
        out_shape=out_shape,
        compiler_params=_cparams(("parallel",)),
        name="in_projection",
    )(*args)


def _head_query(q_ref, head, kv_head):
    blk = q_ref[:, (head // 2) * LANES:(head // 2 + 1) * LANES]
    if head % 2 != kv_head % 2:
        blk = jnp.concatenate([blk[:, HEAD_DIM:], blk[:, :HEAD_DIM]], axis=1)
    lane = lax.broadcasted_iota(jnp.int32, blk.shape, 1)
    keep = (lane >= HEAD_DIM) if kv_head % 2 else (lane < HEAD_DIM)
    return jnp.where(keep, blk, jnp.zeros_like(blk))


def _store_heads(o_ref, outs, kv_of_head):
    for p in range(len(outs) // 2):
        halves = []
        for h in (2 * p, 2 * p + 1):
            half = kv_of_head(h) % 2
            halves.append(outs[h][:, half * HEAD_DIM:(half + 1) * HEAD_DIM])
        o_ref[:, p * LANES:(p + 1) * LANES] = jnp.concatenate(halves, axis=1).astype(o_ref.dtype)


def _global_attn_kernel(bound_ref, q_ref, kc_ref, vc_ref, *rest, n_heads, group, n_lat_chunks):
    if n_lat_chunks:
        kl_ref, vl_ref, o_ref = rest
    else:
        (o_ref,) = rest
    tk = ATTN_KV_TILE

    def attend(bounded):
        outs = []
        for j in range(n_heads // group):
            cb = slice((j // 2) * LANES, (j // 2 + 1) * LANES)
            qzs = [_head_query(q_ref, h, j) for h in range(j * group, (j + 1) * group)]
            kc = kc_ref[:, cb]
            vc = vc_ref[:, cb]
            state = []
            for qz in qzs:
                s = _dot_nt(qz, kc)
                m = jnp.zeros((s.shape[0], 1), F32) if bounded else jnp.max(s, axis=-1, keepdims=True)
                p = jnp.exp(s) if bounded else jnp.exp(s - m)
                state.append((m, jnp.sum(p, axis=-1, keepdims=True), _dot(p.astype(BF16), vc)))

            if n_lat_chunks:
                def step(c, carry, qzs=qzs, cb=cb):
                    rows = pl.ds(pl.multiple_of(c * tk, tk), tk)
                    k = kl_ref[rows, cb]
                    v = vl_ref[rows, cb]
                    new = []
                    for qz, (m, l, o) in zip(qzs, carry):
                        s = _dot_nt(qz, k)
                        if bounded:
                            p = jnp.exp(s)
                            l = l + jnp.sum(p, axis=-1, keepdims=True)
                            o = o + _dot(p.astype(BF16), v)
                        else:
                            m_new = jnp.maximum(m, jnp.max(s, axis=-1, keepdims=True))
                            alpha = jnp.exp(m - m_new)
                            p = jnp.exp(s - m_new)
                            l = alpha * l + jnp.sum(p, axis=-1, keepdims=True)
                            o = alpha * o + _dot(p.astype(BF16), v)
                            m = m_new
                        new.append((m, l, o))
                    return tuple(new)

                state = lax.fori_loop(0, n_lat_chunks, step, tuple(state))
            outs.extend(o / l for (_, l, o) in state)
        _store_heads(o_ref, outs, lambda h: h // group)

    small = bound_ref[0] <= SAFE_LOGIT_BOUND
    pl.when(small)(functools.partial(attend, True))
    pl.when(jnp.logical_not(small))(functools.partial(attend, False))


def _global_attention(logit_bound, q_all, k_all, v_all, batch, seq, ctx_len, *, latent):
    n_q = q_all.shape[1]
    n_kv = k_all.shape[1]
    n_heads = n_q // HEAD_DIM
    group = n_heads // (n_kv // HEAD_DIM)
    ctx_blk0 = (batch * seq) // ctx_len
    if latent:
        tq = ATTN_Q_TILE
        q_tiles = seq // tq
        q_idx = lambda b, i: (b * q_tiles + i, 0)
        out_rows = batch * seq
    else:
        tq = ctx_len
        q_tiles = 1
        q_idx = lambda b, i: (ctx_blk0 + b, 0)
        out_rows = batch * ctx_len
    in_specs = [
        pl.BlockSpec(memory_space=pltpu.SMEM),
        pl.BlockSpec((tq, n_q), q_idx),
        pl.BlockSpec((ctx_len, n_kv), lambda b, i: (ctx_blk0 + b, 0)),
        pl.BlockSpec((ctx_len, n_kv), lambda b, i: (ctx_blk0 + b, 0)),
    ]
    args = [logit_bound, q_all, k_all, v_all]
    if latent:
        in_specs += [pl.BlockSpec((seq, n_kv), lambda b, i: (b, 0)),
                     pl.BlockSpec((seq, n_kv), lambda b, i: (b, 0))]
        args += [k_all, v_all]
    return pl.pallas_call(
        functools.partial(_global_attn_kernel, n_heads=n_heads, group=group,
                          n_lat_chunks=(seq // ATTN_KV_TILE) if latent else 0),
        grid=(batch, q_tiles),
        in_specs=in_specs,
        out_specs=pl.BlockSpec((tq, n_q), lambda b, i: (b * q_tiles + i, 0)),
        out_shape=jax.ShapeDtypeStruct((out_rows, n_q), BF16),
        compiler_params=_cparams(("parallel", "arbitrary")),
        name="global_attention" if latent else "context_attention",
    )(*args)


def _window_attn_kernel(sink_ref, q_ref, kc_ref, vc_ref, kl_ref, vl_ref, o_ref, *, n_heads, group, seq):
    tq = WIN_Q_TILE
    band = WIN_BAND
    start = pl.program_id(1) * tq
    k0 = jnp.clip(start - WINDOW, 0, seq - band)
    k0 = pl.multiple_of(k0, WINDOW)
    rows = pl.ds(k0, band)
    qpos = start + lax.broadcasted_iota(jnp.int32, (tq, band), 0)
    kpos = k0 + lax.broadcasted_iota(jnp.int32, (tq, band), 1)
    valid = jnp.abs(qpos - kpos) <= WINDOW
    outs = []
    for h in range(n_heads):
        j = h // group
        cb = slice((j // 2) * LANES, (j // 2 + 1) * LANES)
        qz = _head_query(q_ref, h, j)
        s_c = _dot_nt(qz, kc_ref[:, cb])
        s_b = jnp.where(valid, _dot_nt(qz, kl_ref[rows, cb]), MASK_VALUE)
        sink = sink_ref[h]
        m = jnp.maximum(jnp.maximum(jnp.max(s_c, axis=-1, keepdims=True),
                                    jnp.max(s_b, axis=-1, keepdims=True)), sink)
        e_c = jnp.exp(s_c - m)
        e_b = jnp.exp(s_b - m)
        denom = (jnp.sum(e_c, axis=-1, keepdims=True) + jnp.sum(e_b, axis=-1, keepdims=True)
                 + jnp.exp(sink - m))
        o = _dot(e_c.astype(BF16), vc_ref[:, cb]) + _dot(e_b.astype(BF16), vl_ref[rows, cb])
        outs.append(o / denom)
    _store_heads(o_ref, outs, lambda h: h // group)


def _window_attention(q_all, k_all, v_all, sink, batch, seq, ctx_len):
    n_q = q_all.shape[1]
    n_kv = k_all.shape[1]
    n_heads = n_q // HEAD_DIM
    group = n_heads // (n_kv // HEAD_DIM)
    tq = WIN_Q_TILE
    q_tiles = seq // tq
    ctx_blk0 = (batch * seq) // ctx_len
    return pl.pallas_call(
        functools.partial(_window_attn_kernel, n_heads=n_heads, group=group, seq=seq),
        grid=(batch, q_tiles),
        in_specs=[
            pl.BlockSpec(memory_space=pltpu.SMEM),
            pl.BlockSpec((tq, n_q), lambda b, i: (b * q_tiles + i, 0)),
            pl.BlockSpec((ctx_len, n_kv), lambda b, i: (ctx_blk0 + b, 0)),
            pl.BlockSpec((ctx_len, n_kv), lambda b, i: (ctx_blk0 + b, 0)),
            pl.BlockSpec((seq, n_kv), lambda b, i: (b, 0)),
            pl.BlockSpec((seq, n_kv), lambda b, i: (b, 0)),
        ],
        out_specs=pl.BlockSpec((tq, n_q), lambda b, i: (b * q_tiles + i, 0)),
        out_shape=jax.ShapeDtypeStruct((batch * seq, n_q), BF16),
        compiler_params=_cparams(("parallel", "arbitrary")),
        name="window_attention",
    )(sink, q_all, k_all, v_all, k_all, v_all)


def _pool_kernel(u_ref, w_ref, sc_ref, o_ref, pad_ref, *, length):
    halo = POOL_HALO
    ch = POOL_CHUNK
    zeros = jnp.zeros((halo, POOL_DIM), F32)
    pad_ref[0:halo, :] = zeros
    pad_ref[halo + length:halo + length + halo, :] = zeros
    pad_ref[halo:halo + length, :] = u_ref[...]
    lane = lax.broadcasted_iota(jnp.int32, (ch, POOL_DIM), 1)
    grp = lane // POOL_GROUP_DIM
    half_w = jnp.left_shift(1, grp)
    row0 = lax.broadcasted_iota(jnp.int32, (ch, POOL_DIM), 0)
    win_rows = ch + 2 * halo

    def chunk(c, carry):
        r0 = pl.multiple_of(c * ch, ch)
        win = pad_ref[pl.ds(r0, win_rows), :]
        acc = jnp.zeros((ch, POOL_DIM), F32)
        for dlt in range(-halo, halo):
            shifted = pltpu.roll(win, win_rows - (halo + dlt), 0)[:ch] if halo + dlt else win[:ch]
            inside = (half_w >= -dlt) if dlt < 0 else (half_w > dlt)
            acc = acc + jnp.where(inside, shifted, 0.0)
        t = row0 + r0
        hi = jnp.minimum(t + half_w, length)
        lo = jnp.maximum(t - half_w, 0)
        centre = win[halo:halo + ch]
        pooled = acc / (hi - lo).astype(F32) - centre
        mixed = _dot(pooled.astype(BF16), w_ref[...]) * sc_ref[...]
        o_ref[pl.ds(r0, ch), :] = mixed.astype(o_ref.dtype)
        return carry

    lax.fori_loop(0, length // ch, chunk, 0)


def _pool_mixer(u_all, w_blockdiag, scale, row0, n_seq, length):
    blk0 = row0 // length
    return pl.pallas_call(
        functools.partial(_pool_kernel, length=length),
        grid=(n_seq,),
        in_specs=[
            pl.BlockSpec((length, POOL_DIM), lambda s: (blk0 + s, 0)),
            pl.BlockSpec((POOL_DIM, POOL_DIM), lambda s: (0, 0)),
            pl.BlockSpec((1, POOL_DIM), lambda s: (0, 0)),
        ],
        out_specs=pl.BlockSpec((length, POOL_DIM), lambda s: (s, 0)),
        out_shape=jax.ShapeDtypeStruct((n_seq * length, POOL_DIM), BF16),
        scratch_shapes=[pltpu.VMEM((length + 2 * POOL_HALO, POOL_DIM), F32)],
        compiler_params=_cparams(("parallel",)),
        name="pool_mixer",
    )(u_all, w_blockdiag, scale)


def _outproj_kernel(x_ref, mod_ref, *rest, n_pool):
    if n_pool:
        m_ref, a_ref, w_ref, o_ref = rest
        y = _dot(m_ref[...], w_ref[:n_pool, :]) + _dot(a_ref[...], w_ref[n_pool:, :])
    else:
        a_ref, w_ref, o_ref = rest
        y = _dot(a_ref[...], w_ref[...])
    o_ref[...] = x_ref[...] + mod_ref[0, 2:3, :] * y


def _out_projection(x_all, mod, mixed, attn, w_bf16, seq, n_groups, n_rows):
    d = x_all.shape[1]
    tm = ROW_TILE
    n_pool = 0 if mixed is None else mixed.shape[1]
    n_att = attn.shape[1]

    def grp(i):
        return (jnp.minimum((i * tm) // seq, n_groups), 0, 0)

    in_specs = [pl.BlockSpec((tm, d), lambda i: (i, 0)), pl.BlockSpec((1, 6, d), grp)]
    args = [x_all, mod]
    if n_pool:
        in_specs.append(pl.BlockSpec((tm, n_pool), lambda i: (i, 0)))
        args.append(mixed)
    in_specs += [pl.BlockSpec((tm, n_att), lambda i: (i, 0)),
                 pl.BlockSpec((n_pool + n_att, d), lambda i: (0, 0))]
    args += [attn, w_bf16]
    return pl.pallas_call(
        functools.partial(_outproj_kernel, n_pool=n_pool),
        grid=(n_rows // tm,),
        in_specs=in_specs,
        out_specs=pl.BlockSpec((tm, d), lambda i: (i, 0)),
        out_shape=jax.ShapeDtypeStruct((n_rows, d), F32),
        compiler_params=_cparams(("parallel",)),
        name="out_projection",
    )(*args)


def _oddeven_merge(lo, hi, r):
    step = r * 2
    if step < hi - lo:
        yield from _oddeven_merge(lo, hi, step)
        yield from _oddeven_merge(lo + r, hi, step)
        yield from [(i, i + r) for i in range(lo + r, hi - r, step)]
    else:
        yield (lo, lo + r)


def _oddeven_sort(lo, hi):
    if hi - lo >= 1:
        mid = lo + (hi - lo) // 2
        yield from _oddeven_sort(lo, mid)
        yield from _oddeven_sort(mid + 1, hi)
        yield from _oddeven_merge(lo, hi, 1)


_SORT16 = tuple(_oddeven_sort(0, 15))
_SORT8 = tuple(_oddeven_sort(0, 7))
_TOP_PAIRS = tuple((r, c) for r in range(PEER_TOPK) for c in range(PEER_TOPK) if (r + 1) * (c + 1) <= PEER_TOPK)


def _cmpx(xs, i, j):
    a, b = xs[i], xs[j]
    xs[i] = jnp.maximum(a, b)
    xs[j] = jnp.minimum(a, b)


def _sort_desc(xs, net):
    xs = list(xs)
    for i, j in net:
        _cmpx(xs, i, j)
    return xs


def _bitonic_merge_desc(xs):
    xs = list(xs)
    n = len(xs)
    d = n // 2
    while d:
        for i in range(n):
            if not i & d:
                _cmpx(xs, i, i + d)
        d //= 2
    return xs


def _merge_top(a, b):
    n = len(a)
    return _bitonic_merge_desc([jnp.maximum(a[i], b[n - 1 - i]) for i in range(n)])


def _top16_sorted(rows):
    w = _sort_desc(rows, _SORT16)
    for shift in (4, 2, 1):
        w = _merge_top(w, [pltpu.roll(x, shift, 0) for x in w])
    return w


def _pair_threshold(pair):
    low = jnp.full_like(pair[(0, 0)], LOWEST)
    row0 = [pair[(0, c)] for c in range(16)]
    col0 = [pair[(r, 0)] for r in range(1, 16)] + [low]
    t01 = _merge_top(row0, col0)
    g2 = _bitonic_merge_desc([pair[(1, c)] for c in range(1, 8)] + [low] * 3
                             + [pair[(r, 1)] for r in range(7, 1, -1)])
    rest = [pair[k] for k in ((2, 2), (2, 3), (2, 4), (3, 2), (4, 2), (3, 3))]
    g3 = _sort_desc(rest + [low, low], _SORT8) + [low] * 8
    t23 = _merge_top(g2, g3)
    tau = None
    for i in range(16):
        m = jnp.maximum(t01[i], t23[15 - i])
        tau = m if tau is None else jnp.minimum(tau, m)
    return tau


def _route_tile(s1, s2):
    assert PEER_TOPK == 16 and PEER_NKEYS == 128
    n_col = s1.shape[1] // LANES
    assert 8 % n_col == 0
    span = 8 // n_col
    cols = []
    for j in range(n_col):
        c1 = s1[:, j * LANES:(j + 1) * LANES]
        c2 = s2[:, j * LANES:(j + 1) * LANES]
        rows1 = [c1[8 * g:8 * g + 8] for g in range(16)]
        rows2 = [c2[8 * g:8 * g + 8] for g in range(16)]
        cols.append((rows1, rows2, _top16_sorted(rows1), _top16_sorted(rows2)))

    sub = lax.broadcasted_iota(jnp.int32, (8, LANES), 0)

    def pack(vals):
        out = vals[-1]
        for j in range(n_col - 2, -1, -1):
            out = jnp.where(sub < (j + 1) * span, vals[j], out)
        return out

    def unpack(x, j):
        return jnp.broadcast_to(x[j * span:j * span + 1, :], (8, LANES))

    v1 = [pack([c[2][r] for c in cols]) for r in range(16)]
    v2 = [pack([c[3][r] for c in cols]) for r in range(16)]
    pair = {(r, c): v1[r] + v2[c] for (r, c) in _TOP_PAIRS}
    tau = _pair_threshold(pair)
    e1 = [jnp.exp(v - v1[0]) for v in v1]
    e2 = [jnp.exp(v - v2[0]) for v in v2]
    z = None
    for (r, c) in _TOP_PAIRS:
        term = jnp.where(pair[(r, c)] >= tau, e1[r] * e2[c], 0.0)
        z = term if z is None else z + term
    inv_z = (1.0 / math.sqrt(2.0)) / z
    n_top = jnp.zeros_like(tau)
    for c in range(16):
        n_top = jnp.where(pair[(0, c)] >= tau, c + 1.0, n_top)

    out = [[], [], [], []]
    for j, (rows1, rows2, t1, t2) in enumerate(cols):
        tau_j, inv_z_j, n_top_j = unpack(tau, j), unpack(inv_z, j), unpack(n_top, j)
        cnt, p1n, rank2, p2 = [], [], [], []
        for g in range(16):
            a = rows1[g]
            n = jnp.zeros_like(a)
            for c in range(8):
                n = jnp.where(a + t2[c] >= tau_j, c + 1.0, n)
            cnt.append(jnp.where(a >= t1[0], n_top_j, n))
            p1n.append(jnp.exp(a - t1[0]) * inv_z_j)
            b = rows2[g]
            k = jnp.zeros_like(b)
            for r in range(16):
                k = jnp.where(t2[r] > b, r + 1.0, k)
            rank2.append(k)
            p2.append(jnp.exp(b - t2[0]))
        for dst, parts in zip(out, (cnt, p1n, rank2, p2)):
            dst.append(jnp.concatenate(parts, axis=0))
    return tuple(jnp.concatenate(parts, axis=1) for parts in out)


def _peer_kernel(x_ref, mod_ref, g_ref, wq_ref, k1_ref, k2_ref, ut_ref, vt_ref, *rest, final_norm, n_blocks):
    if final_norm:
        fg_ref, o_ref, *scratch = rest
    else:
        o_ref, *scratch = rest
    hb_ref, cnt_ref, p1_ref, rank_ref, p2_ref, gw0_ref, gw1_ref, acc_ref = scratch
    s = pl.program_id(1)
    eb, t = gw0_ref.shape
    blocks = eb // PEER_NKEYS
    half = PEER_DKEY // 2

    @pl.when(s == 0)
    def _route():
        x = x_ref[...]
        h2 = _norm_mod(x, g_ref[...], mod_ref[0, 3:4, :], mod_ref[0, 4:5, :])
        hb_ref[...] = h2.astype(BF16)
        acc_ref[...] = jnp.zeros_like(acc_ref)

        def head(h, carry):
            qh = _dot(hb_ref[...], wq_ref[h])
            s1 = _dot_nt(k1_ref[...], qh[:, :half].astype(BF16))
            s2 = _dot_nt(k2_ref[...], qh[:, half:].astype(BF16))
            cnt, p1n, rank2, p2 = _route_tile(s1, s2)
            cnt_ref[h] = cnt
            p1_ref[h] = p1n
            rank_ref[h] = rank2.astype(BF16)
            p2_ref[h] = p2.astype(BF16)
            return carry

        lax.fori_loop(0, PEER_HEADS, head, 0)

    def stage(parity, build, consume):
        gw_w, gw_r = (gw0_ref, gw1_ref) if parity == 0 else (gw1_ref, gw0_ref)
        if build:
            reps = PEER_NKEYS // 16
            span = 2 * PEER_NKEYS
            for pair in range(blocks // 2):
                y_pair = _dot(hb_ref[...], ut_ref[:, pair * span:(pair + 1) * span]).T
                for i in (2 * pair, 2 * pair + 1):
                    a = s * blocks + i
                    w = None
                    for h in range(PEER_HEADS):
                        cnt16 = jnp.broadcast_to(cnt_ref[h, pl.ds(a, 1), :], (16, t)).astype(BF16)
                        p16 = jnp.broadcast_to(p1_ref[h, pl.ds(a, 1), :], (16, t)).astype(BF16)
                        sel = rank_ref[h] < jnp.concatenate([cnt16] * reps, axis=0)
                        term = (jnp.where(sel, p2_ref[h], jnp.zeros((), BF16))
                                * jnp.concatenate([p16] * reps, axis=0))
                        w = term if w is None else w + term
                    y = y_pair[(i % 2) * PEER_NKEYS:(i % 2 + 1) * PEER_NKEYS, :]
                    act = y * (1.0 + lax.erf(y))
                    gw_w[i * PEER_NKEYS:(i + 1) * PEER_NKEYS, :] = act.astype(BF16) * w
        if consume:
            span = 2 * PEER_NKEYS
            for hf in range(2):
                cols = slice(hf * (t // 2), (hf + 1) * (t // 2))
                tot = None
                for pair in range(blocks // 2):
                    rows = slice(pair * span, (pair + 1) * span)
                    part = _dot(vt_ref[:, rows], gw_r[rows, cols])
                    tot = part if tot is None else tot + part
                acc_ref[:, cols] += tot

    pl.when(s == 0)(functools.partial(stage, 0, True, False))
    steady = (s >= 1) & (s < n_blocks)
    pl.when(steady & (s % 2 == 0))(functools.partial(stage, 0, True, True))
    pl.when(steady & (s % 2 == 1))(functools.partial(stage, 1, True, True))
    pl.when(s == n_blocks)(functools.partial(stage, n_blocks % 2, False, True))

    @pl.when(s == n_blocks)
    def _finish():
        y = x_ref[...] + mod_ref[0, 5:6, :] * acc_ref[...].T
        if final_norm:
            y = y * lax.rsqrt(jnp.mean(y * y, axis=-1, keepdims=True) + EPS) * fg_ref[...]
        o_ref[...] = y


def _peer(x_all, mod, gain, wq_heads, keys1, keys2, ut_bf16, vt_bf16, seq, n_groups, n_rows, final_gain):
    d = x_all.shape[1]
    t = PEER_TOKENS
    eb = PEER_EXPERT_BLOCK
    n_blocks = ut_bf16.shape[1] // eb

    def grp(i, e):
        return (jnp.minimum((i * t) // seq, n_groups), 0, 0)

    in_specs = [
        pl.BlockSpec((t, d), lambda i, e: (i, 0)),
        pl.BlockSpec((1, 6, d), grp),
        pl.BlockSpec((1, d), lambda i, e: (0, 0)),
        pl.BlockSpec(wq_heads.shape, lambda i, e: (0, 0, 0)),
        pl.BlockSpec(keys1.shape, lambda i, e: (0, 0)),
        pl.BlockSpec(keys2.shape, lambda i, e: (0, 0)),
        pl.BlockSpec((d, eb), lambda i, e: (0, jnp.minimum(e, n_blocks - 1))),
        pl.BlockSpec((d, eb), lambda i, e: (0, jnp.maximum(e - 1, 0))),
    ]
    args = [x_all, mod, gain, wq_heads, keys1, keys2, ut_bf16, vt_bf16]
    if final_gain is not None:
        in_specs.append(pl.BlockSpec((1, d), lambda i, e: (0, 0)))
        args.append(final_gain)
    per_head = (PEER_HEADS, PEER_NKEYS, t)
    return pl.pallas_call(
        functools.partial(_peer_kernel, final_norm=final_gain is not None, n_blocks=n_blocks),
        grid=(n_rows // t, n_blocks + 1),
        in_specs=in_specs,
        out_specs=pl.BlockSpec((t, d), lambda i, e: (i, 0)),
        out_shape=jax.ShapeDtypeStruct((n_rows, d), F32),
        scratch_shapes=[
            pltpu.VMEM((t, d), BF16),
            pltpu.VMEM(per_head, F32),
            pltpu.VMEM(per_head, F32),
            pltpu.VMEM(per_head, BF16),
            pltpu.VMEM(per_head, BF16),
            pltpu.VMEM((eb, t), BF16),
            pltpu.VMEM((eb, t), BF16),
            pltpu.VMEM((d, t), F32),
        ],
        compiler_params=_cparams(("parallel", "arbitrary")),
        name="peer",
    )(*args)


def _rope_tables(seq, pad_rows):
    half = HEAD_DIM // 2
    t = jnp.arange(seq, dtype=jnp.int32)
    row = (t // GRID_W).astype(F32)
    col = (t % GRID_W).astype(F32)
    inv = 1.0 / (ROPE_THETA ** (jnp.arange(0, half, 2, dtype=F32) / half))
    ang_r = row[:, None] * inv[None, :]
    ang_c = col[:, None] * inv[None, :]
    cos_h = jnp.concatenate([jnp.cos(ang_r)] * 2 + [jnp.cos(ang_c)] * 2, axis=1)
    sin_h = jnp.concatenate([-jnp.sin(ang_r), jnp.sin(ang_r), -jnp.sin(ang_c), jnp.sin(ang_c)], axis=1)
    cos_t = jnp.concatenate([jnp.tile(cos_h, (1, LANES // HEAD_DIM)), jnp.ones((pad_rows, LANES), F32)], axis=0)
    sin_t = jnp.concatenate([jnp.tile(sin_h, (1, LANES // HEAD_DIM)), jnp.zeros((pad_rows, LANES), F32)], axis=0)
    return cos_t, sin_t


def _head_norm_consts(q_gain, k_gain, n_q_heads, n_kv_heads):
    n_heads = n_q_heads + n_kv_heads
    head_of_lane = np.arange(n_heads * HEAD_DIM) // HEAD_DIM
    onehot = (head_of_lane[:, None] == np.arange(LANES)[None, :]).astype(np.float32)
    gsum = jnp.asarray(onehot / HEAD_DIM, BF16)
    gexp = jnp.asarray(onehot.T, BF16)
    hg = jnp.concatenate([jnp.tile(q_gain, n_q_heads), jnp.tile(k_gain, n_kv_heads)])[None, :]
    return gsum, gexp, hg


def kernel(x, c, ctx, c_ctx, ada_w, ada_b, norm1_g, norm2_g, final_g, e_w_in, e_q_norm_g, e_k_norm_g,
           e_pool_w, e_pool_scale, e_w_out, o_w_in, o_sink, o_w_out, p_w_q, p_keys1, p_keys2, p_u, p_v):
    batch, seq, d = x.shape
    ctx_len = ctx.shape[1]
    depth = ada_w.shape[0]
    n_lat = batch * seq
    n_all = n_lat + batch * ctx_len

    x_all = jnp.concatenate([x.reshape(n_lat, d), ctx.reshape(batch * ctx_len, d)], axis=0)
    cond = jnp.concatenate([c, c_ctx[None, :], jnp.zeros((8 - (batch + 1) % 8, d), F32)], axis=0)
    mods = _mod_vectors(cond, ada_w, ada_b)
    mods = mods.reshape(depth, cond.shape[0], 6, d)
    cos_t, sin_t = _rope_tables(seq, ROW_TILE)

    for i in range(depth):
        last = i == depth - 1
        mod = mods[i]
        j = i // 2
        g1 = norm1_g[i][None, :]
        g2 = norm2_g[i][None, :]
        if i % 2 == 0:
            consts = _head_norm_consts(e_q_norm_g[j], e_k_norm_g[j], B_Q_HEADS, B_KV_HEADS)
            u, q, k, v = _in_projection(
                x_all, mod, g1, e_w_in[j].astype(BF16), cos_t, sin_t, seq, batch,
                n_pre=POOL_DIM, n_q=B_Q_HEADS * HEAD_DIM, n_kv=B_KV_HEADS * HEAD_DIM, norm_consts=consts)
            w_pool = jax.scipy.linalg.block_diag(*[e_pool_w[j, g] for g in range(POOL_GROUPS)]).astype(BF16)
            p_scale = e_pool_scale[j][None, :]
            logit_bound = (NORM_BOUND_MARGIN * math.sqrt(HEAD_DIM) * jnp.max(jnp.abs(e_q_norm_g[j]))
                           * jnp.max(jnp.abs(e_k_norm_g[j]))).reshape(1)
            attn = _global_attention(logit_bound, q, k, v, batch, seq, ctx_len, latent=True)
            mixed = _pool_mixer(u, w_pool, p_scale, 0, batch, seq)
            if not last:
                attn = jnp.concatenate(
                    [attn, _global_attention(logit_bound, q, k, v, batch, seq, ctx_len, latent=False)], axis=0)
                mixed = jnp.concatenate([mixed, _pool_mixer(u, w_pool, p_scale, n_lat, batch, ctx_len)], axis=0)
            w_out = e_w_out[j].astype(BF16)
        else:
            _q = C_Q_HEADS * HEAD_DIM
            q, k, v = _in_projection(
                x_all, mod, g1, o_w_in[j].astype(BF16), cos_t, sin_t, seq, batch,
                n_pre=0, n_q=_q, n_kv=C_KV_HEADS * HEAD_DIM, norm_consts=None)
            attn = _window_attention(q, k, v, o_sink[j], batch, seq, ctx_len)
            if not last:
                raise NotImplementedError("context stream after an odd layer")
            mixed = None
            w_out = o_w_out[j].astype(BF16)
        n_rows = n_lat if last else n_all
        x_mid = _out_projection(x_all, mod, mixed, attn, w_out, seq, batch, n_rows)
        wq_heads = p_w_q[i].astype(BF16).reshape(d, PEER_HEADS, PEER_DKEY).transpose(1, 0, 2)
        x_all = _peer(x_mid, mod, g2, wq_heads, p_keys1[i].astype(BF16), p_keys2[i].astype(BF16),
                      (p_u[i] * (1.0 / math.sqrt(2.0))).astype(BF16).T, p_v[i].astype(BF16).T, seq, batch, n_rows,
                      final_g[None, :] if last else None)
    return x_all[:n_lat].reshape(batch, seq, d)
```

```python
import functools
import math

import jax
import jax.numpy as jnp
import numpy as np
from jax import lax
from jax.experimental import pallas as pl
from jax.experimental.pallas import tpu as pltpu

F32 = jnp.float32
BF16 = jnp.bfloat16

GRID_W = 64
HEAD_DIM = 64
ROPE_THETA = 10000.0
EPS = 1e-6
MASK_VALUE = -1e30
POOL_GROUPS = 4
POOL_GROUP_DIM = 64
POOL_DIM = POOL_GROUPS * POOL_GROUP_DIM
POOL_WINDOWS = (2, 4, 8, 16)
POOL_HALO = 8
B_Q_HEADS = 12
B_KV_HEADS = 4
C_Q_HEADS = 16
C_KV_HEADS = 4
WINDOW = 128
PEER_HEADS = 8
PEER_NKEYS = 128
PEER_DKEY = 256
PEER_TOPK = 16
LOWEST = -3.0e38
SAFE_LOGIT_BOUND = 40.0
NORM_BOUND_MARGIN = 1.05

LANES = 128
VMEM_LIMIT_BYTES = 56 * 1024 * 1024
ROW_TILE = 512
ATTN_Q_TILE = 512
ATTN_KV_TILE = 1024
WIN_Q_TILE = 256
WIN_BAND = WIN_Q_TILE + 2 * WINDOW
PEER_TOKENS = 512
PEER_EXPERT_BLOCK = 1024
POOL_CHUNK = 256


def _cparams(semantics):
    return pltpu.CompilerParams(dimension_semantics=semantics, vmem_limit_bytes=VMEM_LIMIT_BYTES)


def _dot(a, b):
    return jnp.dot(a, b, preferred_element_type=F32)


def _dot_nt(a, b):
    return lax.dot_general(a, b, (((1,), (1,)), ((), ())), preferred_element_type=F32)


def _split3(a):
    hi = a.astype(BF16)
    lo = (a - hi.astype(F32)).astype(BF16)
    return hi, lo


def _dot_f32ish(a, b):
    ah, al = _split3(a)
    bh, bl = _split3(b)
    return _dot(ah, bh) + (_dot(ah, bl) + _dot(al, bh))


def _norm_mod(x, gain, shift, scale):
    y = x * lax.rsqrt(jnp.mean(x * x, axis=-1, keepdims=True) + EPS)
    return (y * gain) * (1.0 + scale) + shift


def _mod_kernel(c_ref, w_ref, b_ref, o_ref):
    c = c_ref[...]
    act = c * (1.0 / (1.0 + jnp.exp(-c)))
    o_ref[0] = _dot_f32ish(act, w_ref[0]) + b_ref[0]


def _mod_vectors(cond, ada_w, ada_b):
    depth, d, d6 = ada_w.shape
    g8 = cond.shape[0]
    nblk = d6 // d
    return pl.pallas_call(
        _mod_kernel,
        grid=(depth, nblk),
        in_specs=[
            pl.BlockSpec((g8, d), lambda i, j: (0, 0)),
            pl.BlockSpec((1, d, d), lambda i, j: (i, 0, j)),
            pl.BlockSpec((1, 1, d), lambda i, j: (i, 0, j)),
        ],
        out_specs=pl.BlockSpec((1, g8, d), lambda i, j: (i, 0, j)),
        out_shape=jax.ShapeDtypeStruct((depth, g8, d6), F32),
        compiler_params=_cparams(("arbitrary", "arbitrary")),
        name="mod_vectors",
    )(cond, ada_w, ada_b.reshape(depth, 1, d6))


def _swap16(x):
    lane = lax.broadcasted_iota(jnp.int32, x.shape, 1)
    return jnp.where((lane & 16) == 0, pltpu.roll(x, LANES - 16, 1), pltpu.roll(x, 16, 1))


def _inproj_kernel(x_ref, mod_ref, g_ref, w_ref, cos_ref, sin_ref, *rest, n_pre, n_qk, qk_norm):
    if qk_norm:
        gsum_ref, gexp_ref, hg_ref, *outs = rest
    else:
        outs = rest
    x = x_ref[...]
    h = _norm_mod(x, g_ref[...], mod_ref[0, 0:1, :], mod_ref[0, 1:2, :])
    proj = _dot(h.astype(BF16), w_ref[...])
    qk = proj[:, n_pre:n_pre + n_qk]
    if qk_norm:
        ms = _dot((qk * qk).astype(BF16), gsum_ref[...])
        rinv = lax.rsqrt(ms + EPS)
        r_hi, r_lo = _split3(rinv)
        qk = qk * (_dot(r_hi, gexp_ref[...]) + _dot(r_lo, gexp_ref[...])) * hg_ref[...]
    cos = cos_ref[...]
    sin = sin_ref[...]
    blocks = []
    for j in range(n_qk // LANES):
        blk = qk[:, j * LANES:(j + 1) * LANES]
        blocks.append(blk * cos + _swap16(blk) * sin)
    if n_pre:
        u_ref, q_ref, k_ref, v_ref = outs
        u_ref[...] = proj[:, :n_pre]
    else:
        q_ref, k_ref, v_ref = outs
    n_q = q_ref.shape[1]
    scale = HEAD_DIM ** -0.5
    for j, blk in enumerate(blocks):
        lo = j * LANES
        if lo < n_q:
            q_ref[:, lo:lo + LANES] = (blk * scale).astype(BF16)
        else:
            k_ref[:, lo - n_q:lo - n_q + LANES] = blk.astype(BF16)
    v_ref[...] = proj[:, n_pre + n_qk:].astype(BF16)


def _in_projection(x_all, mod, gain, w_bf16, cos_t, sin_t, seq, n_groups, *, n_pre, n_q, n_kv, norm_consts):
    n, d = x_all.shape
    tm = ROW_TILE
    n_qk = n_q + n_kv
    d_in = w_bf16.shape[1]
    lat_tiles = (n_groups * seq) // tm
    seq_tiles = seq // tm

    def grp(i):
        return (jnp.minimum((i * tm) // seq, n_groups), 0, 0)

    def rope_idx(i):
        return (jnp.where(i < lat_tiles, i % seq_tiles, seq_tiles), 0)

    in_specs = [
        pl.BlockSpec((tm, d), lambda i: (i, 0)),
        pl.BlockSpec((1, 6, d), grp),
        pl.BlockSpec((1, d), lambda i: (0, 0)),
        pl.BlockSpec((d, d_in), lambda i: (0, 0)),
        pl.BlockSpec((tm, LANES), rope_idx),
        pl.BlockSpec((tm, LANES), rope_idx),
    ]
    args = [x_all, mod, gain, w_bf16, cos_t, sin_t]
    if norm_consts is not None:
        gsum, gexp, hg = norm_consts
        in_specs += [
            pl.BlockSpec(gsum.shape, lambda i: (0, 0)),
            pl.BlockSpec(gexp.shape, lambda i: (0, 0)),
            pl.BlockSpec(hg.shape, lambda i: (0, 0)),
        ]
        args += [gsum, gexp, hg]
    out_specs, out_shape = [], []
    if n_pre:
        out_specs.append(pl.BlockSpec((tm, n_pre), lambda i: (i, 0)))
        out_shape.append(jax.ShapeDtypeStruct((n, n_pre), F32))
    for width in (n_q, n_kv, n_kv):
        out_specs.append(pl.BlockSpec((tm, width), lambda i: (i, 0)))
        out_shape.append(jax.ShapeDtypeStruct((n, width), BF16))
    return pl.pallas_call(
        functools.partial(_inproj_kernel, n_pre=n_pre, n_qk=n_qk, qk_norm=norm_consts is not None),
        grid=(n // tm,),
        in_specs=in_specs,
        out_specs=out_specs,
        out_shape=out_shape,
        compiler_params=_cparams(("parallel",)),
        name="in_projection",
    )(*args)


def _head_query(q_ref, head, kv_head):
    blk = q_ref[:, (head // 2) * LANES:(head // 2 + 1) * LANES]
    if head % 2 != kv_head % 2:
        blk = jnp.concatenate([blk[:, HEAD_DIM:], blk[:, :HEAD_DIM]], axis=1)
    lane = lax.broadcasted_iota(jnp.int32, blk.shape, 1)
    keep = (lane >= HEAD_DIM) if kv_head % 2 else (lane < HEAD_DIM)
    return jnp.where(keep, blk, jnp.zeros_like(blk))


def _store_heads(o_ref, outs, kv_of_head):
    for p in range(len(outs) // 2):
        halves = []
        for h in (2 * p, 2 * p + 1):
            half = kv_of_head(h) % 2
            halves.append(outs[h][:, half * HEAD_DIM:(half + 1) * HEAD_DIM])
        o_ref[:, p * LANES:(p + 1) * LANES] = jnp.concatenate(halves, axis=1).astype(o_ref.dtype)


def _global_attn_kernel(bound_ref, q_ref, kc_ref, vc_ref, *rest, n_heads, group, n_lat_chunks):
    if n_lat_chunks:
        kl_ref, vl_ref, o_ref = rest
    else:
        (o_ref,) = rest
    tk = ATTN_KV_TILE

    def attend(bounded):
        outs = []
        for j in range(n_heads // group):
            cb = slice((j // 2) * LANES, (j // 2 + 1) * LANES)
            qzs = [_head_query(q_ref, h, j) for h in range(j * group, (j + 1) * group)]
            kc = kc_ref[:, cb]
            vc = vc_ref[:, cb]
            state = []
            for qz in qzs:
                s = _dot_nt(qz, kc)
                m = jnp.zeros((s.shape[0], 1), F32) if bounded else jnp.max(s, axis=-1, keepdims=True)
                p = jnp.exp(s) if bounded else jnp.exp(s - m)
                state.append((m, jnp.sum(p, axis=-1, keepdims=True), _dot(p.astype(BF16), vc)))

            if n_lat_chunks:
                def step(c, carry, qzs=qzs, cb=cb):
                    rows = pl.ds(pl.multiple_of(c * tk, tk), tk)
                    k = kl_ref[rows, cb]
                    v = vl_ref[rows, cb]
                    new = []
                    for qz, (m, l, o) in zip(qzs, carry):
                        s = _dot_nt(qz, k)
                        if bounded:
                            p = jnp.exp(s)
                            l = l + jnp.sum(p, axis=-1, keepdims=True)
                            o = o + _dot(p.astype(BF16), v)
                        else:
                            m_new = jnp.maximum(m, jnp.max(s, axis=-1, keepdims=True))
                            alpha = jnp.exp(m - m_new)
                            p = jnp.exp(s - m_new)
                            l = alpha * l + jnp.sum(p, axis=-1, keepdims=True)
                            o = alpha * o + _dot(p.astype(BF16), v)
                            m = m_new
                        new.append((m, l, o))
                    return tuple(new)

                state = lax.fori_loop(0, n_lat_chunks, step, tuple(state))
            outs.extend(o / l for (_, l, o) in state)
        _store_heads(o_ref, outs, lambda h: h // group)

    small = bound_ref[0] <= SAFE_LOGIT_BOUND
    pl.when(small)(functools.partial(attend, True))
    pl.when(jnp.logical_not(small))(functools.partial(attend, False))


def _global_attention(logit_bound, q_all, k_all, v_all, batch, seq, ctx_len, *, latent):
    n_q = q_all.shape[1]
    n_kv = k_all.shape[1]
    n_heads = n_q // HEAD_DIM
    group = n_heads // (n_kv // HEAD_DIM)
    ctx_blk0 = (batch * seq) // ctx_len
    if latent:
        tq = ATTN_Q_TILE
        q_tiles = seq // tq
        q_idx = lambda b, i: (b * q_tiles + i, 0)
        out_rows = batch * seq
    else:
        tq = ctx_len
        q_tiles = 1
        q_idx = lambda b, i: (ctx_blk0 + b, 0)
        out_rows = batch * ctx_len
    in_specs = [
        pl.BlockSpec(memory_space=pltpu.SMEM),
        pl.BlockSpec((tq, n_q), q_idx),
        pl.BlockSpec((ctx_len, n_kv), lambda b, i: (ctx_blk0 + b, 0)),
        pl.BlockSpec((ctx_len, n_kv), lambda b, i: (ctx_blk0 + b, 0)),
    ]
    args = [logit_bound, q_all, k_all, v_all]
    if latent:
        in_specs += [pl.BlockSpec((seq, n_kv), lambda b, i: (b, 0)),
                     pl.BlockSpec((seq, n_kv), lambda b, i: (b, 0))]
        args += [k_all, v_all]
    return pl.pallas_call(
        functools.partial(_global_attn_kernel, n_heads=n_heads, group=group,
                          n_lat_chunks=(seq // ATTN_KV_TILE) if latent else 0),
        grid=(batch, q_tiles),
        in_specs=in_specs,
        out_specs=pl.BlockSpec((tq, n_q), lambda b, i: (b * q_tiles + i, 0)),
        out_shape=jax.ShapeDtypeStruct((out_rows, n_q), BF16),
        compiler_params=_cparams(("parallel", "arbitrary")),
        name="global_attention" if latent else "context_attention",
    )(*args)


def _window_attn_kernel(sink_ref, q_ref, kc_ref, vc_ref, kl_ref, vl_ref, o_ref, *, n_heads, group, seq):
    tq = WIN_Q_TILE
    band = WIN_BAND
    start = pl.program_id(1) * tq
    k0 = jnp.clip(start - WINDOW, 0, seq - band)
    k0 = pl.multiple_of(k0, WINDOW)
    rows = pl.ds(k0, band)
    qpos = start + lax.broadcasted_iota(jnp.int32, (tq, band), 0)
    kpos = k0 + lax.broadcasted_iota(jnp.int32, (tq, band), 1)
    valid = jnp.abs(qpos - kpos) <= WINDOW
    outs = []
    for h in range(n_heads):
        j = h // group
        cb = slice((j // 2) * LANES, (j // 2 + 1) * LANES)
        qz = _head_query(q_ref, h, j)
        s_c = _dot_nt(qz, kc_ref[:, cb])
        s_b = jnp.where(valid, _dot_nt(qz, kl_ref[rows, cb]), MASK_VALUE)
        sink = sink_ref[h]
        m = jnp.maximum(jnp.maximum(jnp.max(s_c, axis=-1, keepdims=True),
                                    jnp.max(s_b, axis=-1, keepdims=True)), sink)
        e_c = jnp.exp(s_c - m)
        e_b = jnp.exp(s_b - m)
        denom = (jnp.sum(e_c, axis=-1, keepdims=True) + jnp.sum(e_b, axis=-1, keepdims=True)
                 + jnp.exp(sink - m))
        o = _dot(e_c.astype(BF16), vc_ref[:, cb]) + _dot(e_b.astype(BF16), vl_ref[rows, cb])
        outs.append(o / denom)
    _store_heads(o_ref, outs, lambda h: h // group)


def _window_attention(q_all, k_all, v_all, sink, batch, seq, ctx_len):
    n_q = q_all.shape[1]
    n_kv = k_all.shape[1]
    n_heads = n_q // HEAD_DIM
    group = n_heads // (n_kv // HEAD_DIM)
    tq = WIN_Q_TILE
    q_tiles = seq // tq
    ctx_blk0 = (batch * seq) // ctx_len
    return pl.pallas_call(
        functools.partial(_window_attn_kernel, n_heads=n_heads, group=group, seq=seq),
        grid=(batch, q_tiles),
        in_specs=[
            pl.BlockSpec(memory_space=pltpu.SMEM),
            pl.BlockSpec((tq, n_q), lambda b, i: (b * q_tiles + i, 0)),
            pl.BlockSpec((ctx_len, n_kv), lambda b, i: (ctx_blk0 + b, 0)),
            pl.BlockSpec((ctx_len, n_kv), lambda b, i: (ctx_blk0 + b, 0)),
            pl.BlockSpec((seq, n_kv), lambda b, i: (b, 0)),
            pl.BlockSpec((seq, n_kv), lambda b, i: (b, 0)),
        ],
        out_specs=pl.BlockSpec((tq, n_q), lambda b, i: (b * q_tiles + i, 0)),
        out_shape=jax.ShapeDtypeStruct((batch * seq, n_q), BF16),
        compiler_params=_cparams(("parallel", "arbitrary")),
        name="window_attention",
    )(sink, q_all, k_all, v_all, k_all, v_all)


def _pool_kernel(u_ref, w_ref, sc_ref, o_ref, pad_ref, *, length):
    halo = POOL_HALO
    ch = POOL_CHUNK
    zeros = jnp.zeros((halo, POOL_DIM), F32)
    pad_ref[0:halo, :] = zeros
    pad_ref[halo + length:halo + length + halo, :] = zeros
    pad_ref[halo:halo + length, :] = u_ref[...]
    lane = lax.broadcasted_iota(jnp.int32, (ch, POOL_DIM), 1)
    grp = lane // POOL_GROUP_DIM
    half_w = jnp.left_shift(1, grp)
    row0 = lax.broadcasted_iota(jnp.int32, (ch, POOL_DIM), 0)
    win_rows = ch + 2 * halo

    def chunk(c, carry):
        r0 = pl.multiple_of(c * ch, ch)
        win = pad_ref[pl.ds(r0, win_rows), :]
        acc = jnp.zeros((ch, POOL_DIM), F32)
        for dlt in range(-halo, halo):
            shifted = pltpu.roll(win, win_rows - (halo + dlt), 0)[:ch] if halo + dlt else win[:ch]
            inside = (half_w >= -dlt) if dlt < 0 else (half_w > dlt)
            acc = acc + jnp.where(inside, shifted, 0.0)
        t = row0 + r0
        hi = jnp.minimum(t + half_w, length)
        lo = jnp.maximum(t - half_w, 0)
        centre = win[halo:halo + ch]
        pooled = acc / (hi - lo).astype(F32) - centre
        mixed = _dot(pooled.astype(BF16), w_ref[...]) * sc_ref[...]
        o_ref[pl.ds(r0, ch), :] = mixed.astype(o_ref.dtype)
        return carry

    lax.fori_loop(0, length // ch, chunk, 0)


def _pool_mixer(u_all, w_blockdiag, scale, row0, n_seq, length):
    blk0 = row0 // length
    return pl.pallas_call(
        functools.partial(_pool_kernel, length=length),
        grid=(n_seq,),
        in_specs=[
            pl.BlockSpec((length, POOL_DIM), lambda s: (blk0 + s, 0)),
            pl.BlockSpec((POOL_DIM, POOL_DIM), lambda s: (0, 0)),
            pl.BlockSpec((1, POOL_DIM), lambda s: (0, 0)),
        ],
        out_specs=pl.BlockSpec((length, POOL_DIM), lambda s: (s, 0)),
        out_shape=jax.ShapeDtypeStruct((n_seq * length, POOL_DIM), BF16),
        scratch_shapes=[pltpu.VMEM((length + 2 * POOL_HALO, POOL_DIM), F32)],
        compiler_params=_cparams(("parallel",)),
        name="pool_mixer",
    )(u_all, w_blockdiag, scale)


def _outproj_kernel(x_ref, mod_ref, *rest, n_pool):
    if n_pool:
        m_ref, a_ref, w_ref, o_ref = rest
        y = _dot(m_ref[...], w_ref[:n_pool, :]) + _dot(a_ref[...], w_ref[n_pool:, :])
    else:
        a_ref, w_ref, o_ref = rest
        y = _dot(a_ref[...], w_ref[...])
    o_ref[...] = x_ref[...] + mod_ref[0, 2:3, :] * y


def _out_projection(x_all, mod, mixed, attn, w_bf16, seq, n_groups, n_rows):
    d = x_all.shape[1]
    tm = ROW_TILE
    n_pool = 0 if mixed is None else mixed.shape[1]
    n_att = attn.shape[1]

    def grp(i):
        return (jnp.minimum((i * tm) // seq, n_groups), 0, 0)

    in_specs = [pl.BlockSpec((tm, d), lambda i: (i, 0)), pl.BlockSpec((1, 6, d), grp)]
    args = [x_all, mod]
    if n_pool:
        in_specs.append(pl.BlockSpec((tm, n_pool), lambda i: (i, 0)))
        args.append(mixed)
    in_specs += [pl.BlockSpec((tm, n_att), lambda i: (i, 0)),
                 pl.BlockSpec((n_pool + n_att, d), lambda i: (0, 0))]
    args += [attn, w_bf16]
    return pl.pallas_call(
        functools.partial(_outproj_kernel, n_pool=n_pool),
        grid=(n_rows // tm,),
        in_specs=in_specs,
        out_specs=pl.BlockSpec((tm, d), lambda i: (i, 0)),
        out_shape=jax.ShapeDtypeStruct((n_rows, d), F32),
        compiler_params=_cparams(("parallel",)),
        name="out_projection",
    )(*args)


def _oddeven_merge(lo, hi, r):
    step = r * 2
    if step < hi - lo:
        yield from _oddeven_merge(lo, hi, step)
        yield from _oddeven_merge(lo + r, hi, step)
        yield from [(i, i + r) for i in range(lo + r, hi - r, step)]
    else:
        yield (lo, lo + r)


def _oddeven_sort(lo, hi):
    if hi - lo >= 1:
        mid = lo + (hi - lo) // 2
        yield from _oddeven_sort(lo, mid)
        yield from _oddeven_sort(mid + 1, hi)
        yield from _oddeven_merge(lo, hi, 1)


_SORT16 = tuple(_oddeven_sort(0, 15))
_SORT8 = tuple(_oddeven_sort(0, 7))
_TOP_PAIRS = tuple((r, c) for r in range(PEER_TOPK) for c in range(PEER_TOPK) if (r + 1) * (c + 1) <= PEER_TOPK)


def _cmpx(xs, i, j):
    a, b = xs[i], xs[j]
    xs[i] = jnp.maximum(a, b)
    xs[j] = jnp.minimum(a, b)


def _sort_desc(xs, net):
    xs = list(xs)
    for i, j in net:
        _cmpx(xs, i, j)
    return xs


def _bitonic_merge_desc(xs):
    xs = list(xs)
    n = len(xs)
    d = n // 2
    while d:
        for i in range(n):
            if not i & d:
                _cmpx(xs, i, i + d)
        d //= 2
    return xs


def _merge_top(a, b):
    n = len(a)
    return _bitonic_merge_desc([jnp.maximum(a[i], b[n - 1 - i]) for i in range(n)])


def _top16_sorted(rows):
    w = _sort_desc(rows, _SORT16)
    for shift in (4, 2, 1):
        w = _merge_top(w, [pltpu.roll(x, shift, 0) for x in w])
    return w


def _pair_threshold(pair):
    low = jnp.full_like(pair[(0, 0)], LOWEST)
    row0 = [pair[(0, c)] for c in range(16)]
    col0 = [pair[(r, 0)] for r in range(1, 16)] + [low]
    t01 = _merge_top(row0, col0)
    g2 = _bitonic_merge_desc([pair[(1, c)] for c in range(1, 8)] + [low] * 3
                             + [pair[(r, 1)] for r in range(7, 1, -1)])
    rest = [pair[k] for k in ((2, 2), (2, 3), (2, 4), (3, 2), (4, 2), (3, 3))]
    g3 = _sort_desc(rest + [low, low], _SORT8) + [low] * 8
    t23 = _merge_top(g2, g3)
    tau = None
    for i in range(16):
        m = jnp.maximum(t01[i], t23[15 - i])
        tau = m if tau is None else jnp.minimum(tau, m)
    return tau


def _route_tile(s1, s2):
    assert PEER_TOPK == 16 and PEER_NKEYS == 128
    n_col = s1.shape[1] // LANES
    assert 8 % n_col == 0
    span = 8 // n_col
    cols = []
    for j in range(n_col):
        c1 = s1[:, j * LANES:(j + 1) * LANES]
        c2 = s2[:, j * LANES:(j + 1) * LANES]
        rows1 = [c1[8 * g:8 * g + 8] for g in range(16)]
        rows2 = [c2[8 * g:8 * g + 8] for g in range(16)]
        cols.append((rows1, rows2, _top16_sorted(rows1), _top16_sorted(rows2)))

    sub = lax.broadcasted_iota(jnp.int32, (8, LANES), 0)

    def pack(vals):
        out = vals[-1]
        for j in range(n_col - 2, -1, -1):
            out = jnp.where(sub < (j + 1) * span, vals[j], out)
        return out

    def unpack(x, j):
        return jnp.broadcast_to(x[j * span:j * span + 1, :], (8, LANES))

    v1 = [pack([c[2][r] for c in cols]) for r in range(16)]
    v2 = [pack([c[3][r] for c in cols]) for r in range(16)]
    pair = {(r, c): v1[r] + v2[c] for (r, c) in _TOP_PAIRS}
    tau = _pair_threshold(pair)
    e1 = [jnp.exp(v - v1[0]) for v in v1]
    e2 = [jnp.exp(v - v2[0]) for v in v2]
    z = None
    for (r, c) in _TOP_PAIRS:
        term = jnp.where(pair[(r, c)] >= tau, e1[r] * e2[c], 0.0)
        z = term if z is None else z + term
    inv_z = (1.0 / math.sqrt(2.0)) / z
    n_top = jnp.zeros_like(tau)
    for c in range(16):
        n_top = jnp.where(pair[(0, c)] >= tau, c + 1.0, n_top)

    out = [[], [], [], []]
    for j, (rows1, rows2, t1, t2) in enumerate(cols):
        tau_j, inv_z_j, n_top_j = unpack(tau, j), unpack(inv_z, j), unpack(n_top, j)
        cnt, p1n, rank2, p2 = [], [], [], []
        for g in range(16):
            a = rows1[g]
            n = jnp.zeros_like(a)
            for c in range(8):
                n = jnp.where(a + t2[c] >= tau_j, c + 1.0, n)
            cnt.append(jnp.where(a >= t1[0], n_top_j, n))
            p1n.append(jnp.exp(a - t1[0]) * inv_z_j)
            b = rows2[g]
            k = jnp.zeros_like(b)
            for r in range(16):
                k = jnp.where(t2[r] > b, r + 1.0, k)
            rank2.append(k)
            p2.append(jnp.exp(b - t2[0]))
        for dst, parts in zip(out, (cnt, p1n, rank2, p2)):
            dst.append(jnp.concatenate(parts, axis=0))
    return tuple(jnp.concatenate(parts, axis=1) for parts in out)


def _peer_kernel(x_ref, mod_ref, g_ref, wq_ref, k1_ref, k2_ref, ut_ref, vt_ref, *rest, final_norm, n_blocks):
    if final_norm:
        fg_ref, o_ref, *scratch = rest
    else:
        o_ref, *scratch = rest
    hb_ref, cnt_ref, p1_ref, rank_ref, p2_ref, gw0_ref, gw1_ref, acc_ref = scratch
    s = pl.program_id(1)
    eb, t = gw0_ref.shape
    blocks = eb // PEER_NKEYS
    half = PEER_DKEY // 2

    @pl.when(s == 0)
    def _route():
        x = x_ref[...]
        h2 = _norm_mod(x, g_ref[...], mod_ref[0, 3:4, :], mod_ref[0, 4:5, :])
        hb_ref[...] = h2.astype(BF16)
        acc_ref[...] = jnp.zeros_like(acc_ref)

        def head(h, carry):
            qh = _dot(hb_ref[...], wq_ref[h])
            s1 = _dot_nt(k1_ref[...], qh[:, :half].astype(BF16))
            s2 = _dot_nt(k2_ref[...], qh[:, half:].astype(BF16))
            for j in range(t // LANES):
                col = slice(j * LANES, (j + 1) * LANES)
                cnt, p1n, rank2, p2 = _route_tile(s1[:, col], s2[:, col])
                cnt_ref[h, :, col] = cnt
                p1_ref[h, :, col] = p1n
                rank_ref[h, :, col] = rank2.astype(BF16)
                p2_ref[h, :, col] = p2.astype(BF16)
            return carry

        lax.fori_loop(0, PEER_HEADS, head, 0)

    def stage(parity, build, consume):
        gw_w, gw_r = (gw0_ref, gw1_ref) if parity == 0 else (gw1_ref, gw0_ref)
        reps = PEER_NKEYS // 16
        span = 2 * PEER_NKEYS
        n_pairs = blocks // 2

        def build_pair(pair):
            y_pair = _dot(hb_ref[...], ut_ref[:, pair * span:(pair + 1) * span]).T
            for i in (2 * pair, 2 * pair + 1):
                a = s * blocks + i
                w = None
                for h in range(PEER_HEADS):
                    cnt16 = jnp.broadcast_to(cnt_ref[h, pl.ds(a, 1), :], (16, t)).astype(BF16)
                    p16 = jnp.broadcast_to(p1_ref[h, pl.ds(a, 1), :], (16, t)).astype(BF16)
                    sel = rank_ref[h] < jnp.concatenate([cnt16] * reps, axis=0)
                    term = (jnp.where(sel, p2_ref[h], jnp.zeros((), BF16))
                            * jnp.concatenate([p16] * reps, axis=0))
                    w = term if w is None else w + term
                y = y_pair[(i % 2) * PEER_NKEYS:(i % 2 + 1) * PEER_NKEYS, :]
                act = y * (1.0 + lax.erf(y))
                gw_w[i * PEER_NKEYS:(i + 1) * PEER_NKEYS, :] = act.astype(BF16) * w

        if build:
            for k in range(n_pairs):
                build_pair(k)
        if consume:
            acc_ref[...] += _dot(vt_ref[...], gw_r[...])

    pl.when(s == 0)(functools.partial(stage, 0, True, False))
    steady = (s >= 1) & (s < n_blocks)
    pl.when(steady & (s % 2 == 0))(functools.partial(stage, 0, True, True))
    pl.when(steady & (s % 2 == 1))(functools.partial(stage, 1, True, True))
    pl.when(s == n_blocks)(functools.partial(stage, n_blocks % 2, False, True))

    @pl.when(s == n_blocks)
    def _finish():
        y = x_ref[...] + mod_ref[0, 5:6, :] * acc_ref[...].T
        if final_norm:
            y = y * lax.rsqrt(jnp.mean(y * y, axis=-1, keepdims=True) + EPS) * fg_ref[...]
        o_ref[...] = y


def _peer(x_all, mod, gain, wq_heads, keys1, keys2, ut_bf16, vt_bf16, seq, n_groups, n_rows, final_gain):
    d = x_all.shape[1]
    t = PEER_TOKENS
    eb = PEER_EXPERT_BLOCK
    n_blocks = ut_bf16.shape[1] // eb

    def grp(i, e):
        return (jnp.minimum((i * t) // seq, n_groups), 0, 0)

    in_specs = [
        pl.BlockSpec((t, d), lambda i, e: (i, 0)),
        pl.BlockSpec((1, 6, d), grp),
        pl.BlockSpec((1, d), lambda i, e: (0, 0)),
        pl.BlockSpec(wq_heads.shape, lambda i, e: (0, 0, 0)),
        pl.BlockSpec(keys1.shape, lambda i, e: (0, 0)),
        pl.BlockSpec(keys2.shape, lambda i, e: (0, 0)),
        pl.BlockSpec((d, eb), lambda i, e: (0, jnp.minimum(e, n_blocks - 1))),
        pl.BlockSpec((d, eb), lambda i, e: (0, jnp.maximum(e - 1, 0))),
    ]
    args = [x_all, mod, gain, wq_heads, keys1, keys2, ut_bf16, vt_bf16]
    if final_gain is not None:
        in_specs.append(pl.BlockSpec((1, d), lambda i, e: (0, 0)))
        args.append(final_gain)
    per_head = (PEER_HEADS, PEER_NKEYS, t)
    return pl.pallas_call(
        functools.partial(_peer_kernel, final_norm=final_gain is not None, n_blocks=n_blocks),
        grid=(n_rows // t, n_blocks + 1),
        in_specs=in_specs,
        out_specs=pl.BlockSpec((t, d), lambda i, e: (i, 0)),
        out_shape=jax.ShapeDtypeStruct((n_rows, d), F32),
        scratch_shapes=[
            pltpu.VMEM((t, d), BF16),
            pltpu.VMEM(per_head, F32),
            pltpu.VMEM(per_head, F32),
            pltpu.VMEM(per_head, BF16),
            pltpu.VMEM(per_head, BF16),
            pltpu.VMEM((eb, t), BF16),
            pltpu.VMEM((eb, t), BF16),
            pltpu.VMEM((d, t), F32),
        ],
        compiler_params=_cparams(("parallel", "arbitrary")),
        name="peer",
    )(*args)


def _rope_tables(seq, pad_rows):
    half = HEAD_DIM // 2
    t = jnp.arange(seq, dtype=jnp.int32)
    row = (t // GRID_W).astype(F32)
    col = (t % GRID_W).astype(F32)
    inv = 1.0 / (ROPE_THETA ** (jnp.arange(0, half, 2, dtype=F32) / half))
    ang_r = row[:, None] * inv[None, :]
    ang_c = col[:, None] * inv[None, :]
    cos_h = jnp.concatenate([jnp.cos(ang_r)] * 2 + [jnp.cos(ang_c)] * 2, axis=1)
    sin_h = jnp.concatenate([-jnp.sin(ang_r), jnp.sin(ang_r), -jnp.sin(ang_c), jnp.sin(ang_c)], axis=1)
    cos_t = jnp.concatenate([jnp.tile(cos_h, (1, LANES // HEAD_DIM)), jnp.ones((pad_rows, LANES), F32)], axis=0)
    sin_t = jnp.concatenate([jnp.tile(sin_h, (1, LANES // HEAD_DIM)), jnp.zeros((pad_rows, LANES), F32)], axis=0)
    return cos_t, sin_t


def _head_norm_consts(q_gain, k_gain, n_q_heads, n_kv_heads):
    n_heads = n_q_heads + n_kv_heads
    head_of_lane = np.arange(n_heads * HEAD_DIM) // HEAD_DIM
    onehot = (head_of_lane[:, None] == np.arange(LANES)[None, :]).astype(np.float32)
    gsum = jnp.asarray(onehot / HEAD_DIM, BF16)
    gexp = jnp.asarray(onehot.T, BF16)
    hg = jnp.concatenate([jnp.tile(q_gain, n_q_heads), jnp.tile(k_gain, n_kv_heads)])[None, :]
    return gsum, gexp, hg


def kernel(x, c, ctx, c_ctx, ada_w, ada_b, norm1_g, norm2_g, final_g, e_w_in, e_q_norm_g, e_k_norm_g,
           e_pool_w, e_pool_scale, e_w_out, o_w_in, o_sink, o_w_out, p_w_q, p_keys1, p_keys2, p_u, p_v):
    batch, seq, d = x.shape
    ctx_len = ctx.shape[1]
    depth = ada_w.shape[0]
    n_lat = batch * seq
    n_all = n_lat + batch * ctx_len

    x_all = jnp.concatenate([x.reshape(n_lat, d), ctx.reshape(batch * ctx_len, d)], axis=0)
    cond = jnp.concatenate([c, c_ctx[None, :], jnp.zeros((8 - (batch + 1) % 8, d), F32)], axis=0)
    mods = _mod_vectors(cond, ada_w, ada_b)
    mods = mods.reshape(depth, cond.shape[0], 6, d)
    cos_t, sin_t = _rope_tables(seq, ROW_TILE)

    for i in range(depth):
        last = i == depth - 1
        mod = mods[i]
        j = i // 2
        g1 = norm1_g[i][None, :]
        g2 = norm2_g[i][None, :]
        if i % 2 == 0:
            consts = _head_norm_consts(e_q_norm_g[j], e_k_norm_g[j], B_Q_HEADS, B_KV_HEADS)
            u, q, k, v = _in_projection(
                x_all, mod, g1, e_w_in[j].astype(BF16), cos_t, sin_t, seq, batch,
                n_pre=POOL_DIM, n_q=B_Q_HEADS * HEAD_DIM, n_kv=B_KV_HEADS * HEAD_DIM, norm_consts=consts)
            w_pool = jax.scipy.linalg.block_diag(*[e_pool_w[j, g] for g in range(POOL_GROUPS)]).astype(BF16)
            p_scale = e_pool_scale[j][None, :]
            logit_bound = (NORM_BOUND_MARGIN * math.sqrt(HEAD_DIM) * jnp.max(jnp.abs(e_q_norm_g[j]))
                           * jnp.max(jnp.abs(e_k_norm_g[j]))).reshape(1)
            attn = _global_attention(logit_bound, q, k, v, batch, seq, ctx_len, latent=True)
            mixed = _pool_mixer(u, w_pool, p_scale, 0, batch, seq)
            if not last:
                attn = jnp.concatenate(
                    [attn, _global_attention(logit_bound, q, k, v, batch, seq, ctx_len, latent=False)], axis=0)
                mixed = jnp.concatenate([mixed, _pool_mixer(u, w_pool, p_scale, n_lat, batch, ctx_len)], axis=0)
            w_out = e_w_out[j].astype(BF16)
        else:
            _q = C_Q_HEADS * HEAD_DIM
            q, k, v = _in_projection(
                x_all, mod, g1, o_w_in[j].astype(BF16), cos_t, sin_t, seq, batch,
                n_pre=0, n_q=_q, n_kv=C_KV_HEADS * HEAD_DIM, norm_consts=None)
            attn = _window_attention(q, k, v, o_sink[j], batch, seq, ctx_len)
            if not last:
                raise NotImplementedError("context stream after an odd layer")
            mixed = None
            w_out = o_w_out[j].astype(BF16)
        n_rows = n_lat if last else n_all
        x_mid = _out_projection(x_all, mod, mixed, attn, w_out, seq, batch, n_rows)
        wq_heads = p_w_q[i].astype(BF16).reshape(d, PEER_HEADS, PEER_DKEY).transpose(1, 0, 2)
        x_all = _peer(x_mid, mod, g2, wq_heads, p_keys1[i].astype(BF16), p_keys2[i].astype(BF16),
                      (p_u[i] * (1.0 / math.sqrt(2.0))).astype(BF16).T, p_v[i].astype(BF16).T, seq, batch, n_rows,
                      final_g[None, :] if last else None)
    return x_all[:n_lat].reshape(batch, seq, d)
```

```python
import functools
import math

import jax
import jax.numpy as jnp
import numpy as np
from jax import lax
from jax.experimental import pallas as pl
from jax.experimental.pallas import tpu as pltpu

F32 = jnp.float32
BF16 = jnp.bfloat16

GRID_W = 64
HEAD_DIM = 64
ROPE_THETA = 10000.0
EPS = 1e-6
MASK_VALUE = -1e30
POOL_GROUPS = 4
POOL_GROUP_DIM = 64
POOL_DIM = POOL_GROUPS * POOL_GROUP_DIM
POOL_WINDOWS = (2, 4, 8, 16)
POOL_HALO = 8
B_Q_HEADS = 12
B_KV_HEADS = 4
C_Q_HEADS = 16
C_KV_HEADS = 4
WINDOW = 128
PEER_HEADS = 8
PEER_NKEYS = 128
PEER_DKEY = 256
PEER_TOPK = 16
LOWEST = -3.0e38
SAFE_LOGIT_BOUND = 40.0
NORM_BOUND_MARGIN = 1.05

LANES = 128
VMEM_LIMIT_BYTES = 56 * 1024 * 1024
ROW_TILE = 512
ATTN_Q_TILE = 512
ATTN_KV_TILE = 2048
WIN_Q_TILE = 256
WIN_BAND = WIN_Q_TILE + 2 * WINDOW
PEER_TOKENS = 512
PEER_EXPERT_BLOCK = 1024
POOL_CHUNK = 256


def _cparams(semantics):
    return pltpu.CompilerParams(dimension_semantics=semantics, vmem_limit_bytes=VMEM_LIMIT_BYTES)


def _dot(a, b):
    return jnp.dot(a, b, preferred_element_type=F32)


def _dot_nt(a, b):
    return lax.dot_general(a, b, (((1,), (1,)), ((), ())), preferred_element_type=F32)


def _split3(a):
    hi = a.astype(BF16)
    lo = (a - hi.astype(F32)).astype(BF16)
    return hi, lo


def _dot_f32ish(a, b):
    ah, al = _split3(a)
    bh, bl = _split3(b)
    return _dot(ah, bh) + (_dot(ah, bl) + _dot(al, bh))


def _norm_mod(x, gain, shift, scale):
    y = x * lax.rsqrt(jnp.mean(x * x, axis=-1, keepdims=True) + EPS)
    return (y * gain) * (1.0 + scale) + shift


def _mod_kernel(c_ref, w_ref, b_ref, o_ref):
    c = c_ref[...]
    act = c * (1.0 / (1.0 + jnp.exp(-c)))
    o_ref[0] = _dot_f32ish(act, w_ref[0]) + b_ref[0]


def _mod_vectors(cond, ada_w, ada_b):
    depth, d, d6 = ada_w.shape
    g8 = cond.shape[0]
    nblk = d6 // d
    return pl.pallas_call(
        _mod_kernel,
        grid=(depth, nblk),
        in_specs=[
            pl.BlockSpec((g8, d), lambda i, j: (0, 0)),
            pl.BlockSpec((1, d, d), lambda i, j: (i, 0, j)),
            pl.BlockSpec((1, 1, d), lambda i, j: (i, 0, j)),
        ],
        out_specs=pl.BlockSpec((1, g8, d), lambda i, j: (i, 0, j)),
        out_shape=jax.ShapeDtypeStruct((depth, g8, d6), F32),
        compiler_params=_cparams(("arbitrary", "arbitrary")),
        name="mod_vectors",
    )(cond, ada_w, ada_b.reshape(depth, 1, d6))


def _swap16(x):
    lane = lax.broadcasted_iota(jnp.int32, x.shape, 1)
    return jnp.where((lane & 16) == 0, pltpu.roll(x, LANES - 16, 1), pltpu.roll(x, 16, 1))


def _inproj_kernel(x_ref, mod_ref, g_ref, w_ref, cos_ref, sin_ref, *rest, n_pre, n_qk, qk_norm):
    if qk_norm:
        gsum_ref, gexp_ref, hg_ref, *outs = rest
    else:
        outs = rest
    x = x_ref[...]
    h = _norm_mod(x, g_ref[...], mod_ref[0, 0:1, :], mod_ref[0, 1:2, :])
    proj = _dot(h.astype(BF16), w_ref[...])
    qk = proj[:, n_pre:n_pre + n_qk]
    if qk_norm:
        ms = _dot((qk * qk).astype(BF16), gsum_ref[...])
        rinv = lax.rsqrt(ms + EPS)
        r_hi, r_lo = _split3(rinv)
        qk = qk * (_dot(r_hi, gexp_ref[...]) + _dot(r_lo, gexp_ref[...])) * hg_ref[...]
    cos = cos_ref[...]
    sin = sin_ref[...]
    blocks = []
    for j in range(n_qk // LANES):
        blk = qk[:, j * LANES:(j + 1) * LANES]
        blocks.append(blk * cos + _swap16(blk) * sin)
    if n_pre:
        u_ref, q_ref, k_ref, v_ref = outs
        u_ref[...] = proj[:, :n_pre]
    else:
        q_ref, k_ref, v_ref = outs
    n_q = q_ref.shape[1]
    scale = HEAD_DIM ** -0.5
    for j, blk in enumerate(blocks):
        lo = j * LANES
        if lo < n_q:
            q_ref[:, lo:lo + LANES] = (blk * scale).astype(BF16)
        else:
            k_ref[:, lo - n_q:lo - n_q + LANES] = blk.astype(BF16)
    v_ref[...] = proj[:, n_pre + n_qk:].astype(BF16)


def _in_projection(x_all, mod, gain, w_bf16, cos_t, sin_t, seq, n_groups, *, n_pre, n_q, n_kv, norm_consts):
    n, d = x_all.shape
    tm = ROW_TILE
    n_qk = n_q + n_kv
    d_in = w_bf16.shape[1]
    lat_tiles = (n_groups * seq) // tm
    seq_tiles = seq // tm

    def grp(i):
        return (jnp.minimum((i * tm) // seq, n_groups), 0, 0)

    def rope_idx(i):
        return (jnp.where(i < lat_tiles, i % seq_tiles, seq_tiles), 0)

    in_specs = [
        pl.BlockSpec((tm, d), lambda i: (i, 0)),
        pl.BlockSpec((1, 6, d), grp),
        pl.BlockSpec((1, d), lambda i: (0, 0)),
        pl.BlockSpec((d, d_in), lambda i: (0, 0)),
        pl.BlockSpec((tm, LANES), rope_idx),
        pl.BlockSpec((tm, LANES), rope_idx),
    ]
    args = [x_all, mod, gain, w_bf16, cos_t, sin_t]
    if norm_consts is not None:
        gsum, gexp, hg = norm_consts
        in_specs += [
            pl.BlockSpec(gsum.shape, lambda i: (0, 0)),
            pl.BlockSpec(gexp.shape, lambda i: (0, 0)),
            pl.BlockSpec(hg.shape, lambda i: (0, 0)),
        ]
        args += [gsum, gexp, hg]
    out_specs, out_shape = [], []
    if n_pre:
        out_specs.append(pl.BlockSpec((tm, n_pre), lambda i: (i, 0)))
        out_shape.append(jax.ShapeDtypeStruct((n, n_pre), F32))
    for width in (n_q, n_kv, n_kv):
        out_specs.append(pl.BlockSpec((tm, width), lambda i: (i, 0)))
        out_shape.append(jax.ShapeDtypeStruct((n, width), BF16))
    return pl.pallas_call(
        functools.partial(_inproj_kernel, n_pre=n_pre, n_qk=n_qk, qk_norm=norm_consts is not None),
        grid=(n // tm,),
        in_specs=in_specs,
        out_specs=out_specs,
        out_shape=out_shape,
        compiler_params=_cparams(("parallel",)),
        name="in_projection",
    )(*args)


def _head_query(q_ref, head, kv_head):
    blk = q_ref[:, (head // 2) * LANES:(head // 2 + 1) * LANES]
    if head % 2 != kv_head % 2:
        blk = jnp.concatenate([blk[:, HEAD_DIM:], blk[:, :HEAD_DIM]], axis=1)
    lane = lax.broadcasted_iota(jnp.int32, blk.shape, 1)
    keep = (lane >= HEAD_DIM) if kv_head % 2 else (lane < HEAD_DIM)
    return jnp.where(keep, blk, jnp.zeros_like(blk))


def _store_heads(o_ref, outs, kv_of_head):
    for p in range(len(outs) // 2):
        halves = []
        for h in (2 * p, 2 * p + 1):
            half = kv_of_head(h) % 2
            halves.append(outs[h][:, half * HEAD_DIM:(half + 1) * HEAD_DIM])
        o_ref[:, p * LANES:(p + 1) * LANES] = jnp.concatenate(halves, axis=1).astype(o_ref.dtype)


def _global_attn_kernel(bound_ref, q_ref, kc_ref, vc_ref, *rest, n_heads, group, n_lat_chunks):
    if n_lat_chunks:
        kl_ref, vl_ref, o_ref = rest
    else:
        (o_ref,) = rest
    tk = ATTN_KV_TILE

    def attend(bounded):
        outs = []
        for j in range(n_heads // group):
            cb = slice((j // 2) * LANES, (j // 2 + 1) * LANES)
            qzs = [_head_query(q_ref, h, j) for h in range(j * group, (j + 1) * group)]
            kc = kc_ref[:, cb]
            vc = vc_ref[:, cb]
            state = []
            for qz in qzs:
                s = _dot_nt(qz, kc)
                m = jnp.zeros((s.shape[0], 1), F32) if bounded else jnp.max(s, axis=-1, keepdims=True)
                p = jnp.exp(s) if bounded else jnp.exp(s - m)
                state.append((m, jnp.sum(p, axis=-1, keepdims=True), _dot(p.astype(BF16), vc)))

            if n_lat_chunks:
                def step(c, carry, qzs=qzs, cb=cb):
                    rows = pl.ds(pl.multiple_of(c * tk, tk), tk)
                    k = kl_ref[rows, cb]
                    v = vl_ref[rows, cb]
                    new = []
                    for qz, (m, l, o) in zip(qzs, carry):
                        s = _dot_nt(qz, k)
                        if bounded:
                            p = jnp.exp(s)
                            l = l + jnp.sum(p, axis=-1, keepdims=True)
                            o = o + _dot(p.astype(BF16), v)
                        else:
                            m_new = jnp.maximum(m, jnp.max(s, axis=-1, keepdims=True))
                            alpha = jnp.exp(m - m_new)
                            p = jnp.exp(s - m_new)
                            l = alpha * l + jnp.sum(p, axis=-1, keepdims=True)
                            o = alpha * o + _dot(p.astype(BF16), v)
                            m = m_new
                        new.append((m, l, o))
                    return tuple(new)

                state = lax.fori_loop(0, n_lat_chunks, step, tuple(state))
            outs.extend(o / l for (_, l, o) in state)
        _store_heads(o_ref, outs, lambda h: h // group)

    small = bound_ref[0] <= SAFE_LOGIT_BOUND
    pl.when(small)(functools.partial(attend, True))
    pl.when(jnp.logical_not(small))(functools.partial(attend, False))


def _global_attention(logit_bound, q_all, k_all, v_all, batch, seq, ctx_len, *, latent):
    n_q = q_all.shape[1]
    n_kv = k_all.shape[1]
    n_heads = n_q // HEAD_DIM
    group = n_heads // (n_kv // HEAD_DIM)
    ctx_blk0 = (batch * seq) // ctx_len
    if latent:
        tq = ATTN_Q_TILE
        q_tiles = seq // tq
        q_idx = lambda b, i: (b * q_tiles + i, 0)
        out_rows = batch * seq
    else:
        tq = ctx_len
        q_tiles = 1
        q_idx = lambda b, i: (ctx_blk0 + b, 0)
        out_rows = batch * ctx_len
    in_specs = [
        pl.BlockSpec(memory_space=pltpu.SMEM),
        pl.BlockSpec((tq, n_q), q_idx),
        pl.BlockSpec((ctx_len, n_kv), lambda b, i: (ctx_blk0 + b, 0)),
        pl.BlockSpec((ctx_len, n_kv), lambda b, i: (ctx_blk0 + b, 0)),
    ]
    args = [logit_bound, q_all, k_all, v_all]
    if latent:
        in_specs += [pl.BlockSpec((seq, n_kv), lambda b, i: (b, 0)),
                     pl.BlockSpec((seq, n_kv), lambda b, i: (b, 0))]
        args += [k_all, v_all]
    return pl.pallas_call(
        functools.partial(_global_attn_kernel, n_heads=n_heads, group=group,
                          n_lat_chunks=(seq // ATTN_KV_TILE) if latent else 0),
        grid=(batch, q_tiles),
        in_specs=in_specs,
        out_specs=pl.BlockSpec((tq, n_q), lambda b, i: (b * q_tiles + i, 0)),
        out_shape=jax.ShapeDtypeStruct((out_rows, n_q), BF16),
        compiler_params=_cparams(("parallel", "arbitrary")),
        name="global_attention" if latent else "context_attention",
    )(*args)


def _window_attn_kernel(sink_ref, q_ref, kc_ref, vc_ref, kl_ref, vl_ref, o_ref, *, n_heads, group, seq):
    tq = WIN_Q_TILE
    band = WIN_BAND
    start = pl.program_id(1) * tq
    k0 = jnp.clip(start - WINDOW, 0, seq - band)
    k0 = pl.multiple_of(k0, WINDOW)
    rows = pl.ds(k0, band)
    qpos = start + lax.broadcasted_iota(jnp.int32, (tq, band), 0)
    kpos = k0 + lax.broadcasted_iota(jnp.int32, (tq, band), 1)
    valid = jnp.abs(qpos - kpos) <= WINDOW
    outs = []
    for h in range(n_heads):
        j = h // group
        cb = slice((j // 2) * LANES, (j // 2 + 1) * LANES)
        qz = _head_query(q_ref, h, j)
        s_c = _dot_nt(qz, kc_ref[:, cb])
        s_b = jnp.where(valid, _dot_nt(qz, kl_ref[rows, cb]), MASK_VALUE)
        sink = sink_ref[h]
        m = jnp.maximum(jnp.maximum(jnp.max(s_c, axis=-1, keepdims=True),
                                    jnp.max(s_b, axis=-1, keepdims=True)), sink)
        e_c = jnp.exp(s_c - m)
        e_b = jnp.exp(s_b - m)
        denom = (jnp.sum(e_c, axis=-1, keepdims=True) + jnp.sum(e_b, axis=-1, keepdims=True)
                 + jnp.exp(sink - m))
        o = _dot(e_c.astype(BF16), vc_ref[:, cb]) + _dot(e_b.astype(BF16), vl_ref[rows, cb])
        outs.append(o / denom)
    _store_heads(o_ref, outs, lambda h: h // group)


def _window_attention(q_all, k_all, v_all, sink, batch, seq, ctx_len):
    n_q = q_all.shape[1]
    n_kv = k_all.shape[1]
    n_heads = n_q // HEAD_DIM
    group = n_heads // (n_kv // HEAD_DIM)
    tq = WIN_Q_TILE
    q_tiles = seq // tq
    ctx_blk0 = (batch * seq) // ctx_len
    return pl.pallas_call(
        functools.partial(_window_attn_kernel, n_heads=n_heads, group=group, seq=seq),
        grid=(batch, q_tiles),
        in_specs=[
            pl.BlockSpec(memory_space=pltpu.SMEM),
            pl.BlockSpec((tq, n_q), lambda b, i: (b * q_tiles + i, 0)),
            pl.BlockSpec((ctx_len, n_kv), lambda b, i: (ctx_blk0 + b, 0)),
            pl.BlockSpec((ctx_len, n_kv), lambda b, i: (ctx_blk0 + b, 0)),
            pl.BlockSpec((seq, n_kv), lambda b, i: (b, 0)),
            pl.BlockSpec((seq, n_kv), lambda b, i: (b, 0)),
        ],
        out_specs=pl.BlockSpec((tq, n_q), lambda b, i: (b * q_tiles + i, 0)),
        out_shape=jax.ShapeDtypeStruct((batch * seq, n_q), BF16),
        compiler_params=_cparams(("parallel", "arbitrary")),
        name="window_attention",
    )(sink, q_all, k_all, v_all, k_all, v_all)


def _pool_kernel(u_ref, w_ref, sc_ref, o_ref, pad_ref, *, length):
    halo = POOL_HALO
    ch = POOL_CHUNK
    zeros = jnp.zeros((halo, POOL_DIM), F32)
    pad_ref[0:halo, :] = zeros
    pad_ref[halo + length:halo + length + halo, :] = zeros
    pad_ref[halo:halo + length, :] = u_ref[...]
    lane = lax.broadcasted_iota(jnp.int32, (ch, POOL_DIM), 1)
    grp = lane // POOL_GROUP_DIM
    half_w = jnp.left_shift(1, grp)
    row0 = lax.broadcasted_iota(jnp.int32, (ch, POOL_DIM), 0)
    win_rows = ch + 2 * halo

    def chunk(c, carry):
        r0 = pl.multiple_of(c * ch, ch)
        win = pad_ref[pl.ds(r0, win_rows), :]
        acc = jnp.zeros((ch, POOL_DIM), F32)
        for dlt in range(-halo, halo):
            shifted = pltpu.roll(win, win_rows - (halo + dlt), 0)[:ch] if halo + dlt else win[:ch]
            inside = (half_w >= -dlt) if dlt < 0 else (half_w > dlt)
            acc = acc + jnp.where(inside, shifted, 0.0)
        t = row0 + r0
        hi = jnp.minimum(t + half_w, length)
        lo = jnp.maximum(t - half_w, 0)
        centre = win[halo:halo + ch]
        pooled = acc / (hi - lo).astype(F32) - centre
        mixed = _dot(pooled.astype(BF16), w_ref[...]) * sc_ref[...]
        o_ref[pl.ds(r0, ch), :] = mixed.astype(o_ref.dtype)
        return carry

    lax.fori_loop(0, length // ch, chunk, 0)


def _pool_mixer(u_all, w_blockdiag, scale, row0, n_seq, length):
    blk0 = row0 // length
    return pl.pallas_call(
        functools.partial(_pool_kernel, length=length),
        grid=(n_seq,),
        in_specs=[
            pl.BlockSpec((length, POOL_DIM), lambda s: (blk0 + s, 0)),
            pl.BlockSpec((POOL_DIM, POOL_DIM), lambda s: (0, 0)),
            pl.BlockSpec((1, POOL_DIM), lambda s: (0, 0)),
        ],
        out_specs=pl.BlockSpec((length, POOL_DIM), lambda s: (s, 0)),
        out_shape=jax.ShapeDtypeStruct((n_seq * length, POOL_DIM), BF16),
        scratch_shapes=[pltpu.VMEM((length + 2 * POOL_HALO, POOL_DIM), F32)],
        compiler_params=_cparams(("parallel",)),
        name="pool_mixer",
    )(u_all, w_blockdiag, scale)


def _outproj_kernel(x_ref, mod_ref, *rest, n_pool):
    if n_pool:
        m_ref, a_ref, w_ref, o_ref = rest
        y = _dot(m_ref[...], w_ref[:n_pool, :]) + _dot(a_ref[...], w_ref[n_pool:, :])
    else:
        a_ref, w_ref, o_ref = rest
        y = _dot(a_ref[...], w_ref[...])
    o_ref[...] = x_ref[...] + mod_ref[0, 2:3, :] * y


def _out_projection(x_all, mod, mixed, attn, w_bf16, seq, n_groups, n_rows):
    d = x_all.shape[1]
    tm = ROW_TILE
    n_pool = 0 if mixed is None else mixed.shape[1]
    n_att = attn.shape[1]

    def grp(i):
        return (jnp.minimum((i * tm) // seq, n_groups), 0, 0)

    in_specs = [pl.BlockSpec((tm, d), lambda i: (i, 0)), pl.BlockSpec((1, 6, d), grp)]
    args = [x_all, mod]
    if n_pool:
        in_specs.append(pl.BlockSpec((tm, n_pool), lambda i: (i, 0)))
        args.append(mixed)
    in_specs += [pl.BlockSpec((tm, n_att), lambda i: (i, 0)),
                 pl.BlockSpec((n_pool + n_att, d), lambda i: (0, 0))]
    args += [attn, w_bf16]
    return pl.pallas_call(
        functools.partial(_outproj_kernel, n_pool=n_pool),
        grid=(n_rows // tm,),
        in_specs=in_specs,
        out_specs=pl.BlockSpec((tm, d), lambda i: (i, 0)),
        out_shape=jax.ShapeDtypeStruct((n_rows, d), F32),
        compiler_params=_cparams(("parallel",)),
        name="out_projection",
    )(*args)


def _oddeven_merge(lo, hi, r):
    step = r * 2
    if step < hi - lo:
        yield from _oddeven_merge(lo, hi, step)
        yield from _oddeven_merge(lo + r, hi, step)
        yield from [(i, i + r) for i in range(lo + r, hi - r, step)]
    else:
        yield (lo, lo + r)


def _oddeven_sort(lo, hi):
    if hi - lo >= 1:
        mid = lo + (hi - lo) // 2
        yield from _oddeven_sort(lo, mid)
        yield from _oddeven_sort(mid + 1, hi)
        yield from _oddeven_merge(lo, hi, 1)


_SORT16 = tuple(_oddeven_sort(0, 15))
_SORT8 = tuple(_oddeven_sort(0, 7))
_TOP_PAIRS = tuple((r, c) for r in range(PEER_TOPK) for c in range(PEER_TOPK) if (r + 1) * (c + 1) <= PEER_TOPK)


def _cmpx(xs, i, j):
    a, b = xs[i], xs[j]
    xs[i] = jnp.maximum(a, b)
    xs[j] = jnp.minimum(a, b)


def _sort_desc(xs, net):
    xs = list(xs)
    for i, j in net:
        _cmpx(xs, i, j)
    return xs


def _bitonic_merge_desc(xs):
    xs = list(xs)
    n = len(xs)
    d = n // 2
    while d:
        for i in range(n):
            if not i & d:
                _cmpx(xs, i, i + d)
        d //= 2
    return xs


def _merge_top(a, b):
    n = len(a)
    return _bitonic_merge_desc([jnp.maximum(a[i], b[n - 1 - i]) for i in range(n)])


def _top16_sorted(rows):
    w = _sort_desc(rows, _SORT16)
    for shift in (4, 2, 1):
        w = _merge_top(w, [pltpu.roll(x, shift, 0) for x in w])
    return w


def _pair_threshold(pair):
    low = jnp.full_like(pair[(0, 0)], LOWEST)
    row0 = [pair[(0, c)] for c in range(16)]
    col0 = [pair[(r, 0)] for r in range(1, 16)] + [low]
    t01 = _merge_top(row0, col0)
    g2 = _bitonic_merge_desc([pair[(1, c)] for c in range(1, 8)] + [low] * 3
                             + [pair[(r, 1)] for r in range(7, 1, -1)])
    rest = [pair[k] for k in ((2, 2), (2, 3), (2, 4), (3, 2), (4, 2), (3, 3))]
    g3 = _sort_desc(rest + [low, low], _SORT8) + [low] * 8
    t23 = _merge_top(g2, g3)
    tau = None
    for i in range(16):
        m = jnp.maximum(t01[i], t23[15 - i])
        tau = m if tau is None else jnp.minimum(tau, m)
    return tau


def _route_tile(s1, s2):
    assert PEER_TOPK == 16 and PEER_NKEYS == 128
    n_col = s1.shape[1] // LANES
    assert 8 % n_col == 0
    span = 8 // n_col
    cols = []
    for j in range(n_col):
        c1 = s1[:, j * LANES:(j + 1) * LANES]
        c2 = s2[:, j * LANES:(j + 1) * LANES]
        rows1 = [c1[8 * g:8 * g + 8] for g in range(16)]
        rows2 = [c2[8 * g:8 * g + 8] for g in range(16)]
        cols.append((rows1, rows2, _top16_sorted(rows1), _top16_sorted(rows2)))

    sub = lax.broadcasted_iota(jnp.int32, (8, LANES), 0)

    def pack(vals):
        out = vals[-1]
        for j in range(n_col - 2, -1, -1):
            out = jnp.where(sub < (j + 1) * span, vals[j], out)
        return out

    def unpack(x, j):
        return jnp.broadcast_to(x[j * span:j * span + 1, :], (8, LANES))

    v1 = [pack([c[2][r] for c in cols]) for r in range(16)]
    v2 = [pack([c[3][r] for c in cols]) for r in range(16)]
    pair = {(r, c): v1[r] + v2[c] for (r, c) in _TOP_PAIRS}
    tau = _pair_threshold(pair)
    e1 = [jnp.exp(v - v1[0]) for v in v1]
    e2 = [jnp.exp(v - v2[0]) for v in v2]
    z = None
    for (r, c) in _TOP_PAIRS:
        term = jnp.where(pair[(r, c)] >= tau, e1[r] * e2[c], 0.0)
        z = term if z is None else z + term
    inv_z = (1.0 / math.sqrt(2.0)) / z
    n_top = jnp.zeros_like(tau)
    for c in range(16):
        n_top = jnp.where(pair[(0, c)] >= tau, c + 1.0, n_top)

    out = [[], [], [], []]
    for j, (rows1, rows2, t1, t2) in enumerate(cols):
        tau_j, inv_z_j, n_top_j = unpack(tau, j), unpack(inv_z, j), unpack(n_top, j)
        cnt, p1n, rank2, p2 = [], [], [], []
        for g in range(16):
            a = rows1[g]
            n = jnp.zeros_like(a)
            for c in range(8):
                n = jnp.where(a + t2[c] >= tau_j, c + 1.0, n)
            cnt.append(jnp.where(a >= t1[0], n_top_j, n))
            p1n.append(jnp.exp(a - t1[0]) * inv_z_j)
            b = rows2[g]
            k = jnp.zeros_like(b)
            for r in range(16):
                k = jnp.where(t2[r] > b, r + 1.0, k)
            rank2.append(k)
            p2.append(jnp.exp(b - t2[0]))
        for dst, parts in zip(out, (cnt, p1n, rank2, p2)):
            dst.append(jnp.concatenate(parts, axis=0))
    return tuple(jnp.concatenate(parts, axis=1) for parts in out)


def _peer_kernel(x_ref, mod_ref, g_ref, wq_ref, k1_ref, k2_ref, ut_ref, vt_ref, *rest, final_norm, n_blocks):
    if final_norm:
        fg_ref, o_ref, *scratch = rest
    else:
        o_ref, *scratch = rest
    hb_ref, cnt_ref, p1_ref, rank_ref, p2_ref, gw0_ref, gw1_ref, acc_ref = scratch
    s = pl.program_id(1)
    eb, t = gw0_ref.shape
    blocks = eb // PEER_NKEYS
    half = PEER_DKEY // 2

    @pl.when(s == 0)
    def _route():
        x = x_ref[...]
        h2 = _norm_mod(x, g_ref[...], mod_ref[0, 3:4, :], mod_ref[0, 4:5, :])
        hb_ref[...] = h2.astype(BF16)
        acc_ref[...] = jnp.zeros_like(acc_ref)

        def head(h, carry):
            qh = _dot(hb_ref[...], wq_ref[h])
            s1 = _dot_nt(k1_ref[...], qh[:, :half].astype(BF16))
            s2 = _dot_nt(k2_ref[...], qh[:, half:].astype(BF16))
            cnt, p1n, rank2, p2 = _route_tile(s1, s2)
            cnt_ref[h] = cnt
            p1_ref[h] = p1n
            rank_ref[h] = rank2.astype(BF16)
            p2_ref[h] = p2.astype(BF16)
            return carry

        lax.fori_loop(0, PEER_HEADS, head, 0)

    def stage(parity, build, consume):
        gw_w, gw_r = (gw0_ref, gw1_ref) if parity == 0 else (gw1_ref, gw0_ref)
        reps = PEER_NKEYS // 16
        span = 2 * PEER_NKEYS
        n_pairs = blocks // 2

        def build_pair(pair):
            y_pair = _dot(hb_ref[...], ut_ref[:, pair * span:(pair + 1) * span]).T
            for i in (2 * pair, 2 * pair + 1):
                a = s * blocks + i
                w = None
                for h in range(PEER_HEADS):
                    cnt16 = jnp.broadcast_to(cnt_ref[h, pl.ds(a, 1), :], (16, t)).astype(BF16)
                    p16 = jnp.broadcast_to(p1_ref[h, pl.ds(a, 1), :], (16, t)).astype(BF16)
                    sel = rank_ref[h] < jnp.concatenate([cnt16] * reps, axis=0)
                    term = (jnp.where(sel, p2_ref[h], jnp.zeros((), BF16))
                            * jnp.concatenate([p16] * reps, axis=0))
                    w = term if w is None else w + term
                y = y_pair[(i % 2) * PEER_NKEYS:(i % 2 + 1) * PEER_NKEYS, :]
                act = y * (1.0 + lax.erf(y))
                gw_w[i * PEER_NKEYS:(i + 1) * PEER_NKEYS, :] = act.astype(BF16) * w

        if build:
            for k in range(n_pairs):
                build_pair(k)
        if consume:
            acc_ref[...] += _dot(vt_ref[...], gw_r[...])

    pl.when(s == 0)(functools.partial(stage, 0, True, False))
    steady = (s >= 1) & (s < n_blocks)
    pl.when(steady & (s % 2 == 0))(functools.partial(stage, 0, True, True))
    pl.when(steady & (s % 2 == 1))(functools.partial(stage, 1, True, True))
    pl.when(s == n_blocks)(functools.partial(stage, n_blocks % 2, False, True))

    @pl.when(s == n_blocks)
    def _finish():
        y = x_ref[...] + mod_ref[0, 5:6, :] * acc_ref[...].T
        if final_norm:
            y = y * lax.rsqrt(jnp.mean(y * y, axis=-1, keepdims=True) + EPS) * fg_ref[...]
        o_ref[...] = y


def _peer(x_all, mod, gain, wq_heads, keys1, keys2, ut_bf16, vt_bf16, seq, n_groups, n_rows, final_gain):
    d = x_all.shape[1]
    t = PEER_TOKENS
    eb = PEER_EXPERT_BLOCK
    n_blocks = ut_bf16.shape[1] // eb

    def grp(i, e):
        return (jnp.minimum((i * t) // seq, n_groups), 0, 0)

    in_specs = [
        pl.BlockSpec((t, d), lambda i, e: (i, 0)),
        pl.BlockSpec((1, 6, d), grp),
        pl.BlockSpec((1, d), lambda i, e: (0, 0)),
        pl.BlockSpec(wq_heads.shape, lambda i, e: (0, 0, 0)),
        pl.BlockSpec(keys1.shape, lambda i, e: (0, 0)),
        pl.BlockSpec(keys2.shape, lambda i, e: (0, 0)),
        pl.BlockSpec((d, eb), lambda i, e: (0, jnp.minimum(e, n_blocks - 1))),
        pl.BlockSpec((d, eb), lambda i, e: (0, jnp.maximum(e - 1, 0))),
    ]
    args = [x_all, mod, gain, wq_heads, keys1, keys2, ut_bf16, vt_bf16]
    if final_gain is not None:
        in_specs.append(pl.BlockSpec((1, d), lambda i, e: (0, 0)))
        args.append(final_gain)
    per_head = (PEER_HEADS, PEER_NKEYS, t)
    return pl.pallas_call(
        functools.partial(_peer_kernel, final_norm=final_gain is not None, n_blocks=n_blocks),
        grid=(n_rows // t, n_blocks + 1),
        in_specs=in_specs,
        out_specs=pl.BlockSpec((t, d), lambda i, e: (i, 0)),
        out_shape=jax.ShapeDtypeStruct((n_rows, d), F32),
        scratch_shapes=[
            pltpu.VMEM((t, d), BF16),
            pltpu.VMEM(per_head, F32),
            pltpu.VMEM(per_head, F32),
            pltpu.VMEM(per_head, BF16),
            pltpu.VMEM(per_head, BF16),
            pltpu.VMEM((eb, t), BF16),
            pltpu.VMEM((eb, t), BF16),
            pltpu.VMEM((d, t), F32),
        ],
        compiler_params=_cparams(("parallel", "arbitrary")),
        name="peer",
    )(*args)


def _rope_tables(seq, pad_rows):
    half = HEAD_DIM // 2
    t = jnp.arange(seq, dtype=jnp.int32)
    row = (t // GRID_W).astype(F32)
    col = (t % GRID_W).astype(F32)
    inv = 1.0 / (ROPE_THETA ** (jnp.arange(0, half, 2, dtype=F32) / half))
    ang_r = row[:, None] * inv[None, :]
    ang_c = col[:, None] * inv[None, :]
    cos_h = jnp.concatenate([jnp.cos(ang_r)] * 2 + [jnp.cos(ang_c)] * 2, axis=1)
    sin_h = jnp.concatenate([-jnp.sin(ang_r), jnp.sin(ang_r), -jnp.sin(ang_c), jnp.sin(ang_c)], axis=1)
    cos_t = jnp.concatenate([jnp.tile(cos_h, (1, LANES // HEAD_DIM)), jnp.ones((pad_rows, LANES), F32)], axis=0)
    sin_t = jnp.concatenate([jnp.tile(sin_h, (1, LANES // HEAD_DIM)), jnp.zeros((pad_rows, LANES), F32)], axis=0)
    return cos_t, sin_t


def _head_norm_consts(q_gain, k_gain, n_q_heads, n_kv_heads):
    n_heads = n_q_heads + n_kv_heads
    head_of_lane = np.arange(n_heads * HEAD_DIM) // HEAD_DIM
    onehot = (head_of_lane[:, None] == np.arange(LANES)[None, :]).astype(np.float32)
    gsum = jnp.asarray(onehot / HEAD_DIM, BF16)
    gexp = jnp.asarray(onehot.T, BF16)
    hg = jnp.concatenate([jnp.tile(q_gain, n_q_heads), jnp.tile(k_gain, n_kv_heads)])[None, :]
    return gsum, gexp, hg


def kernel(x, c, ctx, c_ctx, ada_w, ada_b, norm1_g, norm2_g, final_g, e_w_in, e_q_norm_g, e_k_norm_g,
           e_pool_w, e_pool_scale, e_w_out, o_w_in, o_sink, o_w_out, p_w_q, p_keys1, p_keys2, p_u, p_v):
    batch, seq, d = x.shape
    ctx_len = ctx.shape[1]
    depth = ada_w.shape[0]
    n_lat = batch * seq
    n_all = n_lat + batch * ctx_len

    x_all = jnp.concatenate([x.reshape(n_lat, d), ctx.reshape(batch * ctx_len, d)], axis=0)
    cond = jnp.concatenate([c, c_ctx[None, :], jnp.zeros((8 - (batch + 1) % 8, d), F32)], axis=0)
    mods = _mod_vectors(cond, ada_w, ada_b)
    mods = mods.reshape(depth, cond.shape[0], 6, d)
    cos_t, sin_t = _rope_tables(seq, ROW_TILE)

    for i in range(depth):
        last = i == depth - 1
        mod = mods[i]
        j = i // 2
        g1 = norm1_g[i][None, :]
        g2 = norm2_g[i][None, :]
        if i % 2 == 0:
            consts = _head_norm_consts(e_q_norm_g[j], e_k_norm_g[j], B_Q_HEADS, B_KV_HEADS)
            u, q, k, v = _in_projection(
                x_all, mod, g1, e_w_in[j].astype(BF16), cos_t, sin_t, seq, batch,
                n_pre=POOL_DIM, n_q=B_Q_HEADS * HEAD_DIM, n_kv=B_KV_HEADS * HEAD_DIM, norm_consts=consts)
            w_pool = jax.scipy.linalg.block_diag(*[e_pool_w[j, g] for g in range(POOL_GROUPS)]).astype(BF16)
            p_scale = e_pool_scale[j][None, :]
            logit_bound = (NORM_BOUND_MARGIN * math.sqrt(HEAD_DIM) * jnp.max(jnp.abs(e_q_norm_g[j]))
                           * jnp.max(jnp.abs(e_k_norm_g[j]))).reshape(1)
            attn = _global_attention(logit_bound, q, k, v, batch, seq, ctx_len, latent=True)
            mixed = _pool_mixer(u, w_pool, p_scale, 0, batch, seq)
            if not last:
                attn = jnp.concatenate(
                    [attn, _global_attention(logit_bound, q, k, v, batch, seq, ctx_len, latent=False)], axis=0)
                mixed = jnp.concatenate([mixed, _pool_mixer(u, w_pool, p_scale, n_lat, batch, ctx_len)], axis=0)
            w_out = e_w_out[j].astype(BF16)
        else:
            _q = C_Q_HEADS * HEAD_DIM
            q, k, v = _in_projection(
                x_all, mod, g1, o_w_in[j].astype(BF16), cos_t, sin_t, seq, batch,
                n_pre=0, n_q=_q, n_kv=C_KV_HEADS * HEAD_DIM, norm_consts=None)
            attn = _window_attention(q, k, v, o_sink[j], batch, seq, ctx_len)
            if not last:
                raise NotImplementedError("context stream after an odd layer")
            mixed = None
            w_out = o_w_out[j].astype(BF16)
        n_rows = n_lat if last else n_all
        x_mid = _out_projection(x_all, mod, mixed, attn, w_out, seq, batch, n_rows)
        wq_heads = p_w_q[i].astype(BF16).reshape(d, PEER_HEADS, PEER_DKEY).transpose(1, 0, 2)
        x_all = _peer(x_mid, mod, g2, wq_heads, p_keys1[i].astype(BF16), p_keys2[i].astype(BF16),
                      (p_u[i] * (1.0 / math.sqrt(2.0))).astype(BF16).T, p_v[i].astype(BF16).T, seq, batch, n_rows,
                      final_g[None, :] if last else None)
    return x_all[:n_lat].reshape(batch, seq, d)
```

```python
import functools
import math

import jax
import jax.numpy as jnp
import numpy as np
from jax import lax
from jax.experimental import pallas as pl
from jax.experimental.pallas import tpu as pltpu

F32 = jnp.float32
BF16 = jnp.bfloat16

GRID_W = 64
HEAD_DIM = 64
ROPE_THETA = 10000.0
EPS = 1e-6
MASK_VALUE = -1e30
POOL_GROUPS = 4
POOL_GROUP_DIM = 64
POOL_DIM = POOL_GROUPS * POOL_GROUP_DIM
POOL_WINDOWS = (2, 4, 8, 16)
POOL_HALO = 8
B_Q_HEADS = 12
B_KV_HEADS = 4
C_Q_HEADS = 16
C_KV_HEADS = 4
WINDOW = 128
PEER_HEADS = 8
PEER_NKEYS = 128
PEER_DKEY = 256
PEER_TOPK = 16
LOWEST = -3.0e38
SAFE_LOGIT_BOUND = 40.0
NORM_BOUND_MARGIN = 1.05

LANES = 128
VMEM_LIMIT_BYTES = 56 * 1024 * 1024
ROW_TILE = 512
ATTN_Q_TILE = 512
ATTN_KV_TILE = 1024
WIN_Q_TILE = 256
WIN_BAND = WIN_Q_TILE + 2 * WINDOW
PEER_TOKENS = 512
PEER_EXPERT_BLOCK = 1024
POOL_CHUNK = 256


def _cparams(semantics):
    return pltpu.CompilerParams(dimension_semantics=semantics, vmem_limit_bytes=VMEM_LIMIT_BYTES)


def _dot(a, b):
    return jnp.dot(a, b, preferred_element_type=F32)


def _dot_nt(a, b):
    return lax.dot_general(a, b, (((1,), (1,)), ((), ())), preferred_element_type=F32)


def _split3(a):
    hi = a.astype(BF16)
    lo = (a - hi.astype(F32)).astype(BF16)
    return hi, lo


def _dot_f32ish(a, b):
    ah, al = _split3(a)
    bh, bl = _split3(b)
    return _dot(ah, bh) + (_dot(ah, bl) + _dot(al, bh))


def _norm_mod(x, gain, shift, scale):
    y = x * lax.rsqrt(jnp.mean(x * x, axis=-1, keepdims=True) + EPS)
    return (y * gain) * (1.0 + scale) + shift


def _mod_kernel(c_ref, w_ref, b_ref, o_ref):
    c = c_ref[...]
    act = c * (1.0 / (1.0 + jnp.exp(-c)))
    o_ref[0] = _dot_f32ish(act, w_ref[0]) + b_ref[0]


def _mod_vectors(cond, ada_w, ada_b):
    depth, d, d6 = ada_w.shape
    g8 = cond.shape[0]
    nblk = d6 // d
    return pl.pallas_call(
        _mod_kernel,
        grid=(depth, nblk),
        in_specs=[
            pl.BlockSpec((g8, d), lambda i, j: (0, 0)),
            pl.BlockSpec((1, d, d), lambda i, j: (i, 0, j)),
            pl.BlockSpec((1, 1, d), lambda i, j: (i, 0, j)),
        ],
        out_specs=pl.BlockSpec((1, g8, d), lambda i, j: (i, 0, j)),
        out_shape=jax.ShapeDtypeStruct((depth, g8, d6), F32),
        compiler_params=_cparams(("arbitrary", "arbitrary")),
        name="mod_vectors",
    )(cond, ada_w, ada_b.reshape(depth, 1, d6))


def _swap16(x):
    lane = lax.broadcasted_iota(jnp.int32, x.shape, 1)
    return jnp.where((lane & 16) == 0, pltpu.roll(x, LANES - 16, 1), pltpu.roll(x, 16, 1))


def _inproj_kernel(x_ref, mod_ref, g_ref, w_ref, cos_ref, sin_ref, *rest, n_pre, n_qk, qk_norm):
    if qk_norm:
        gsum_ref, gexp_ref, hg_ref, *outs = rest
    else:
        outs = rest
    x = x_ref[...]
    h = _norm_mod(x, g_ref[...], mod_ref[0, 0:1, :], mod_ref[0, 1:2, :])
    proj = _dot(h.astype(BF16), w_ref[...])
    qk = proj[:, n_pre:n_pre + n_qk]
    if qk_norm:
        ms = _dot((qk * qk).astype(BF16), gsum_ref[...])
        rinv = lax.rsqrt(ms + EPS)
        r_hi, r_lo = _split3(rinv)
        qk = qk * (_dot(r_hi, gexp_ref[...]) + _dot(r_lo, gexp_ref[...])) * hg_ref[...]
    cos = cos_ref[...]
    sin = sin_ref[...]
    blocks = []
    for j in range(n_qk // LANES):
        blk = qk[:, j * LANES:(j + 1) * LANES]
        blocks.append(blk * cos + _swap16(blk) * sin)
    if n_pre:
        u_ref, q_ref, k_ref, v_ref = outs
        u_ref[...] = proj[:, :n_pre]
    else:
        q_ref, k_ref, v_ref = outs
    n_q = q_ref.shape[1]
    scale = HEAD_DIM ** -0.5
    for j, blk in enumerate(blocks):
        lo = j * LANES
        if lo < n_q:
            q_ref[:, lo:lo + LANES] = (blk * scale).astype(BF16)
        else:
            k_ref[:, lo - n_q:lo - n_q + LANES] = blk.astype(BF16)
    v_ref[...] = proj[:, n_pre + n_qk:].astype(BF16)


def _in_projection(x_all, mod, gain, w_bf16, cos_t, sin_t, seq, n_groups, *, n_pre, n_q, n_kv, norm_consts):
    n, d = x_all.shape
    tm = ROW_TILE
    n_qk = n_q + n_kv
    d_in = w_bf16.shape[1]
    lat_tiles = (n_groups * seq) // tm
    seq_tiles = seq // tm

    def grp(i):
        return (jnp.minimum((i * tm) // seq, n_groups), 0, 0)

    def rope_idx(i):
        return (jnp.where(i < lat_tiles, i % seq_tiles, seq_tiles), 0)

    in_specs = [
        pl.BlockSpec((tm, d), lambda i: (i, 0)),
        pl.BlockSpec((1, 6, d), grp),
        pl.BlockSpec((1, d), lambda i: (0, 0)),
        pl.BlockSpec((d, d_in), lambda i: (0, 0)),
        pl.BlockSpec((tm, LANES), rope_idx),
        pl.BlockSpec((tm, LANES), rope_idx),
    ]
    args = [x_all, mod, gain, w_bf16, cos_t, sin_t]
    if norm_consts is not None:
        gsum, gexp, hg = norm_consts
        in_specs += [
            pl.BlockSpec(gsum.shape, lambda i: (0, 0)),
            pl.BlockSpec(gexp.shape, lambda i: (0, 0)),
            pl.BlockSpec(hg.shape, lambda i: (0, 0)),
        ]
        args += [gsum, gexp, hg]
    out_specs, out_shape = [], []
    if n_pre:
        out_specs.append(pl.BlockSpec((tm, n_pre), lambda i: (i, 0)))
        out_shape.append(jax.ShapeDtypeStruct((n, n_pre), F32))
    for width in (n_q, n_kv, n_kv):
        out_specs.append(pl.BlockSpec((tm, width), lambda i: (i, 0)))
        out_shape.append(jax.ShapeDtypeStruct((n, width), BF16))
    return pl.pallas_call(
        functools.partial(_inproj_kernel, n_pre=n_pre, n_qk=n_qk, qk_norm=norm_consts is not None),
        grid=(n // tm,),
        in_specs=in_specs,
        out_specs=out_specs,
        out_shape=out_shape,
        compiler_params=_cparams(("parallel",)),
        name="in_projection",
    )(*args)


def _head_query(q_ref, head, kv_head):
    blk = q_ref[:, (head // 2) * LANES:(head // 2 + 1) * LANES]
    if head % 2 != kv_head % 2:
        blk = jnp.concatenate([blk[:, HEAD_DIM:], blk[:, :HEAD_DIM]], axis=1)
    lane = lax.broadcasted_iota(jnp.int32, blk.shape, 1)
    keep = (lane >= HEAD_DIM) if kv_head % 2 else (lane < HEAD_DIM)
    return jnp.where(keep, blk, jnp.zeros_like(blk))


def _store_heads(o_ref, outs, kv_of_head):
    for p in range(len(outs) // 2):
        halves = []
        for h in (2 * p, 2 * p + 1):
            half = kv_of_head(h) % 2
            halves.append(outs[h][:, half * HEAD_DIM:(half + 1) * HEAD_DIM])
        o_ref[:, p * LANES:(p + 1) * LANES] = jnp.concatenate(halves, axis=1).astype(o_ref.dtype)


def _global_attn_kernel(bound_ref, q_ref, kc_ref, vc_ref, *rest, n_heads, group, n_lat_chunks):
    if n_lat_chunks:
        kl_ref, vl_ref, o_ref = rest
    else:
        (o_ref,) = rest
    tk = ATTN_KV_TILE

    def attend(bounded):
        outs = []
        for j in range(n_heads // group):
            cb = slice((j // 2) * LANES, (j // 2 + 1) * LANES)
            qzs = [_head_query(q_ref, h, j) for h in range(j * group, (j + 1) * group)]
            kc = kc_ref[:, cb]
            vc = vc_ref[:, cb]
            state = []
            for qz in qzs:
                s = _dot_nt(qz, kc)
                m = jnp.zeros((s.shape[0], 1), F32) if bounded else jnp.max(s, axis=-1, keepdims=True)
                p = jnp.exp(s) if bounded else jnp.exp(s - m)
                state.append((m, jnp.sum(p, axis=-1, keepdims=True), _dot(p.astype(BF16), vc)))

            if n_lat_chunks:
                def step(c, carry, qzs=qzs, cb=cb):
                    rows = pl.ds(pl.multiple_of(c * tk, tk), tk)
                    k = kl_ref[rows, cb]
                    v = vl_ref[rows, cb]
                    new = []
                    for qz, (m, l, o) in zip(qzs, carry):
                        s = _dot_nt(qz, k)
                        if bounded:
                            p = jnp.exp(s)
                            l = l + jnp.sum(p, axis=-1, keepdims=True)
                            o = o + _dot(p.astype(BF16), v)
                        else:
                            m_new = jnp.maximum(m, jnp.max(s, axis=-1, keepdims=True))
                            alpha = jnp.exp(m - m_new)
                            p = jnp.exp(s - m_new)
                            l = alpha * l + jnp.sum(p, axis=-1, keepdims=True)
                            o = alpha * o + _dot(p.astype(BF16), v)
                            m = m_new
                        new.append((m, l, o))
                    return tuple(new)

                state = lax.fori_loop(0, n_lat_chunks, step, tuple(state))
            outs.extend(o / l for (_, l, o) in state)
        _store_heads(o_ref, outs, lambda h: h // group)

    small = bound_ref[0] <= SAFE_LOGIT_BOUND
    pl.when(small)(functools.partial(attend, True))
    pl.when(jnp.logical_not(small))(functools.partial(attend, False))


def _global_attention(logit_bound, q_all, k_all, v_all, batch, seq, ctx_len, *, latent):
    n_q = q_all.shape[1]
    n_kv = k_all.shape[1]
    n_heads = n_q // HEAD_DIM
    group = n_heads // (n_kv // HEAD_DIM)
    ctx_blk0 = (batch * seq) // ctx_len
    if latent:
        tq = ATTN_Q_TILE
        q_tiles = seq // tq
        q_idx = lambda b, i: (b * q_tiles + i, 0)
        out_rows = batch * seq
    else:
        tq = ctx_len
        q_tiles = 1
        q_idx = lambda b, i: (ctx_blk0 + b, 0)
        out_rows = batch * ctx_len
    in_specs = [
        pl.BlockSpec(memory_space=pltpu.SMEM),
        pl.BlockSpec((tq, n_q), q_idx),
        pl.BlockSpec((ctx_len, n_kv), lambda b, i: (ctx_blk0 + b, 0)),
        pl.BlockSpec((ctx_len, n_kv), lambda b, i: (ctx_blk0 + b, 0)),
    ]
    args = [logit_bound, q_all, k_all, v_all]
    if latent:
        in_specs += [pl.BlockSpec((seq, n_kv), lambda b, i: (b, 0)),
                     pl.BlockSpec((seq, n_kv), lambda b, i: (b, 0))]
        args += [k_all, v_all]
    return pl.pallas_call(
        functools.partial(_global_attn_kernel, n_heads=n_heads, group=group,
                          n_lat_chunks=(seq // ATTN_KV_TILE) if latent else 0),
        grid=(batch, q_tiles),
        in_specs=in_specs,
        out_specs=pl.BlockSpec((tq, n_q), lambda b, i: (b * q_tiles + i, 0)),
        out_shape=jax.ShapeDtypeStruct((out_rows, n_q), BF16),
        compiler_params=_cparams(("parallel", "arbitrary")),
        name="global_attention" if latent else "context_attention",
    )(*args)


def _window_attn_kernel(sink_ref, q_ref, kc_ref, vc_ref, kl_ref, vl_ref, o_ref, *, n_heads, group, seq):
    tq = WIN_Q_TILE
    band = WIN_BAND
    start = pl.program_id(1) * tq
    k0 = jnp.clip(start - WINDOW, 0, seq - band)
    k0 = pl.multiple_of(k0, WINDOW)
    rows = pl.ds(k0, band)
    qpos = start + lax.broadcasted_iota(jnp.int32, (tq, band), 0)
    kpos = k0 + lax.broadcasted_iota(jnp.int32, (tq, band), 1)
    valid = jnp.abs(qpos - kpos) <= WINDOW
    outs = []
    for h in range(n_heads):
        j = h // group
        cb = slice((j // 2) * LANES, (j // 2 + 1) * LANES)
        qz = _head_query(q_ref, h, j)
        s_c = _dot_nt(qz, kc_ref[:, cb])
        s_b = jnp.where(valid, _dot_nt(qz, kl_ref[rows, cb]), MASK_VALUE)
        sink = sink_ref[h]
        m = jnp.maximum(jnp.maximum(jnp.max(s_c, axis=-1, keepdims=True),
                                    jnp.max(s_b, axis=-1, keepdims=True)), sink)
        e_c = jnp.exp(s_c - m)
        e_b = jnp.exp(s_b - m)
        denom = (jnp.sum(e_c, axis=-1, keepdims=True) + jnp.sum(e_b, axis=-1, keepdims=True)
                 + jnp.exp(sink - m))
        o = _dot(e_c.astype(BF16), vc_ref[:, cb]) + _dot(e_b.astype(BF16), vl_ref[rows, cb])
        outs.append(o / denom)
    _store_heads(o_ref, outs, lambda h: h // group)


def _window_attention(q_all, k_all, v_all, sink, batch, seq, ctx_len):
    n_q = q_all.shape[1]
    n_kv = k_all.shape[1]
    n_heads = n_q // HEAD_DIM
    group = n_heads // (n_kv // HEAD_DIM)
    tq = WIN_Q_TILE
    q_tiles = seq // tq
    ctx_blk0 = (batch * seq) // ctx_len
    return pl.pallas_call(
        functools.partial(_window_attn_kernel, n_heads=n_heads, group=group, seq=seq),
        grid=(batch, q_tiles),
        in_specs=[
            pl.BlockSpec(memory_space=pltpu.SMEM),
            pl.BlockSpec((tq, n_q), lambda b, i: (b * q_tiles + i, 0)),
            pl.BlockSpec((ctx_len, n_kv), lambda b, i: (ctx_blk0 + b, 0)),
            pl.BlockSpec((ctx_len, n_kv), lambda b, i: (ctx_blk0 + b, 0)),
            pl.BlockSpec((seq, n_kv), lambda b, i: (b, 0)),
            pl.BlockSpec((seq, n_kv), lambda b, i: (b, 0)),
        ],
        out_specs=pl.BlockSpec((tq, n_q), lambda b, i: (b * q_tiles + i, 0)),
        out_shape=jax.ShapeDtypeStruct((batch * seq, n_q), BF16),
        compiler_params=_cparams(("parallel", "arbitrary")),
        name="window_attention",
    )(sink, q_all, k_all, v_all, k_all, v_all)


def _pool_kernel(u_ref, w_ref, sc_ref, o_ref, pad_ref, *, length):
    halo = POOL_HALO
    ch = POOL_CHUNK
    zeros = jnp.zeros((halo, POOL_DIM), F32)
    pad_ref[0:halo, :] = zeros
    pad_ref[halo + length:halo + length + halo, :] = zeros
    pad_ref[halo:halo + length, :] = u_ref[...]
    lane = lax.broadcasted_iota(jnp.int32, (ch, POOL_DIM), 1)
    grp = lane // POOL_GROUP_DIM
    half_w = jnp.left_shift(1, grp)
    row0 = lax.broadcasted_iota(jnp.int32, (ch, POOL_DIM), 0)
    win_rows = ch + 2 * halo

    def chunk(c, carry):
        r0 = pl.multiple_of(c * ch, ch)
        win = pad_ref[pl.ds(r0, win_rows), :]
        acc = jnp.zeros((ch, POOL_DIM), F32)
        for dlt in range(-halo, halo):
            shifted = pltpu.roll(win, win_rows - (halo + dlt), 0)[:ch] if halo + dlt else win[:ch]
            inside = (half_w >= -dlt) if dlt < 0 else (half_w > dlt)
            acc = acc + jnp.where(inside, shifted, 0.0)
        t = row0 + r0
        hi = jnp.minimum(t + half_w, length)
        lo = jnp.maximum(t - half_w, 0)
        centre = win[halo:halo + ch]
        pooled = acc / (hi - lo).astype(F32) - centre
        mixed = _dot(pooled.astype(BF16), w_ref[...]) * sc_ref[...]
        o_ref[pl.ds(r0, ch), :] = mixed.astype(o_ref.dtype)
        return carry

    lax.fori_loop(0, length // ch, chunk, 0)


def _pool_mixer(u_all, w_blockdiag, scale, row0, n_seq, length):
    blk0 = row0 // length
    return pl.pallas_call(
        functools.partial(_pool_kernel, length=length),
        grid=(n_seq,),
        in_specs=[
            pl.BlockSpec((length, POOL_DIM), lambda s: (blk0 + s, 0)),
            pl.BlockSpec((POOL_DIM, POOL_DIM), lambda s: (0, 0)),
            pl.BlockSpec((1, POOL_DIM), lambda s: (0, 0)),
        ],
        out_specs=pl.BlockSpec((length, POOL_DIM), lambda s: (s, 0)),
        out_shape=jax.ShapeDtypeStruct((n_seq * length, POOL_DIM), BF16),
        scratch_shapes=[pltpu.VMEM((length + 2 * POOL_HALO, POOL_DIM), F32)],
        compiler_params=_cparams(("parallel",)),
        name="pool_mixer",
    )(u_all, w_blockdiag, scale)


def _outproj_kernel(x_ref, mod_ref, *rest, n_pool):
    if n_pool:
        m_ref, a_ref, w_ref, o_ref = rest
        y = _dot(m_ref[...], w_ref[:n_pool, :]) + _dot(a_ref[...], w_ref[n_pool:, :])
    else:
        a_ref, w_ref, o_ref = rest
        y = _dot(a_ref[...], w_ref[...])
    o_ref[...] = x_ref[...] + mod_ref[0, 2:3, :] * y


def _out_projection(x_all, mod, mixed, attn, w_bf16, seq, n_groups, n_rows):
    d = x_all.shape[1]
    tm = ROW_TILE
    n_pool = 0 if mixed is None else mixed.shape[1]
    n_att = attn.shape[1]

    def grp(i):
        return (jnp.minimum((i * tm) // seq, n_groups), 0, 0)

    in_specs = [pl.BlockSpec((tm, d), lambda i: (i, 0)), pl.BlockSpec((1, 6, d), grp)]
    args = [x_all, mod]
    if n_pool:
        in_specs.append(pl.BlockSpec((tm, n_pool), lambda i: (i, 0)))
        args.append(mixed)
    in_specs += [pl.BlockSpec((tm, n_att), lambda i: (i, 0)),
                 pl.BlockSpec((n_pool + n_att, d), lambda i: (0, 0))]
    args += [attn, w_bf16]
    return pl.pallas_call(
        functools.partial(_outproj_kernel, n_pool=n_pool),
        grid=(n_rows // tm,),
        in_specs=in_specs,
        out_specs=pl.BlockSpec((tm, d), lambda i: (i, 0)),
        out_shape=jax.ShapeDtypeStruct((n_rows, d), F32),
        compiler_params=_cparams(("parallel",)),
        name="out_projection",
    )(*args)


def _oddeven_merge(lo, hi, r):
    step = r * 2
    if step < hi - lo:
        yield from _oddeven_merge(lo, hi, step)
        yield from _oddeven_merge(lo + r, hi, step)
        yield from [(i, i + r) for i in range(lo + r, hi - r, step)]
    else:
        yield (lo, lo + r)


def _oddeven_sort(lo, hi):
    if hi - lo >= 1:
        mid = lo + (hi - lo) // 2
        yield from _oddeven_sort(lo, mid)
        yield from _oddeven_sort(mid + 1, hi)
        yield from _oddeven_merge(lo, hi, 1)


_SORT16 = tuple(_oddeven_sort(0, 15))
_SORT8 = tuple(_oddeven_sort(0, 7))
_TOP_PAIRS = tuple((r, c) for r in range(PEER_TOPK) for c in range(PEER_TOPK) if (r + 1) * (c + 1) <= PEER_TOPK)


def _cmpx(xs, i, j):
    a, b = xs[i], xs[j]
    xs[i] = jnp.maximum(a, b)
    xs[j] = jnp.minimum(a, b)


def _sort_desc(xs, net):
    xs = list(xs)
    for i, j in net:
        _cmpx(xs, i, j)
    return xs


def _bitonic_merge_desc(xs):
    xs = list(xs)
    n = len(xs)
    d = n // 2
    while d:
        for i in range(n):
            if not i & d:
                _cmpx(xs, i, i + d)
        d //= 2
    return xs


def _merge_top(a, b):
    n = len(a)
    return _bitonic_merge_desc([jnp.maximum(a[i], b[n - 1 - i]) for i in range(n)])


def _top16_sorted(rows):
    w = _sort_desc(rows, _SORT16)
    for shift in (4, 2, 1):
        w = _merge_top(w, [pltpu.roll(x, shift, 0) for x in w])
    return w


def _pair_threshold(pair):
    low = jnp.full_like(pair[(0, 0)], LOWEST)
    row0 = [pair[(0, c)] for c in range(16)]
    col0 = [pair[(r, 0)] for r in range(1, 16)] + [low]
    t01 = _merge_top(row0, col0)
    g2 = _bitonic_merge_desc([pair[(1, c)] for c in range(1, 8)] + [low] * 3
                             + [pair[(r, 1)] for r in range(7, 1, -1)])
    rest = [pair[k] for k in ((2, 2), (2, 3), (2, 4), (3, 2), (4, 2), (3, 3))]
    g3 = _sort_desc(rest + [low, low], _SORT8) + [low] * 8
    t23 = _merge_top(g2, g3)
    tau = None
    for i in range(16):
        m = jnp.maximum(t01[i], t23[15 - i])
        tau = m if tau is None else jnp.minimum(tau, m)
    return tau


def _route_tile(s1, s2):
    assert PEER_TOPK == 16 and PEER_NKEYS == 128
    n_col = s1.shape[1] // LANES
    assert 8 % n_col == 0
    span = 8 // n_col
    cols = []
    for j in range(n_col):
        c1 = s1[:, j * LANES:(j + 1) * LANES]
        c2 = s2[:, j * LANES:(j + 1) * LANES]
        rows1 = [c1[8 * g:8 * g + 8] for g in range(16)]
        rows2 = [c2[8 * g:8 * g + 8] for g in range(16)]
        cols.append((rows1, rows2, _top16_sorted(rows1), _top16_sorted(rows2)))

    sub = lax.broadcasted_iota(jnp.int32, (8, LANES), 0)

    def pack(vals):
        out = vals[-1]
        for j in range(n_col - 2, -1, -1):
            out = jnp.where(sub < (j + 1) * span, vals[j], out)
        return out

    def unpack(x, j):
        return jnp.broadcast_to(x[j * span:j * span + 1, :], (8, LANES))

    v1 = [pack([c[2][r] for c in cols]) for r in range(16)]
    v2 = [pack([c[3][r] for c in cols]) for r in range(16)]
    pair = {(r, c): v1[r] + v2[c] for (r, c) in _TOP_PAIRS}
    tau = _pair_threshold(pair)
    e1 = [jnp.exp(v - v1[0]) for v in v1]
    e2 = [jnp.exp(v - v2[0]) for v in v2]
    z = None
    for (r, c) in _TOP_PAIRS:
        term = jnp.where(pair[(r, c)] >= tau, e1[r] * e2[c], 0.0)
        z = term if z is None else z + term
    inv_z = (1.0 / math.sqrt(2.0)) / z
    n_top = jnp.zeros_like(tau)
    for c in range(16):
        n_top = jnp.where(pair[(0, c)] >= tau, c + 1.0, n_top)

    out = [[], [], [], []]
    for j, (rows1, rows2, t1, t2) in enumerate(cols):
        tau_j, inv_z_j, n_top_j = unpack(tau, j), unpack(inv_z, j), unpack(n_top, j)
        cnt, p1n, rank2, p2 = [], [], [], []
        for g in range(16):
            a = rows1[g]
            n = jnp.zeros_like(a)
            for c in range(8):
                n = jnp.where(a + t2[c] >= tau_j, c + 1.0, n)
            cnt.append(jnp.where(a >= t1[0], n_top_j, n))
            p1n.append(jnp.exp(a - t1[0]) * inv_z_j)
            b = rows2[g]
            k = jnp.zeros_like(b)
            for r in range(16):
                k = jnp.where(t2[r] > b, r + 1.0, k)
            rank2.append(k)
            p2.append(jnp.exp(b - t2[0]))
        for dst, parts in zip(out, (cnt, p1n, rank2, p2)):
            dst.append(jnp.concatenate(parts, axis=0))
    return tuple(jnp.concatenate(parts, axis=1) for parts in out)


def _peer_kernel(x_ref, mod_ref, g_ref, wq_ref, k1_ref, k2_ref, ut_ref, vt_ref, *rest, final_norm, n_blocks):
    if final_norm:
        fg_ref, o_ref, *scratch = rest
    else:
        o_ref, *scratch = rest
    hb_ref, cnt_ref, p1_ref, rank_ref, p2_ref, gw0_ref, gw1_ref, acc_ref = scratch
    s = pl.program_id(1)
    eb, t = gw0_ref.shape
    blocks = eb // PEER_NKEYS
    half = PEER_DKEY // 2

    @pl.when(s == 0)
    def _route():
        x = x_ref[...]
        h2 = _norm_mod(x, g_ref[...], mod_ref[0, 3:4, :], mod_ref[0, 4:5, :])
        hb_ref[...] = h2.astype(BF16)
        acc_ref[...] = jnp.zeros_like(acc_ref)

        def head(h, carry):
            qh = _dot(hb_ref[...], wq_ref[h])
            s1 = _dot_nt(k1_ref[...], qh[:, :half].astype(BF16))
            s2 = _dot_nt(k2_ref[...], qh[:, half:].astype(BF16))
            cnt, p1n, rank2, p2 = _route_tile(s1, s2)
            cnt_ref[h] = cnt
            p1_ref[h] = p1n
            rank_ref[h] = rank2.astype(BF16)
            p2_ref[h] = p2.astype(BF16)
            return carry

        lax.fori_loop(0, PEER_HEADS, head, 0)

    def stage(parity, build, consume):
        gw_w, gw_r = (gw0_ref, gw1_ref) if parity == 0 else (gw1_ref, gw0_ref)
        reps = PEER_NKEYS // 16
        span = 2 * PEER_NKEYS
        n_pairs = blocks // 2

        def build_pair(pair):
            y_pair = _dot(hb_ref[...], ut_ref[:, pair * span:(pair + 1) * span]).T
            for i in (2 * pair, 2 * pair + 1):
                a = s * blocks + i
                w = None
                for h in range(PEER_HEADS):
                    cnt16 = jnp.broadcast_to(cnt_ref[h, pl.ds(a, 1), :], (16, t)).astype(BF16)
                    p16 = jnp.broadcast_to(p1_ref[h, pl.ds(a, 1), :], (16, t)).astype(BF16)
                    sel = rank_ref[h] < jnp.concatenate([cnt16] * reps, axis=0)
                    term = (jnp.where(sel, p2_ref[h], jnp.zeros((), BF16))
                            * jnp.concatenate([p16] * reps, axis=0))
                    w = term if w is None else w + term
                y = y_pair[(i % 2) * PEER_NKEYS:(i % 2 + 1) * PEER_NKEYS, :]
                act = y * (1.0 + lax.erf(y))
                gw_w[i * PEER_NKEYS:(i + 1) * PEER_NKEYS, :] = act.astype(BF16) * w

        if build:
            for k in range(n_pairs):
                build_pair(k)
        if consume:
            acc_ref[...] += _dot(vt_ref[...], gw_r[...])

    pl.when(s == 0)(functools.partial(stage, 0, True, False))
    steady = (s >= 1) & (s < n_blocks)
    pl.when(steady & (s % 2 == 0))(functools.partial(stage, 0, True, True))
    pl.when(steady & (s % 2 == 1))(functools.partial(stage, 1, True, True))
    pl.when(s == n_blocks)(functools.partial(stage, n_blocks % 2, False, True))

    @pl.when(s == n_blocks)
    def _finish():
        y = x_ref[...] + mod_ref[0, 5:6, :] * acc_ref[...].T
        if final_norm:
            y = y * lax.rsqrt(jnp.mean(y * y, axis=-1, keepdims=True) + EPS) * fg_ref[...]
        o_ref[...] = y


def _peer(x_all, mod, gain, wq_heads, keys1, keys2, ut_bf16, vt_bf16, seq, n_groups, n_rows, final_gain):
    d = x_all.shape[1]
    t = PEER_TOKENS
    eb = PEER_EXPERT_BLOCK
    n_blocks = ut_bf16.shape[1] // eb

    def grp(i, e):
        return (jnp.minimum((i * t) // seq, n_groups), 0, 0)

    in_specs = [
        pl.BlockSpec((t, d), lambda i, e: (i, 0)),
        pl.BlockSpec((1, 6, d), grp),
        pl.BlockSpec((1, d), lambda i, e: (0, 0)),
        pl.BlockSpec(wq_heads.shape, lambda i, e: (0, 0, 0)),
        pl.BlockSpec(keys1.shape, lambda i, e: (0, 0)),
        pl.BlockSpec(keys2.shape, lambda i, e: (0, 0)),
        pl.BlockSpec((d, eb), lambda i, e: (0, jnp.minimum(e, n_blocks - 1))),
        pl.BlockSpec((d, eb), lambda i, e: (0, jnp.maximum(e - 1, 0))),
    ]
    args = [x_all, mod, gain, wq_heads, keys1, keys2, ut_bf16, vt_bf16]
    if final_gain is not None:
        in_specs.append(pl.BlockSpec((1, d), lambda i, e: (0, 0)))
        args.append(final_gain)
    per_head = (PEER_HEADS, PEER_NKEYS, t)
    return pl.pallas_call(
        functools.partial(_peer_kernel, final_norm=final_gain is not None, n_blocks=n_blocks),
        grid=(n_rows // t, n_blocks + 1),
        in_specs=in_specs,
        out_specs=pl.BlockSpec((t, d), lambda i, e: (i, 0)),
        out_shape=jax.ShapeDtypeStruct((n_rows, d), F32),
        scratch_shapes=[
            pltpu.VMEM((t, d), BF16),
            pltpu.VMEM(per_head, F32),
            pltpu.VMEM(per_head, F32),
            pltpu.VMEM(per_head, BF16),
            pltpu.VMEM(per_head, BF16),
            pltpu.VMEM((eb, t), BF16),
            pltpu.VMEM((eb, t), BF16),
            pltpu.VMEM((d, t), F32),
        ],
        compiler_params=_cparams(("parallel", "arbitrary")),
        name="peer",
    )(*args)


def _rope_tables(seq, pad_rows):
    half = HEAD_DIM // 2
    t = jnp.arange(seq, dtype=jnp.int32)
    row = (t // GRID_W).astype(F32)
    col = (t % GRID_W).astype(F32)
    inv = 1.0 / (ROPE_THETA ** (jnp.arange(0, half, 2, dtype=F32) / half))
    ang_r = row[:, None] * inv[None, :]
    ang_c = col[:, None] * inv[None, :]
    cos_h = jnp.concatenate([jnp.cos(ang_r)] * 2 + [jnp.cos(ang_c)] * 2, axis=1)
    sin_h = jnp.concatenate([-jnp.sin(ang_r), jnp.sin(ang_r), -jnp.sin(ang_c), jnp.sin(ang_c)], axis=1)
    cos_t = jnp.concatenate([jnp.tile(cos_h, (1, LANES // HEAD_DIM)), jnp.ones((pad_rows, LANES), F32)], axis=0)
    sin_t = jnp.concatenate([jnp.tile(sin_h, (1, LANES // HEAD_DIM)), jnp.zeros((pad_rows, LANES), F32)], axis=0)
    return cos_t, sin_t


def _head_norm_consts(q_gain, k_gain, n_q_heads, n_kv_heads):
    n_heads = n_q_heads + n_kv_heads
    head_of_lane = np.arange(n_heads * HEAD_DIM) // HEAD_DIM
    onehot = (head_of_lane[:, None] == np.arange(LANES)[None, :]).astype(np.float32)
    gsum = jnp.asarray(onehot / HEAD_DIM, BF16)
    gexp = jnp.asarray(onehot.T, BF16)
    hg = jnp.concatenate([jnp.tile(q_gain, n_q_heads), jnp.tile(k_gain, n_kv_heads)])[None, :]
    return gsum, gexp, hg


def kernel(x, c, ctx, c_ctx, ada_w, ada_b, norm1_g, norm2_g, final_g, e_w_in, e_q_norm_g, e_k_norm_g,
           e_pool_w, e_pool_scale, e_w_out, o_w_in, o_sink, o_w_out, p_w_q, p_keys1, p_keys2, p_u, p_v):
    batch, seq, d = x.shape
    ctx_len = ctx.shape[1]
    depth = ada_w.shape[0]
    n_lat = batch * seq
    n_all = n_lat + batch * ctx_len

    x_all = jnp.concatenate([x.reshape(n_lat, d), ctx.reshape(batch * ctx_len, d)], axis=0)
    cond = jnp.concatenate([c, c_ctx[None, :], jnp.zeros((8 - (batch + 1) % 8, d), F32)], axis=0)
    mods = _mod_vectors(cond, ada_w, ada_b)
    mods = mods.reshape(depth, cond.shape[0], 6, d)
    cos_t, sin_t = _rope_tables(seq, ROW_TILE)

    for i in range(depth):
        last = i == depth - 1
        mod = mods[i]
        j = i // 2
        g1 = norm1_g[i][None, :]
        g2 = norm2_g[i][None, :]
        if i % 2 == 0:
            consts = _head_norm_consts(e_q_norm_g[j], e_k_norm_g[j], B_Q_HEADS, B_KV_HEADS)
            u, q, k, v = _in_projection(
                x_all, mod, g1, e_w_in[j].astype(BF16), cos_t, sin_t, seq, batch,
                n_pre=POOL_DIM, n_q=B_Q_HEADS * HEAD_DIM, n_kv=B_KV_HEADS * HEAD_DIM, norm_consts=consts)
            w_pool = jax.scipy.linalg.block_diag(*[e_pool_w[j, g] for g in range(POOL_GROUPS)]).astype(BF16)
            p_scale = e_pool_scale[j][None, :]
            logit_bound = (NORM_BOUND_MARGIN * math.sqrt(HEAD_DIM) * jnp.max(jnp.abs(e_q_norm_g[j]))
                           * jnp.max(jnp.abs(e_k_norm_g[j]))).reshape(1)
            attn = _global_attention(logit_bound, q, k, v, batch, seq, ctx_len, latent=True)
            mixed = _pool_mixer(u, w_pool, p_scale, 0, batch, seq)
            if not last:
                attn = jnp.concatenate(
                    [attn, _global_attention(logit_bound, q, k, v, batch, seq, ctx_len, latent=False)], axis=0)
                mixed = jnp.concatenate([mixed, _pool_mixer(u, w_pool, p_scale, n_lat, batch, ctx_len)], axis=0)
            w_out = e_w_out[j].astype(BF16)
        else:
            _q = C_Q_HEADS * HEAD_DIM
            q, k, v = _in_projection(
                x_all, mod, g1, o_w_in[j].astype(BF16), cos_t, sin_t, seq, batch,
                n_pre=0, n_q=_q, n_kv=C_KV_HEADS * HEAD_DIM, norm_consts=None)
            attn = _window_attention(q, k, v, o_sink[j], batch, seq, ctx_len)
            if not last:
                raise NotImplementedError("context stream after an odd layer")
            mixed = None
            w_out = o_w_out[j].astype(BF16)
        n_rows = n_lat if last else n_all
        x_mid = _out_projection(x_all, mod, mixed, attn, w_out, seq, batch, n_rows)
        wq_heads = p_w_q[i].astype(BF16).reshape(d, PEER_HEADS, PEER_DKEY).transpose(1, 0, 2)
        x_all = _peer(x_mid, mod, g2, wq_heads, p_keys1[i].astype(BF16), p_keys2[i].astype(BF16),
                      (p_u[i] * (1.0 / math.sqrt(2.0))).astype(BF16).T, p_v[i].astype(BF16).T, seq, batch, n_rows,
                      final_g[None, :] if last else None)
    return x_all[:n_lat].reshape(batch, seq, d)
```

```python
import functools
import math

import jax
import jax.numpy as jnp
import numpy as np
from jax import lax
from jax.experimental import pallas as pl
from jax.experimental.pallas import tpu as pltpu

F32 = jnp.float32
BF16 = jnp.bfloat16

GRID_W = 64
HEAD_DIM = 64
ROPE_THETA = 10000.0
EPS = 1e-6
MASK_VALUE = -1e30
POOL_GROUPS = 4
POOL_GROUP_DIM = 64
POOL_DIM = POOL_GROUPS * POOL_GROUP_DIM
POOL_WINDOWS = (2, 4, 8, 16)
POOL_HALO = 8
B_Q_HEADS = 12
B_KV_HEADS = 4
C_Q_HEADS = 16
C_KV_HEADS = 4
WINDOW = 128
PEER_HEADS = 8
PEER_NKEYS = 128
PEER_DKEY = 256
PEER_TOPK = 16
LOWEST = -3.0e38
SAFE_LOGIT_BOUND = 40.0
NORM_BOUND_MARGIN = 1.05

LANES = 128
VMEM_LIMIT_BYTES = 56 * 1024 * 1024
ROW_TILE = 512
ATTN_Q_TILE = 512
ATTN_KV_TILE = 512
WIN_Q_TILE = 256
WIN_BAND = WIN_Q_TILE + 2 * WINDOW
PEER_TOKENS = 512
PEER_EXPERT_BLOCK = 1024
POOL_CHUNK = 256


def _cparams(semantics):
    return pltpu.CompilerParams(dimension_semantics=semantics, vmem_limit_bytes=VMEM_LIMIT_BYTES)


def _dot(a, b):
    return jnp.dot(a, b, preferred_element_type=F32)


def _dot_nt(a, b):
    return lax.dot_general(a, b, (((1,), (1,)), ((), ())), preferred_element_type=F32)


def _split3(a):
    hi = a.astype(BF16)
    lo = (a - hi.astype(F32)).astype(BF16)
    return hi, lo


def _dot_f32ish(a, b):
    ah, al = _split3(a)
    bh, bl = _split3(b)
    return _dot(ah, bh) + (_dot(ah, bl) + _dot(al, bh))


def _norm_mod(x, gain, shift, scale):
    y = x * lax.rsqrt(jnp.mean(x * x, axis=-1, keepdims=True) + EPS)
    return (y * gain) * (1.0 + scale) + shift


def _mod_kernel(c_ref, w_ref, b_ref, o_ref):
    c = c_ref[...]
    act = c * (1.0 / (1.0 + jnp.exp(-c)))
    o_ref[0] = _dot_f32ish(act, w_ref[0]) + b_ref[0]


def _mod_vectors(cond, ada_w, ada_b):
    depth, d, d6 = ada_w.shape
    g8 = cond.shape[0]
    nblk = d6 // d
    return pl.pallas_call(
        _mod_kernel,
        grid=(depth, nblk),
        in_specs=[
            pl.BlockSpec((g8, d), lambda i, j: (0, 0)),
            pl.BlockSpec((1, d, d), lambda i, j: (i, 0, j)),
            pl.BlockSpec((1, 1, d), lambda i, j: (i, 0, j)),
        ],
        out_specs=pl.BlockSpec((1, g8, d), lambda i, j: (i, 0, j)),
        out_shape=jax.ShapeDtypeStruct((depth, g8, d6), F32),
        compiler_params=_cparams(("arbitrary", "arbitrary")),
        name="mod_vectors",
    )(cond, ada_w, ada_b.reshape(depth, 1, d6))


def _swap16(x):
    lane = lax.broadcasted_iota(jnp.int32, x.shape, 1)
    return jnp.where((lane & 16) == 0, pltpu.roll(x, LANES - 16, 1), pltpu.roll(x, 16, 1))


def _inproj_kernel(x_ref, mod_ref, g_ref, w_ref, cos_ref, sin_ref, *rest, n_pre, n_qk, qk_norm):
    if qk_norm:
        gsum_ref, gexp_ref, hg_ref, *outs = rest
    else:
        outs = rest
    x = x_ref[...]
    h = _norm_mod(x, g_ref[...], mod_ref[0, 0:1, :], mod_ref[0, 1:2, :])
    proj = _dot(h.astype(BF16), w_ref[...])
    qk = proj[:, n_pre:n_pre + n_qk]
    if qk_norm:
        ms = _dot((qk * qk).astype(BF16), gsum_ref[...])
        rinv = lax.rsqrt(ms + EPS)
        r_hi, r_lo = _split3(rinv)
        qk = qk * (_dot(r_hi, gexp_ref[...]) + _dot(r_lo, gexp_ref[...])) * hg_ref[...]
    cos = cos_ref[...]
    sin = sin_ref[...]
    blocks = []
    for j in range(n_qk // LANES):
        blk = qk[:, j * LANES:(j + 1) * LANES]
        blocks.append(blk * cos + _swap16(blk) * sin)
    if n_pre:
        u_ref, q_ref, k_ref, v_ref = outs
        u_ref[...] = proj[:, :n_pre]
    else:
        q_ref, k_ref, v_ref = outs
    n_q = q_ref.shape[1]
    scale = HEAD_DIM ** -0.5
    for j, blk in enumerate(blocks):
        lo = j * LANES
        if lo < n_q:
            q_ref[:, lo:lo + LANES] = (blk * scale).astype(BF16)
        else:
            k_ref[:, lo - n_q:lo - n_q + LANES] = blk.astype(BF16)
    v_ref[...] = proj[:, n_pre + n_qk:].astype(BF16)


def _in_projection(x_all, mod, gain, w_bf16, cos_t, sin_t, seq, n_groups, *, n_pre, n_q, n_kv, norm_consts):
    n, d = x_all.shape
    tm = ROW_TILE
    n_qk = n_q + n_kv
    d_in = w_bf16.shape[1]
    lat_tiles = (n_groups * seq) // tm
    seq_tiles = seq // tm

    def grp(i):
        return (jnp.minimum((i * tm) // seq, n_groups), 0, 0)

    def rope_idx(i):
        return (jnp.where(i < lat_tiles, i % seq_tiles, seq_tiles), 0)

    in_specs = [
        pl.BlockSpec((tm, d), lambda i: (i, 0)),
        pl.BlockSpec((1, 6, d), grp),
        pl.BlockSpec((1, d), lambda i: (0, 0)),
        pl.BlockSpec((d, d_in), lambda i: (0, 0)),
        pl.BlockSpec((tm, LANES), rope_idx),
        pl.BlockSpec((tm, LANES), rope_idx),
    ]
    args = [x_all, mod, gain, w_bf16, cos_t, sin_t]
    if norm_consts is not None:
        gsum, gexp, hg = norm_consts
        in_specs += [
            pl.BlockSpec(gsum.shape, lambda i: (0, 0)),
            pl.BlockSpec(gexp.shape, lambda i: (0, 0)),
            pl.BlockSpec(hg.shape, lambda i: (0, 0)),
        ]
        args += [gsum, gexp, hg]
    out_specs, out_shape = [], []
    if n_pre:
        out_specs.append(pl.BlockSpec((tm, n_pre), lambda i: (i, 0)))
        out_shape.append(jax.ShapeDtypeStruct((n, n_pre), F32))
    for width in (n_q, n_kv, n_kv):
        out_specs.append(pl.BlockSpec((tm, width), lambda i: (i, 0)))
        out_shape.append(jax.ShapeDtypeStruct((n, width), BF16))
    return pl.pallas_call(
        functools.partial(_inproj_kernel, n_pre=n_pre, n_qk=n_qk, qk_norm=norm_consts is not None),
        grid=(n // tm,),
        in_specs=in_specs,
        out_specs=out_specs,
        out_shape=out_shape,
        compiler_params=_cparams(("parallel",)),
        name="in_projection",
    )(*args)


def _head_query(q_ref, head, kv_head):
    blk = q_ref[:, (head // 2) * LANES:(head // 2 + 1) * LANES]
    if head % 2 != kv_head % 2:
        blk = jnp.concatenate([blk[:, HEAD_DIM:], blk[:, :HEAD_DIM]], axis=1)
    lane = lax.broadcasted_iota(jnp.int32, blk.shape, 1)
    keep = (lane >= HEAD_DIM) if kv_head % 2 else (lane < HEAD_DIM)
    return jnp.where(keep, blk, jnp.zeros_like(blk))


def _store_heads(o_ref, outs, kv_of_head):
    for p in range(len(outs) // 2):
        halves = []
        for h in (2 * p, 2 * p + 1):
            half = kv_of_head(h) % 2
            halves.append(outs[h][:, half * HEAD_DIM:(half + 1) * HEAD_DIM])
        o_ref[:, p * LANES:(p + 1) * LANES] = jnp.concatenate(halves, axis=1).astype(o_ref.dtype)


def _global_attn_kernel(bound_ref, q_ref, kc_ref, vc_ref, *rest, n_heads, group, n_lat_chunks):
    if n_lat_chunks:
        kl_ref, vl_ref, o_ref = rest
    else:
        (o_ref,) = rest
    tk = ATTN_KV_TILE

    def attend(bounded):
        outs = []
        for j in range(n_heads // group):
            cb = slice((j // 2) * LANES, (j // 2 + 1) * LANES)
            qzs = [_head_query(q_ref, h, j) for h in range(j * group, (j + 1) * group)]
            kc = kc_ref[:, cb]
            vc = vc_ref[:, cb]
            state = []
            for qz in qzs:
                s = _dot_nt(qz, kc)
                m = jnp.zeros((s.shape[0], 1), F32) if bounded else jnp.max(s, axis=-1, keepdims=True)
                p = jnp.exp(s) if bounded else jnp.exp(s - m)
                state.append((m, jnp.sum(p, axis=-1, keepdims=True), _dot(p.astype(BF16), vc)))

            if n_lat_chunks:
                def step(c, carry, qzs=qzs, cb=cb):
                    rows = pl.ds(pl.multiple_of(c * tk, tk), tk)
                    k = kl_ref[rows, cb]
                    v = vl_ref[rows, cb]
                    new = []
                    for qz, (m, l, o) in zip(qzs, carry):
                        s = _dot_nt(qz, k)
                        if bounded:
                            p = jnp.exp(s)
                            l = l + jnp.sum(p, axis=-1, keepdims=True)
                            o = o + _dot(p.astype(BF16), v)
                        else:
                            m_new = jnp.maximum(m, jnp.max(s, axis=-1, keepdims=True))
                            alpha = jnp.exp(m - m_new)
                            p = jnp.exp(s - m_new)
                            l = alpha * l + jnp.sum(p, axis=-1, keepdims=True)
                            o = alpha * o + _dot(p.astype(BF16), v)
                            m = m_new
                        new.append((m, l, o))
                    return tuple(new)

                state = lax.fori_loop(0, n_lat_chunks, step, tuple(state))
            outs.extend(o / l for (_, l, o) in state)
        _store_heads(o_ref, outs, lambda h: h // group)

    small = bound_ref[0] <= SAFE_LOGIT_BOUND
    pl.when(small)(functools.partial(attend, True))
    pl.when(jnp.logical_not(small))(functools.partial(attend, False))


def _global_attention(logit_bound, q_all, k_all, v_all, batch, seq, ctx_len, *, latent):
    n_q = q_all.shape[1]
    n_kv = k_all.shape[1]
    n_heads = n_q // HEAD_DIM
    group = n_heads // (n_kv // HEAD_DIM)
    ctx_blk0 = (batch * seq) // ctx_len
    if latent:
        tq = ATTN_Q_TILE
        q_tiles = seq // tq
        q_idx = lambda b, i: (b * q_tiles + i, 0)
        out_rows = batch * seq
    else:
        tq = ctx_len
        q_tiles = 1
        q_idx = lambda b, i: (ctx_blk0 + b, 0)
        out_rows = batch * ctx_len
    in_specs = [
        pl.BlockSpec(memory_space=pltpu.SMEM),
        pl.BlockSpec((tq, n_q), q_idx),
        pl.BlockSpec((ctx_len, n_kv), lambda b, i: (ctx_blk0 + b, 0)),
        pl.BlockSpec((ctx_len, n_kv), lambda b, i: (ctx_blk0 + b, 0)),
    ]
    args = [logit_bound, q_all, k_all, v_all]
    if latent:
        in_specs += [pl.BlockSpec((seq, n_kv), lambda b, i: (b, 0)),
                     pl.BlockSpec((seq, n_kv), lambda b, i: (b, 0))]
        args += [k_all, v_all]
    return pl.pallas_call(
        functools.partial(_global_attn_kernel, n_heads=n_heads, group=group,
                          n_lat_chunks=(seq // ATTN_KV_TILE) if latent else 0),
        grid=(batch, q_tiles),
        in_specs=in_specs,
        out_specs=pl.BlockSpec((tq, n_q), lambda b, i: (b * q_tiles + i, 0)),
        out_shape=jax.ShapeDtypeStruct((out_rows, n_q), BF16),
        compiler_params=_cparams(("parallel", "arbitrary")),
        name="global_attention" if latent else "context_attention",
    )(*args)


def _window_attn_kernel(sink_ref, q_ref, kc_ref, vc_ref, kl_ref, vl_ref, o_ref, *, n_heads, group, seq):
    tq = WIN_Q_TILE
    band = WIN_BAND
    start = pl.program_id(1) * tq
    k0 = jnp.clip(start - WINDOW, 0, seq - band)
    k0 = pl.multiple_of(k0, WINDOW)
    rows = pl.ds(k0, band)
    qpos = start + lax.broadcasted_iota(jnp.int32, (tq, band), 0)
    kpos = k0 + lax.broadcasted_iota(jnp.int32, (tq, band), 1)
    valid = jnp.abs(qpos - kpos) <= WINDOW
    outs = []
    for h in range(n_heads):
        j = h // group
        cb = slice((j // 2) * LANES, (j // 2 + 1) * LANES)
        qz = _head_query(q_ref, h, j)
        s_c = _dot_nt(qz, kc_ref[:, cb])
        s_b = jnp.where(valid, _dot_nt(qz, kl_ref[rows, cb]), MASK_VALUE)
        sink = sink_ref[h]
        m = jnp.maximum(jnp.maximum(jnp.max(s_c, axis=-1, keepdims=True),
                                    jnp.max(s_b, axis=-1, keepdims=True)), sink)
        e_c = jnp.exp(s_c - m)
        e_b = jnp.exp(s_b - m)
        denom = (jnp.sum(e_c, axis=-1, keepdims=True) + jnp.sum(e_b, axis=-1, keepdims=True)
                 + jnp.exp(sink - m))
        o = _dot(e_c.astype(BF16), vc_ref[:, cb]) + _dot(e_b.astype(BF16), vl_ref[rows, cb])
        outs.append(o / denom)
    _store_heads(o_ref, outs, lambda h: h // group)


def _window_attention(q_all, k_all, v_all, sink, batch, seq, ctx_len):
    n_q = q_all.shape[1]
    n_kv = k_all.shape[1]
    n_heads = n_q // HEAD_DIM
    group = n_heads // (n_kv // HEAD_DIM)
    tq = WIN_Q_TILE
    q_tiles = seq // tq
    ctx_blk0 = (batch * seq) // ctx_len
    return pl.pallas_call(
        functools.partial(_window_attn_kernel, n_heads=n_heads, group=group, seq=seq),
        grid=(batch, q_tiles),
        in_specs=[
            pl.BlockSpec(memory_space=pltpu.SMEM),
            pl.BlockSpec((tq, n_q), lambda b, i: (b * q_tiles + i, 0)),
            pl.BlockSpec((ctx_len, n_kv), lambda b, i: (ctx_blk0 + b, 0)),
            pl.BlockSpec((ctx_len, n_kv), lambda b, i: (ctx_blk0 + b, 0)),
            pl.BlockSpec((seq, n_kv), lambda b, i: (b, 0)),
            pl.BlockSpec((seq, n_kv), lambda b, i: (b, 0)),
        ],
        out_specs=pl.BlockSpec((tq, n_q), lambda b, i: (b * q_tiles + i, 0)),
        out_shape=jax.ShapeDtypeStruct((batch * seq, n_q), BF16),
        compiler_params=_cparams(("parallel", "arbitrary")),
        name="window_attention",
    )(sink, q_all, k_all, v_all, k_all, v_all)


def _pool_kernel(u_ref, w_ref, sc_ref, o_ref, pad_ref, *, length):
    halo = POOL_HALO
    ch = POOL_CHUNK
    zeros = jnp.zeros((halo, POOL_DIM), F32)
    pad_ref[0:halo, :] = zeros
    pad_ref[halo + length:halo + length + halo, :] = zeros
    pad_ref[halo:halo + length, :] = u_ref[...]
    lane = lax.broadcasted_iota(jnp.int32, (ch, POOL_DIM), 1)
    grp = lane // POOL_GROUP_DIM
    half_w = jnp.left_shift(1, grp)
    row0 = lax.broadcasted_iota(jnp.int32, (ch, POOL_DIM), 0)
    win_rows = ch + 2 * halo

    def chunk(c, carry):
        r0 = pl.multiple_of(c * ch, ch)
        win = pad_ref[pl.ds(r0, win_rows), :]
        acc = jnp.zeros((ch, POOL_DIM), F32)
        for dlt in range(-halo, halo):
            shifted = pltpu.roll(win, win_rows - (halo + dlt), 0)[:ch] if halo + dlt else win[:ch]
            inside = (half_w >= -dlt) if dlt < 0 else (half_w > dlt)
            acc = acc + jnp.where(inside, shifted, 0.0)
        t = row0 + r0
        hi = jnp.minimum(t + half_w, length)
        lo = jnp.maximum(t - half_w, 0)
        centre = win[halo:halo + ch]
        pooled = acc / (hi - lo).astype(F32) - centre
        mixed = _dot(pooled.astype(BF16), w_ref[...]) * sc_ref[...]
        o_ref[pl.ds(r0, ch), :] = mixed.astype(o_ref.dtype)
        return carry

    lax.fori_loop(0, length // ch, chunk, 0)


def _pool_mixer(u_all, w_blockdiag, scale, row0, n_seq, length):
    blk0 = row0 // length
    return pl.pallas_call(
        functools.partial(_pool_kernel, length=length),
        grid=(n_seq,),
        in_specs=[
            pl.BlockSpec((length, POOL_DIM), lambda s: (blk0 + s, 0)),
            pl.BlockSpec((POOL_DIM, POOL_DIM), lambda s: (0, 0)),
            pl.BlockSpec((1, POOL_DIM), lambda s: (0, 0)),
        ],
        out_specs=pl.BlockSpec((length, POOL_DIM), lambda s: (s, 0)),
        out_shape=jax.ShapeDtypeStruct((n_seq * length, POOL_DIM), BF16),
        scratch_shapes=[pltpu.VMEM((length + 2 * POOL_HALO, POOL_DIM), F32)],
        compiler_params=_cparams(("parallel",)),
        name="pool_mixer",
    )(u_all, w_blockdiag, scale)


def _outproj_kernel(x_ref, mod_ref, *rest, n_pool):
    if n_pool:
        m_ref, a_ref, w_ref, o_ref = rest
        y = _dot(m_ref[...], w_ref[:n_pool, :]) + _dot(a_ref[...], w_ref[n_pool:, :])
    else:
        a_ref, w_ref, o_ref = rest
        y = _dot(a_ref[...], w_ref[...])
    o_ref[...] = x_ref[...] + mod_ref[0, 2:3, :] * y


def _out_projection(x_all, mod, mixed, attn, w_bf16, seq, n_groups, n_rows):
    d = x_all.shape[1]
    tm = ROW_TILE
    n_pool = 0 if mixed is None else mixed.shape[1]
    n_att = attn.shape[1]

    def grp(i):
        return (jnp.minimum((i * tm) // seq, n_groups), 0, 0)

    in_specs = [pl.BlockSpec((tm, d), lambda i: (i, 0)), pl.BlockSpec((1, 6, d), grp)]
    args = [x_all, mod]
    if n_pool:
        in_specs.append(pl.BlockSpec((tm, n_pool), lambda i: (i, 0)))
        args.append(mixed)
    in_specs += [pl.BlockSpec((tm, n_att), lambda i: (i, 0)),
                 pl.BlockSpec((n_pool + n_att, d), lambda i: (0, 0))]
    args += [attn, w_bf16]
    return pl.pallas_call(
        functools.partial(_outproj_kernel, n_pool=n_pool),
        grid=(n_rows // tm,),
        in_specs=in_specs,
        out_specs=pl.BlockSpec((tm, d), lambda i: (i, 0)),
        out_shape=jax.ShapeDtypeStruct((n_rows, d), F32),
        compiler_params=_cparams(("parallel",)),
        name="out_projection",
    )(*args)


def _oddeven_merge(lo, hi, r):
    step = r * 2
    if step < hi - lo:
        yield from _oddeven_merge(lo, hi, step)
        yield from _oddeven_merge(lo + r, hi, step)
        yield from [(i, i + r) for i in range(lo + r, hi - r, step)]
    else:
        yield (lo, lo + r)


def _oddeven_sort(lo, hi):
    if hi - lo >= 1:
        mid = lo + (hi - lo) // 2
        yield from _oddeven_sort(lo, mid)
        yield from _oddeven_sort(mid + 1, hi)
        yield from _oddeven_merge(lo, hi, 1)


_SORT16 = tuple(_oddeven_sort(0, 15))
_SORT8 = tuple(_oddeven_sort(0, 7))
_TOP_PAIRS = tuple((r, c) for r in range(PEER_TOPK) for c in range(PEER_TOPK) if (r + 1) * (c + 1) <= PEER_TOPK)


def _cmpx(xs, i, j):
    a, b = xs[i], xs[j]
    xs[i] = jnp.maximum(a, b)
    xs[j] = jnp.minimum(a, b)


def _sort_desc(xs, net):
    xs = list(xs)
    for i, j in net:
        _cmpx(xs, i, j)
    return xs


def _bitonic_merge_desc(xs):
    xs = list(xs)
    n = len(xs)
    d = n // 2
    while d:
        for i in range(n):
            if not i & d:
                _cmpx(xs, i, i + d)
        d //= 2
    return xs


def _merge_top(a, b):
    n = len(a)
    return _bitonic_merge_desc([jnp.maximum(a[i], b[n - 1 - i]) for i in range(n)])


def _top16_sorted(rows):
    w = _sort_desc(rows, _SORT16)
    for shift in (4, 2, 1):
        w = _merge_top(w, [pltpu.roll(x, shift, 0) for x in w])
    return w


def _pair_threshold(pair):
    low = jnp.full_like(pair[(0, 0)], LOWEST)
    row0 = [pair[(0, c)] for c in range(16)]
    col0 = [pair[(r, 0)] for r in range(1, 16)] + [low]
    t01 = _merge_top(row0, col0)
    g2 = _bitonic_merge_desc([pair[(1, c)] for c in range(1, 8)] + [low] * 3
                             + [pair[(r, 1)] for r in range(7, 1, -1)])
    rest = [pair[k] for k in ((2, 2), (2, 3), (2, 4), (3, 2), (4, 2), (3, 3))]
    g3 = _sort_desc(rest + [low, low], _SORT8) + [low] * 8
    t23 = _merge_top(g2, g3)
    tau = None
    for i in range(16):
        m = jnp.maximum(t01[i], t23[15 - i])
        tau = m if tau is None else jnp.minimum(tau, m)
    return tau


def _route_tile(s1, s2):
    assert PEER_TOPK == 16 and PEER_NKEYS == 128
    n_col = s1.shape[1] // LANES
    assert 8 % n_col == 0
    span = 8 // n_col
    cols = []
    for j in range(n_col):
        c1 = s1[:, j * LANES:(j + 1) * LANES]
        c2 = s2[:, j * LANES:(j + 1) * LANES]
        rows1 = [c1[8 * g:8 * g + 8] for g in range(16)]
        rows2 = [c2[8 * g:8 * g + 8] for g in range(16)]
        cols.append((rows1, rows2, _top16_sorted(rows1), _top16_sorted(rows2)))

    sub = lax.broadcasted_iota(jnp.int32, (8, LANES), 0)

    def pack(vals):
        out = vals[-1]
        for j in range(n_col - 2, -1, -1):
            out = jnp.where(sub < (j + 1) * span, vals[j], out)
        return out

    def unpack(x, j):
        return jnp.broadcast_to(x[j * span:j * span + 1, :], (8, LANES))

    v1 = [pack([c[2][r] for c in cols]) for r in range(16)]
    v2 = [pack([c[3][r] for c in cols]) for r in range(16)]
    pair = {(r, c): v1[r] + v2[c] for (r, c) in _TOP_PAIRS}
    tau = _pair_threshold(pair)
    e1 = [jnp.exp(v - v1[0]) for v in v1]
    e2 = [jnp.exp(v - v2[0]) for v in v2]
    z = None
    for (r, c) in _TOP_PAIRS:
        term = jnp.where(pair[(r, c)] >= tau, e1[r] * e2[c], 0.0)
        z = term if z is None else z + term
    inv_z = (1.0 / math.sqrt(2.0)) / z
    n_top = jnp.zeros_like(tau)
    for c in range(16):
        n_top = jnp.where(pair[(0, c)] >= tau, c + 1.0, n_top)

    out = [[], [], [], []]
    for j, (rows1, rows2, t1, t2) in enumerate(cols):
        tau_j, inv_z_j, n_top_j = unpack(tau, j), unpack(inv_z, j), unpack(n_top, j)
        cnt, p1n, rank2, p2 = [], [], [], []
        for g in range(16):
            a = rows1[g]
            n = jnp.zeros_like(a)
            for c in range(8):
                n = jnp.where(a + t2[c] >= tau_j, c + 1.0, n)
            cnt.append(jnp.where(a >= t1[0], n_top_j, n))
            p1n.append(jnp.exp(a - t1[0]) * inv_z_j)
            b = rows2[g]
            k = jnp.zeros_like(b)
            for r in range(16):
                k = jnp.where(t2[r] > b, r + 1.0, k)
            rank2.append(k)
            p2.append(jnp.exp(b - t2[0]))
        for dst, parts in zip(out, (cnt, p1n, rank2, p2)):
            dst.append(jnp.concatenate(parts, axis=0))
    return tuple(jnp.concatenate(parts, axis=1) for parts in out)


def _peer_kernel(x_ref, mod_ref, g_ref, wq_ref, k1_ref, k2_ref, ut_ref, vt_ref, *rest, final_norm, n_blocks):
    if final_norm:
        fg_ref, o_ref, *scratch = rest
    else:
        o_ref, *scratch = rest
    hb_ref, cnt_ref, p1_ref, rank_ref, p2_ref, gw0_ref, gw1_ref, acc_ref = scratch
    s = pl.program_id(1)
    eb, t = gw0_ref.shape
    blocks = eb // PEER_NKEYS
    half = PEER_DKEY // 2

    @pl.when(s == 0)
    def _route():
        x = x_ref[...]
        h2 = _norm_mod(x, g_ref[...], mod_ref[0, 3:4, :], mod_ref[0, 4:5, :])
        hb_ref[...] = h2.astype(BF16)
        acc_ref[...] = jnp.zeros_like(acc_ref)

        def head(h, carry):
            qh = _dot(hb_ref[...], wq_ref[h])
            s1 = _dot_nt(k1_ref[...], qh[:, :half].astype(BF16))
            s2 = _dot_nt(k2_ref[...], qh[:, half:].astype(BF16))
            cnt, p1n, rank2, p2 = _route_tile(s1, s2)
            cnt_ref[h] = cnt
            p1_ref[h] = p1n
            rank_ref[h] = rank2.astype(BF16)
            p2_ref[h] = p2.astype(BF16)
            return carry

        lax.fori_loop(0, PEER_HEADS, head, 0)

    def stage(parity, build, consume):
        gw_w, gw_r = (gw0_ref, gw1_ref) if parity == 0 else (gw1_ref, gw0_ref)
        reps = PEER_NKEYS // 16
        span = 2 * PEER_NKEYS
        n_pairs = blocks // 2

        def build_pair(pair):
            y_pair = _dot(hb_ref[...], ut_ref[:, pair * span:(pair + 1) * span]).T
            for i in (2 * pair, 2 * pair + 1):
                a = s * blocks + i
                w = None
                for h in range(PEER_HEADS):
                    cnt16 = jnp.broadcast_to(cnt_ref[h, pl.ds(a, 1), :], (16, t)).astype(BF16)
                    p16 = jnp.broadcast_to(p1_ref[h, pl.ds(a, 1), :], (16, t)).astype(BF16)
                    sel = rank_ref[h] < jnp.concatenate([cnt16] * reps, axis=0)
                    term = (jnp.where(sel, p2_ref[h], jnp.zeros((), BF16))
                            * jnp.concatenate([p16] * reps, axis=0))
                    w = term if w is None else w + term
                y = y_pair[(i % 2) * PEER_NKEYS:(i % 2 + 1) * PEER_NKEYS, :]
                act = y * (1.0 + lax.erf(y))
                gw_w[i * PEER_NKEYS:(i + 1) * PEER_NKEYS, :] = act.astype(BF16) * w

        if build:
            for k in range(n_pairs):
                build_pair(k)
        if consume:
            acc_ref[...] += _dot(vt_ref[...], gw_r[...])

    pl.when(s == 0)(functools.partial(stage, 0, True, False))
    steady = (s >= 1) & (s < n_blocks)
    pl.when(steady & (s % 2 == 0))(functools.partial(stage, 0, True, True))
    pl.when(steady & (s % 2 == 1))(functools.partial(stage, 1, True, True))
    pl.when(s == n_blocks)(functools.partial(stage, n_blocks % 2, False, True))

    @pl.when(s == n_blocks)
    def _finish():
        y = x_ref[...] + mod_ref[0, 5:6, :] * acc_ref[...].T
        if final_norm:
            y = y * lax.rsqrt(jnp.mean(y * y, axis=-1, keepdims=True) + EPS) * fg_ref[...]
        o_ref[...] = y


def _peer(x_all, mod, gain, wq_heads, keys1, keys2, ut_bf16, vt_bf16, seq, n_groups, n_rows, final_gain):
    d = x_all.shape[1]
    t = PEER_TOKENS
    eb = PEER_EXPERT_BLOCK
    n_blocks = ut_bf16.shape[1] // eb

    def grp(i, e):
        return (jnp.minimum((i * t) // seq, n_groups), 0, 0)

    in_specs = [
        pl.BlockSpec((t, d), lambda i, e: (i, 0)),
        pl.BlockSpec((1, 6, d), grp),
        pl.BlockSpec((1, d), lambda i, e: (0, 0)),
        pl.BlockSpec(wq_heads.shape, lambda i, e: (0, 0, 0)),
        pl.BlockSpec(keys1.shape, lambda i, e: (0, 0)),
        pl.BlockSpec(keys2.shape, lambda i, e: (0, 0)),
        pl.BlockSpec((d, eb), lambda i, e: (0, jnp.minimum(e, n_blocks - 1))),
        pl.BlockSpec((d, eb), lambda i, e: (0, jnp.maximum(e - 1, 0))),
    ]
    args = [x_all, mod, gain, wq_heads, keys1, keys2, ut_bf16, vt_bf16]
    if final_gain is not None:
        in_specs.append(pl.BlockSpec((1, d), lambda i, e: (0, 0)))
        args.append(final_gain)
    per_head = (PEER_HEADS, PEER_NKEYS, t)
    return pl.pallas_call(
        functools.partial(_peer_kernel, final_norm=final_gain is not None, n_blocks=n_blocks),
        grid=(n_rows // t, n_blocks + 1),
        in_specs=in_specs,
        out_specs=pl.BlockSpec((t, d), lambda i, e: (i, 0)),
        out_shape=jax.ShapeDtypeStruct((n_rows, d), F32),
        scratch_shapes=[
            pltpu.VMEM((t, d), BF16),
            pltpu.VMEM(per_head, F32),
            pltpu.VMEM(per_head, F32),
            pltpu.VMEM(per_head, BF16),
            pltpu.VMEM(per_head, BF16),
            pltpu.VMEM((eb, t), BF16),
            pltpu.VMEM((eb, t), BF16),
            pltpu.VMEM((d, t), F32),
        ],
        compiler_params=_cparams(("parallel", "arbitrary")),
        name="peer",
    )(*args)


def _rope_tables(seq, pad_rows):
    half = HEAD_DIM // 2
    t = jnp.arange(seq, dtype=jnp.int32)
    row = (t // GRID_W).astype(F32)
    col = (t % GRID_W).astype(F32)
    inv = 1.0 / (ROPE_THETA ** (jnp.arange(0, half, 2, dtype=F32) / half))
    ang_r = row[:, None] * inv[None, :]
    ang_c = col[:, None] * inv[None, :]
    cos_h = jnp.concatenate([jnp.cos(ang_r)] * 2 + [jnp.cos(ang_c)] * 2, axis=1)
    sin_h = jnp.concatenate([-jnp.sin(ang_r), jnp.sin(ang_r), -jnp.sin(ang_c), jnp.sin(ang_c)], axis=1)
    cos_t = jnp.concatenate([jnp.tile(cos_h, (1, LANES // HEAD_DIM)), jnp.ones((pad_rows, LANES), F32)], axis=0)
    sin_t = jnp.concatenate([jnp.tile(sin_h, (1, LANES // HEAD_DIM)), jnp.zeros((pad_rows, LANES), F32)], axis=0)
    return cos_t, sin_t


def _head_norm_consts(q_gain, k_gain, n_q_heads, n_kv_heads):
    n_heads = n_q_heads + n_kv_heads
    head_of_lane = np.arange(n_heads * HEAD_DIM) // HEAD_DIM
    onehot = (head_of_lane[:, None] == np.arange(LANES)[None, :]).astype(np.float32)
    gsum = jnp.asarray(onehot / HEAD_DIM, BF16)
    gexp = jnp.asarray(onehot.T, BF16)
    hg = jnp.concatenate([jnp.tile(q_gain, n_q_heads), jnp.tile(k_gain, n_kv_heads)])[None, :]
    return gsum, gexp, hg


def kernel(x, c, ctx, c_ctx, ada_w, ada_b, norm1_g, norm2_g, final_g, e_w_in, e_q_norm_g, e_k_norm_g,
           e_pool_w, e_pool_scale, e_w_out, o_w_in, o_sink, o_w_out, p_w_q, p_keys1, p_keys2, p_u, p_v):
    batch, seq, d = x.shape
    ctx_len = ctx.shape[1]
    depth = ada_w.shape[0]
    n_lat = batch * seq
    n_all = n_lat + batch * ctx_len

    x_all = jnp.concatenate([x.reshape(n_lat, d), ctx.reshape(batch * ctx_len, d)], axis=0)
    cond = jnp.concatenate([c, c_ctx[None, :], jnp.zeros((8 - (batch + 1) % 8, d), F32)], axis=0)
    mods = _mod_vectors(cond, ada_w, ada_b)
    mods = mods.reshape(depth, cond.shape[0], 6, d)
    cos_t, sin_t = _rope_tables(seq, ROW_TILE)

    for i in range(depth):
        last = i == depth - 1
        mod = mods[i]
        j = i // 2
        g1 = norm1_g[i][None, :]
        g2 = norm2_g[i][None, :]
        if i % 2 == 0:
            consts = _head_norm_consts(e_q_norm_g[j], e_k_norm_g[j], B_Q_HEADS, B_KV_HEADS)
            u, q, k, v = _in_projection(
                x_all, mod, g1, e_w_in[j].astype(BF16), cos_t, sin_t, seq, batch,
                n_pre=POOL_DIM, n_q=B_Q_HEADS * HEAD_DIM, n_kv=B_KV_HEADS * HEAD_DIM, norm_consts=consts)
            w_pool = jax.scipy.linalg.block_diag(*[e_pool_w[j, g] for g in range(POOL_GROUPS)]).astype(BF16)
            p_scale = e_pool_scale[j][None, :]
            logit_bound = (NORM_BOUND_MARGIN * math.sqrt(HEAD_DIM) * jnp.max(jnp.abs(e_q_norm_g[j]))
                           * jnp.max(jnp.abs(e_k_norm_g[j]))).reshape(1)
            attn = _global_attention(logit_bound, q, k, v, batch, seq, ctx_len, latent=True)
            mixed = _pool_mixer(u, w_pool, p_scale, 0, batch, seq)
            if not last:
                attn = jnp.concatenate(
                    [attn, _global_attention(logit_bound, q, k, v, batch, seq, ctx_len, latent=False)], axis=0)
                mixed = jnp.concatenate([mixed, _pool_mixer(u, w_pool, p_scale, n_lat, batch, ctx_len)], axis=0)
            w_out = e_w_out[j].astype(BF16)
        else:
            _q = C_Q_HEADS * HEAD_DIM
            q, k, v = _in_projection(
                x_all, mod, g1, o_w_in[j].astype(BF16), cos_t, sin_t, seq, batch,
                n_pre=0, n_q=_q, n_kv=C_KV_HEADS * HEAD_DIM, norm_consts=None)
            attn = _window_attention(q, k, v, o_sink[j], batch, seq, ctx_len)
            if not last:
                raise NotImplementedError("context stream after an odd layer")
            mixed = None
            w_out = o_w_out[j].astype(BF16)
        n_rows = n_lat if last else n_all
        x_mid = _out_projection(x_all, mod, mixed, attn, w_out, seq, batch, n_rows)
        wq_heads = p_w_q[i].astype(BF16).reshape(d, PEER_HEADS, PEER_DKEY).transpose(1, 0, 2)
        x_all = _peer(x_mid, mod, g2, wq_heads, p_keys1[i].astype(BF16), p_keys2[i].astype(BF16),
                      (p_u[i] * (1.0 / math.sqrt(2.0))).astype(BF16).T, p_v[i].astype(BF16).T, seq, batch, n_rows,
                      final_g[None, :] if last else None)
    return x_all[:n_lat].reshape(batch, seq, d)
```

```python
import functools
import math

import jax
import jax.numpy as jnp
import numpy as np
from jax import lax
from jax.experimental import pallas as pl
from jax.experimental.pallas import tpu as pltpu

F32 = jnp.float32
BF16 = jnp.bfloat16

GRID_W = 64
HEAD_DIM = 64
ROPE_THETA = 10000.0
EPS = 1e-6
MASK_VALUE = -1e30
POOL_GROUPS = 4
POOL_GROUP_DIM = 64
POOL_DIM = POOL_GROUPS * POOL_GROUP_DIM
POOL_WINDOWS = (2, 4, 8, 16)
POOL_HALO = 8
B_Q_HEADS = 12
B_KV_HEADS = 4
C_Q_HEADS = 16
C_KV_HEADS = 4
WINDOW = 128
PEER_HEADS = 8
PEER_NKEYS = 128
PEER_DKEY = 256
PEER_TOPK = 16
LOWEST = -3.0e38
SAFE_LOGIT_BOUND = 40.0
NORM_BOUND_MARGIN = 1.05

LANES = 128
VMEM_LIMIT_BYTES = 56 * 1024 * 1024
ROW_TILE = 512
ATTN_Q_TILE = 512
ATTN_KV_TILE = 1024
WIN_Q_TILE = 256
WIN_BAND = WIN_Q_TILE + 2 * WINDOW
PEER_TOKENS = 512
PEER_EXPERT_BLOCK = 2048
POOL_CHUNK = 256


def _cparams(semantics):
    return pltpu.CompilerParams(dimension_semantics=semantics, vmem_limit_bytes=VMEM_LIMIT_BYTES)


def _dot(a, b):
    return jnp.dot(a, b, preferred_element_type=F32)


def _dot_nt(a, b):
    return lax.dot_general(a, b, (((1,), (1,)), ((), ())), preferred_element_type=F32)


def _split3(a):
    hi = a.astype(BF16)
    lo = (a - hi.astype(F32)).astype(BF16)
    return hi, lo


def _dot_f32ish(a, b):
    ah, al = _split3(a)
    bh, bl = _split3(b)
    return _dot(ah, bh) + (_dot(ah, bl) + _dot(al, bh))


def _norm_mod(x, gain, shift, scale):
    y = x * lax.rsqrt(jnp.mean(x * x, axis=-1, keepdims=True) + EPS)
    return (y * gain) * (1.0 + scale) + shift


def _mod_kernel(c_ref, w_ref, b_ref, o_ref):
    c = c_ref[...]
    act = c * (1.0 / (1.0 + jnp.exp(-c)))
    o_ref[0] = _dot_f32ish(act, w_ref[0]) + b_ref[0]


def _mod_vectors(cond, ada_w, ada_b):
    depth, d, d6 = ada_w.shape
    g8 = cond.shape[0]
    nblk = d6 // d
    return pl.pallas_call(
        _mod_kernel,
        grid=(depth, nblk),
        in_specs=[
            pl.BlockSpec((g8, d), lambda i, j: (0, 0)),
            pl.BlockSpec((1, d, d), lambda i, j: (i, 0, j)),
            pl.BlockSpec((1, 1, d), lambda i, j: (i, 0, j)),
        ],
        out_specs=pl.BlockSpec((1, g8, d), lambda i, j: (i, 0, j)),
        out_shape=jax.ShapeDtypeStruct((depth, g8, d6), F32),
        compiler_params=_cparams(("arbitrary", "arbitrary")),
        name="mod_vectors",
    )(cond, ada_w, ada_b.reshape(depth, 1, d6))


def _swap16(x):
    lane = lax.broadcasted_iota(jnp.int32, x.shape, 1)
    return jnp.where((lane & 16) == 0, pltpu.roll(x, LANES - 16, 1), pltpu.roll(x, 16, 1))


def _inproj_kernel(x_ref, mod_ref, g_ref, w_ref, cos_ref, sin_ref, *rest, n_pre, n_qk, qk_norm):
    if qk_norm:
        gsum_ref, gexp_ref, hg_ref, *outs = rest
    else:
        outs = rest
    x = x_ref[...]
    h = _norm_mod(x, g_ref[...], mod_ref[0, 0:1, :], mod_ref[0, 1:2, :])
    proj = _dot(h.astype(BF16), w_ref[...])
    qk = proj[:, n_pre:n_pre + n_qk]
    if qk_norm:
        ms = _dot((qk * qk).astype(BF16), gsum_ref[...])
        rinv = lax.rsqrt(ms + EPS)
        r_hi, r_lo = _split3(rinv)
        qk = qk * (_dot(r_hi, gexp_ref[...]) + _dot(r_lo, gexp_ref[...])) * hg_ref[...]
    cos = cos_ref[...]
    sin = sin_ref[...]
    blocks = []
    for j in range(n_qk // LANES):
        blk = qk[:, j * LANES:(j + 1) * LANES]
        blocks.append(blk * cos + _swap16(blk) * sin)
    if n_pre:
        u_ref, q_ref, k_ref, v_ref = outs
        u_ref[...] = proj[:, :n_pre]
    else:
        q_ref, k_ref, v_ref = outs
    n_q = q_ref.shape[1]
    scale = HEAD_DIM ** -0.5
    for j, blk in enumerate(blocks):
        lo = j * LANES
        if lo < n_q:
            q_ref[:, lo:lo + LANES] = (blk * scale).astype(BF16)
        else:
            k_ref[:, lo - n_q:lo - n_q + LANES] = blk.astype(BF16)
    v_ref[...] = proj[:, n_pre + n_qk:].astype(BF16)


def _in_projection(x_all, mod, gain, w_bf16, cos_t, sin_t, seq, n_groups, *, n_pre, n_q, n_kv, norm_consts):
    n, d = x_all.shape
    tm = ROW_TILE
    n_qk = n_q + n_kv
    d_in = w_bf16.shape[1]
    lat_tiles = (n_groups * seq) // tm
    seq_tiles = seq // tm

    def grp(i):
        return (jnp.minimum((i * tm) // seq, n_groups), 0, 0)

    def rope_idx(i):
        return (jnp.where(i < lat_tiles, i % seq_tiles, seq_tiles), 0)

    in_specs = [
        pl.BlockSpec((tm, d), lambda i: (i, 0)),
        pl.BlockSpec((1, 6, d), grp),
        pl.BlockSpec((1, d), lambda i: (0, 0)),
        pl.BlockSpec((d, d_in), lambda i: (0, 0)),
        pl.BlockSpec((tm, LANES), rope_idx),
        pl.BlockSpec((tm, LANES), rope_idx),
    ]
    args = [x_all, mod, gain, w_bf16, cos_t, sin_t]
    if norm_consts is not None:
        gsum, gexp, hg = norm_consts
        in_specs += [
            pl.BlockSpec(gsum.shape, lambda i: (0, 0)),
            pl.BlockSpec(gexp.shape, lambda i: (0, 0)),
            pl.BlockSpec(hg.shape, lambda i: (0, 0)),
        ]
        args += [gsum, gexp, hg]
    out_specs, out_shape = [], []
    if n_pre:
        out_specs.append(pl.BlockSpec((tm, n_pre), lambda i: (i, 0)))
        out_shape.append(jax.ShapeDtypeStruct((n, n_pre), F32))
    for width in (n_q, n_kv, n_kv):
        out_specs.append(pl.BlockSpec((tm, width), lambda i: (i, 0)))
        out_shape.append(jax.ShapeDtypeStruct((n, width), BF16))
    return pl.pallas_call(
        functools.partial(_inproj_kernel, n_pre=n_pre, n_qk=n_qk, qk_norm=norm_consts is not None),
        grid=(n // tm,),
        in_specs=in_specs,
        out_specs=out_specs,
        out_shape=out_shape,
        compiler_params=_cparams(("parallel",)),
        name="in_projection",
    )(*args)


def _head_query(q_ref, head, kv_head):
    blk = q_ref[:, (head // 2) * LANES:(head // 2 + 1) * LANES]
    if head % 2 != kv_head % 2:
        blk = jnp.concatenate([blk[:, HEAD_DIM:], blk[:, :HEAD_DIM]], axis=1)
    lane = lax.broadcasted_iota(jnp.int32, blk.shape, 1)
    keep = (lane >= HEAD_DIM) if kv_head % 2 else (lane < HEAD_DIM)
    return jnp.where(keep, blk, jnp.zeros_like(blk))


def _store_heads(o_ref, outs, kv_of_head):
    for p in range(len(outs) // 2):
        halves = []
        for h in (2 * p, 2 * p + 1):
            half = kv_of_head(h) % 2
            halves.append(outs[h][:, half * HEAD_DIM:(half + 1) * HEAD_DIM])
        o_ref[:, p * LANES:(p + 1) * LANES] = jnp.concatenate(halves, axis=1).astype(o_ref.dtype)


def _global_attn_kernel(bound_ref, q_ref, kc_ref, vc_ref, *rest, n_heads, group, n_lat_chunks):
    if n_lat_chunks:
        kl_ref, vl_ref, o_ref = rest
    else:
        (o_ref,) = rest
    tk = ATTN_KV_TILE

    def attend(bounded):
        outs = []
        for j in range(n_heads // group):
            cb = slice((j // 2) * LANES, (j // 2 + 1) * LANES)
            qzs = [_head_query(q_ref, h, j) for h in range(j * group, (j + 1) * group)]
            kc = kc_ref[:, cb]
            vc = vc_ref[:, cb]
            state = []
            for qz in qzs:
                s = _dot_nt(qz, kc)
                m = jnp.zeros((s.shape[0], 1), F32) if bounded else jnp.max(s, axis=-1, keepdims=True)
                p = jnp.exp(s) if bounded else jnp.exp(s - m)
                state.append((m, jnp.sum(p, axis=-1, keepdims=True), _dot(p.astype(BF16), vc)))

            if n_lat_chunks:
                def step(c, carry, qzs=qzs, cb=cb):
                    rows = pl.ds(pl.multiple_of(c * tk, tk), tk)
                    k = kl_ref[rows, cb]
                    v = vl_ref[rows, cb]
                    new = []
                    for qz, (m, l, o) in zip(qzs, carry):
                        s = _dot_nt(qz, k)
                        if bounded:
                            p = jnp.exp(s)
                            l = l + jnp.sum(p, axis=-1, keepdims=True)
                            o = o + _dot(p.astype(BF16), v)
                        else:
                            m_new = jnp.maximum(m, jnp.max(s, axis=-1, keepdims=True))
                            alpha = jnp.exp(m - m_new)
                            p = jnp.exp(s - m_new)
                            l = alpha * l + jnp.sum(p, axis=-1, keepdims=True)
                            o = alpha * o + _dot(p.astype(BF16), v)
                            m = m_new
                        new.append((m, l, o))
                    return tuple(new)

                state = lax.fori_loop(0, n_lat_chunks, step, tuple(state))
            outs.extend(o / l for (_, l, o) in state)
        _store_heads(o_ref, outs, lambda h: h // group)

    small = bound_ref[0] <= SAFE_LOGIT_BOUND
    pl.when(small)(functools.partial(attend, True))
    pl.when(jnp.logical_not(small))(functools.partial(attend, False))


def _global_attention(logit_bound, q_all, k_all, v_all, batch, seq, ctx_len, *, latent):
    n_q = q_all.shape[1]
    n_kv = k_all.shape[1]
    n_heads = n_q // HEAD_DIM
    group = n_heads // (n_kv // HEAD_DIM)
    ctx_blk0 = (batch * seq) // ctx_len
    if latent:
        tq = ATTN_Q_TILE
        q_tiles = seq // tq
        q_idx = lambda b, i: (b * q_tiles + i, 0)
        out_rows = batch * seq
    else:
        tq = ctx_len
        q_tiles = 1
        q_idx = lambda b, i: (ctx_blk0 + b, 0)
        out_rows = batch * ctx_len
    in_specs = [
        pl.BlockSpec(memory_space=pltpu.SMEM),
        pl.BlockSpec((tq, n_q), q_idx),
        pl.BlockSpec((ctx_len, n_kv), lambda b, i: (ctx_blk0 + b, 0)),
        pl.BlockSpec((ctx_len, n_kv), lambda b, i: (ctx_blk0 + b, 0)),
    ]
    args = [logit_bound, q_all, k_all, v_all]
    if latent:
        in_specs += [pl.BlockSpec((seq, n_kv), lambda b, i: (b, 0)),
                     pl.BlockSpec((seq, n_kv), lambda b, i: (b, 0))]
        args += [k_all, v_all]
    return pl.pallas_call(
        functools.partial(_global_attn_kernel, n_heads=n_heads, group=group,
                          n_lat_chunks=(seq // ATTN_KV_TILE) if latent else 0),
        grid=(batch, q_tiles),
        in_specs=in_specs,
        out_specs=pl.BlockSpec((tq, n_q), lambda b, i: (b * q_tiles + i, 0)),
        out_shape=jax.ShapeDtypeStruct((out_rows, n_q), BF16),
        compiler_params=_cparams(("parallel", "arbitrary")),
        name="global_attention" if latent else "context_attention",
    )(*args)


def _window_attn_kernel(sink_ref, q_ref, kc_ref, vc_ref, kl_ref, vl_ref, o_ref, *, n_heads, group, seq):
    tq = WIN_Q_TILE
    band = WIN_BAND
    start = pl.program_id(1) * tq
    k0 = jnp.clip(start - WINDOW, 0, seq - band)
    k0 = pl.multiple_of(k0, WINDOW)
    rows = pl.ds(k0, band)
    qpos = start + lax.broadcasted_iota(jnp.int32, (tq, band), 0)
    kpos = k0 + lax.broadcasted_iota(jnp.int32, (tq, band), 1)
    valid = jnp.abs(qpos - kpos) <= WINDOW
    outs = []
    for h in range(n_heads):
        j = h // group
        cb = slice((j // 2) * LANES, (j // 2 + 1) * LANES)
        qz = _head_query(q_ref, h, j)
        s_c = _dot_nt(qz, kc_ref[:, cb])
        s_b = jnp.where(valid, _dot_nt(qz, kl_ref[rows, cb]), MASK_VALUE)
        sink = sink_ref[h]
        m = jnp.maximum(jnp.maximum(jnp.max(s_c, axis=-1, keepdims=True),
                                    jnp.max(s_b, axis=-1, keepdims=True)), sink)
        e_c = jnp.exp(s_c - m)
        e_b = jnp.exp(s_b - m)
        denom = (jnp.sum(e_c, axis=-1, keepdims=True) + jnp.sum(e_b, axis=-1, keepdims=True)
                 + jnp.exp(sink - m))
        o = _dot(e_c.astype(BF16), vc_ref[:, cb]) + _dot(e_b.astype(BF16), vl_ref[rows, cb])
        outs.append(o / denom)
    _store_heads(o_ref, outs, lambda h: h // group)


def _window_attention(q_all, k_all, v_all, sink, batch, seq, ctx_len):
    n_q = q_all.shape[1]
    n_kv = k_all.shape[1]
    n_heads = n_q // HEAD_DIM
    group = n_heads // (n_kv // HEAD_DIM)
    tq = WIN_Q_TILE
    q_tiles = seq // tq
    ctx_blk0 = (batch * seq) // ctx_len
    return pl.pallas_call(
        functools.partial(_window_attn_kernel, n_heads=n_heads, group=group, seq=seq),
        grid=(batch, q_tiles),
        in_specs=[
            pl.BlockSpec(memory_space=pltpu.SMEM),
            pl.BlockSpec((tq, n_q), lambda b, i: (b * q_tiles + i, 0)),
            pl.BlockSpec((ctx_len, n_kv), lambda b, i: (ctx_blk0 + b, 0)),
            pl.BlockSpec((ctx_len, n_kv), lambda b, i: (ctx_blk0 + b, 0)),
            pl.BlockSpec((seq, n_kv), lambda b, i: (b, 0)),
            pl.BlockSpec((seq, n_kv), lambda b, i: (b, 0)),
        ],
        out_specs=pl.BlockSpec((tq, n_q), lambda b, i: (b * q_tiles + i, 0)),
        out_shape=jax.ShapeDtypeStruct((batch * seq, n_q), BF16),
        compiler_params=_cparams(("parallel", "arbitrary")),
        name="window_attention",
    )(sink, q_all, k_all, v_all, k_all, v_all)


def _pool_kernel(u_ref, w_ref, sc_ref, o_ref, pad_ref, *, length):
    halo = POOL_HALO
    ch = POOL_CHUNK
    zeros = jnp.zeros((halo, POOL_DIM), F32)
    pad_ref[0:halo, :] = zeros
    pad_ref[halo + length:halo + length + halo, :] = zeros
    pad_ref[halo:halo + length, :] = u_ref[...]
    lane = lax.broadcasted_iota(jnp.int32, (ch, POOL_DIM), 1)
    grp = lane // POOL_GROUP_DIM
    half_w = jnp.left_shift(1, grp)
    row0 = lax.broadcasted_iota(jnp.int32, (ch, POOL_DIM), 0)
    win_rows = ch + 2 * halo

    def chunk(c, carry):
        r0 = pl.multiple_of(c * ch, ch)
        win = pad_ref[pl.ds(r0, win_rows), :]
        acc = jnp.zeros((ch, POOL_DIM), F32)
        for dlt in range(-halo, halo):
            shifted = pltpu.roll(win, win_rows - (halo + dlt), 0)[:ch] if halo + dlt else win[:ch]
            inside = (half_w >= -dlt) if dlt < 0 else (half_w > dlt)
            acc = acc + jnp.where(inside, shifted, 0.0)
        t = row0 + r0
        hi = jnp.minimum(t + half_w, length)
        lo = jnp.maximum(t - half_w, 0)
        centre = win[halo:halo + ch]
        pooled = acc / (hi - lo).astype(F32) - centre
        mixed = _dot(pooled.astype(BF16), w_ref[...]) * sc_ref[...]
        o_ref[pl.ds(r0, ch), :] = mixed.astype(o_ref.dtype)
        return carry

    lax.fori_loop(0, length // ch, chunk, 0)


def _pool_mixer(u_all, w_blockdiag, scale, row0, n_seq, length):
    blk0 = row0 // length
    return pl.pallas_call(
        functools.partial(_pool_kernel, length=length),
        grid=(n_seq,),
        in_specs=[
            pl.BlockSpec((length, POOL_DIM), lambda s: (blk0 + s, 0)),
            pl.BlockSpec((POOL_DIM, POOL_DIM), lambda s: (0, 0)),
            pl.BlockSpec((1, POOL_DIM), lambda s: (0, 0)),
        ],
        out_specs=pl.BlockSpec((length, POOL_DIM), lambda s: (s, 0)),
        out_shape=jax.ShapeDtypeStruct((n_seq * length, POOL_DIM), BF16),
        scratch_shapes=[pltpu.VMEM((length + 2 * POOL_HALO, POOL_DIM), F32)],
        compiler_params=_cparams(("parallel",)),
        name="pool_mixer",
    )(u_all, w_blockdiag, scale)


def _outproj_kernel(x_ref, mod_ref, *rest, n_pool):
    if n_pool:
        m_ref, a_ref, w_ref, o_ref = rest
        y = _dot(m_ref[...], w_ref[:n_pool, :]) + _dot(a_ref[...], w_ref[n_pool:, :])
    else:
        a_ref, w_ref, o_ref = rest
        y = _dot(a_ref[...], w_ref[...])
    o_ref[...] = x_ref[...] + mod_ref[0, 2:3, :] * y


def _out_projection(x_all, mod, mixed, attn, w_bf16, seq, n_groups, n_rows):
    d = x_all.shape[1]
    tm = ROW_TILE
    n_pool = 0 if mixed is None else mixed.shape[1]
    n_att = attn.shape[1]

    def grp(i):
        return (jnp.minimum((i * tm) // seq, n_groups), 0, 0)

    in_specs = [pl.BlockSpec((tm, d), lambda i: (i, 0)), pl.BlockSpec((1, 6, d), grp)]
    args = [x_all, mod]
    if n_pool:
        in_specs.append(pl.BlockSpec((tm, n_pool), lambda i: (i, 0)))
        args.append(mixed)
    in_specs += [pl.BlockSpec((tm, n_att), lambda i: (i, 0)),
                 pl.BlockSpec((n_pool + n_att, d), lambda i: (0, 0))]
    args += [attn, w_bf16]
    return pl.pallas_call(
        functools.partial(_outproj_kernel, n_pool=n_pool),
        grid=(n_rows // tm,),
        in_specs=in_specs,
        out_specs=pl.BlockSpec((tm, d), lambda i: (i, 0)),
        out_shape=jax.ShapeDtypeStruct((n_rows, d), F32),
        compiler_params=_cparams(("parallel",)),
        name="out_projection",
    )(*args)


def _oddeven_merge(lo, hi, r):
    step = r * 2
    if step < hi - lo:
        yield from _oddeven_merge(lo, hi, step)
        yield from _oddeven_merge(lo + r, hi, step)
        yield from [(i, i + r) for i in range(lo + r, hi - r, step)]
    else:
        yield (lo, lo + r)


def _oddeven_sort(lo, hi):
    if hi - lo >= 1:
        mid = lo + (hi - lo) // 2
        yield from _oddeven_sort(lo, mid)
        yield from _oddeven_sort(mid + 1, hi)
        yield from _oddeven_merge(lo, hi, 1)


_SORT16 = tuple(_oddeven_sort(0, 15))
_SORT8 = tuple(_oddeven_sort(0, 7))
_TOP_PAIRS = tuple((r, c) for r in range(PEER_TOPK) for c in range(PEER_TOPK) if (r + 1) * (c + 1) <= PEER_TOPK)


def _cmpx(xs, i, j):
    a, b = xs[i], xs[j]
    xs[i] = jnp.maximum(a, b)
    xs[j] = jnp.minimum(a, b)


def _sort_desc(xs, net):
    xs = list(xs)
    for i, j in net:
        _cmpx(xs, i, j)
    return xs


def _bitonic_merge_desc(xs):
    xs = list(xs)
    n = len(xs)
    d = n // 2
    while d:
        for i in range(n):
            if not i & d:
                _cmpx(xs, i, i + d)
        d //= 2
    return xs


def _merge_top(a, b):
    n = len(a)
    return _bitonic_merge_desc([jnp.maximum(a[i], b[n - 1 - i]) for i in range(n)])


def _top16_sorted(rows):
    w = _sort_desc(rows, _SORT16)
    for shift in (4, 2, 1):
        w = _merge_top(w, [pltpu.roll(x, shift, 0) for x in w])
    return w


def _pair_threshold(pair):
    low = jnp.full_like(pair[(0, 0)], LOWEST)
    row0 = [pair[(0, c)] for c in range(16)]
    col0 = [pair[(r, 0)] for r in range(1, 16)] + [low]
    t01 = _merge_top(row0, col0)
    g2 = _bitonic_merge_desc([pair[(1, c)] for c in range(1, 8)] + [low] * 3
                             + [pair[(r, 1)] for r in range(7, 1, -1)])
    rest = [pair[k] for k in ((2, 2), (2, 3), (2, 4), (3, 2), (4, 2), (3, 3))]
    g3 = _sort_desc(rest + [low, low], _SORT8) + [low] * 8
    t23 = _merge_top(g2, g3)
    tau = None
    for i in range(16):
        m = jnp.maximum(t01[i], t23[15 - i])
        tau = m if tau is None else jnp.minimum(tau, m)
    return tau


def _route_tile(s1, s2):
    assert PEER_TOPK == 16 and PEER_NKEYS == 128
    n_col = s1.shape[1] // LANES
    assert 8 % n_col == 0
    span = 8 // n_col
    cols = []
    for j in range(n_col):
        c1 = s1[:, j * LANES:(j + 1) * LANES]
        c2 = s2[:, j * LANES:(j + 1) * LANES]
        rows1 = [c1[8 * g:8 * g + 8] for g in range(16)]
        rows2 = [c2[8 * g:8 * g + 8] for g in range(16)]
        cols.append((rows1, rows2, _top16_sorted(rows1), _top16_sorted(rows2)))

    sub = lax.broadcasted_iota(jnp.int32, (8, LANES), 0)

    def pack(vals):
        out = vals[-1]
        for j in range(n_col - 2, -1, -1):
            out = jnp.where(sub < (j + 1) * span, vals[j], out)
        return out

    def unpack(x, j):
        return jnp.broadcast_to(x[j * span:j * span + 1, :], (8, LANES))

    v1 = [pack([c[2][r] for c in cols]) for r in range(16)]
    v2 = [pack([c[3][r] for c in cols]) for r in range(16)]
    pair = {(r, c): v1[r] + v2[c] for (r, c) in _TOP_PAIRS}
    tau = _pair_threshold(pair)
    e1 = [jnp.exp(v - v1[0]) for v in v1]
    e2 = [jnp.exp(v - v2[0]) for v in v2]
    z = None
    for (r, c) in _TOP_PAIRS:
        term = jnp.where(pair[(r, c)] >= tau, e1[r] * e2[c], 0.0)
        z = term if z is None else z + term
    inv_z = (1.0 / math.sqrt(2.0)) / z
    n_top = jnp.zeros_like(tau)
    for c in range(16):
        n_top = jnp.where(pair[(0, c)] >= tau, c + 1.0, n_top)

    out = [[], [], [], []]
    for j, (rows1, rows2, t1, t2) in enumerate(cols):
        tau_j, inv_z_j, n_top_j = unpack(tau, j), unpack(inv_z, j), unpack(n_top, j)
        cnt, p1n, rank2, p2 = [], [], [], []
        for g in range(16):
            a = rows1[g]
            n = jnp.zeros_like(a)
            for c in range(8):
                n = jnp.where(a + t2[c] >= tau_j, c + 1.0, n)
            cnt.append(jnp.where(a >= t1[0], n_top_j, n))
            p1n.append(jnp.exp(a - t1[0]) * inv_z_j)
            b = rows2[g]
            k = jnp.zeros_like(b)
            for r in range(16):
                k = jnp.where(t2[r] > b, r + 1.0, k)
            rank2.append(k)
            p2.append(jnp.exp(b - t2[0]))
        for dst, parts in zip(out, (cnt, p1n, rank2, p2)):
            dst.append(jnp.concatenate(parts, axis=0))
    return tuple(jnp.concatenate(parts, axis=1) for parts in out)


def _peer_kernel(x_ref, mod_ref, g_ref, wq_ref, k1_ref, k2_ref, ut_ref, vt_ref, *rest, final_norm, n_blocks):
    if final_norm:
        fg_ref, o_ref, *scratch = rest
    else:
        o_ref, *scratch = rest
    hb_ref, cnt_ref, p1_ref, rank_ref, p2_ref, gw0_ref, gw1_ref, acc_ref = scratch
    s = pl.program_id(1)
    eb, t = gw0_ref.shape
    blocks = eb // PEER_NKEYS
    half = PEER_DKEY // 2

    @pl.when(s == 0)
    def _route():
        x = x_ref[...]
        h2 = _norm_mod(x, g_ref[...], mod_ref[0, 3:4, :], mod_ref[0, 4:5, :])
        hb_ref[...] = h2.astype(BF16)
        acc_ref[...] = jnp.zeros_like(acc_ref)

        def head(h, carry):
            qh = _dot(hb_ref[...], wq_ref[h])
            s1 = _dot_nt(k1_ref[...], qh[:, :half].astype(BF16))
            s2 = _dot_nt(k2_ref[...], qh[:, half:].astype(BF16))
            cnt, p1n, rank2, p2 = _route_tile(s1, s2)
            cnt_ref[h] = cnt
            p1_ref[h] = p1n
            rank_ref[h] = rank2.astype(BF16)
            p2_ref[h] = p2.astype(BF16)
            return carry

        lax.fori_loop(0, PEER_HEADS, head, 0)

    def stage(parity, build, consume):
        gw_w, gw_r = (gw0_ref, gw1_ref) if parity == 0 else (gw1_ref, gw0_ref)
        reps = PEER_NKEYS // 16
        span = 2 * PEER_NKEYS
        n_pairs = blocks // 2

        def build_pair(pair):
            y_pair = _dot(hb_ref[...], ut_ref[:, pair * span:(pair + 1) * span]).T
            for i in (2 * pair, 2 * pair + 1):
                a = s * blocks + i
                w = None
                for h in range(PEER_HEADS):
                    cnt16 = jnp.broadcast_to(cnt_ref[h, pl.ds(a, 1), :], (16, t)).astype(BF16)
                    p16 = jnp.broadcast_to(p1_ref[h, pl.ds(a, 1), :], (16, t)).astype(BF16)
                    sel = rank_ref[h] < jnp.concatenate([cnt16] * reps, axis=0)
                    term = (jnp.where(sel, p2_ref[h], jnp.zeros((), BF16))
                            * jnp.concatenate([p16] * reps, axis=0))
                    w = term if w is None else w + term
                y = y_pair[(i % 2) * PEER_NKEYS:(i % 2 + 1) * PEER_NKEYS, :]
                act = y * (1.0 + lax.erf(y))
                gw_w[i * PEER_NKEYS:(i + 1) * PEER_NKEYS, :] = act.astype(BF16) * w

        if build:
            for k in range(n_pairs):
                build_pair(k)
        if consume:
            acc_ref[...] += _dot(vt_ref[...], gw_r[...])

    pl.when(s == 0)(functools.partial(stage, 0, True, False))
    steady = (s >= 1) & (s < n_blocks)
    pl.when(steady & (s % 2 == 0))(functools.partial(stage, 0, True, True))
    pl.when(steady & (s % 2 == 1))(functools.partial(stage, 1, True, True))
    pl.when(s == n_blocks)(functools.partial(stage, n_blocks % 2, False, True))

    @pl.when(s == n_blocks)
    def _finish():
        y = x_ref[...] + mod_ref[0, 5:6, :] * acc_ref[...].T
        if final_norm:
            y = y * lax.rsqrt(jnp.mean(y * y, axis=-1, keepdims=True) + EPS) * fg_ref[...]
        o_ref[...] = y


def _peer(x_all, mod, gain, wq_heads, keys1, keys2, ut_bf16, vt_bf16, seq, n_groups, n_rows, final_gain):
    d = x_all.shape[1]
    t = PEER_TOKENS
    eb = PEER_EXPERT_BLOCK
    n_blocks = ut_bf16.shape[1] // eb

    def grp(i, e):
        return (jnp.minimum((i * t) // seq, n_groups), 0, 0)

    in_specs = [
        pl.BlockSpec((t, d), lambda i, e: (i, 0)),
        pl.BlockSpec((1, 6, d), grp),
        pl.BlockSpec((1, d), lambda i, e: (0, 0)),
        pl.BlockSpec(wq_heads.shape, lambda i, e: (0, 0, 0)),
        pl.BlockSpec(keys1.shape, lambda i, e: (0, 0)),
        pl.BlockSpec(keys2.shape, lambda i, e: (0, 0)),
        pl.BlockSpec((d, eb), lambda i, e: (0, jnp.minimum(e, n_blocks - 1))),
        pl.BlockSpec((d, eb), lambda i, e: (0, jnp.maximum(e - 1, 0))),
    ]
    args = [x_all, mod, gain, wq_heads, keys1, keys2, ut_bf16, vt_bf16]
    if final_gain is not None:
        in_specs.append(pl.BlockSpec((1, d), lambda i, e: (0, 0)))
        args.append(final_gain)
    per_head = (PEER_HEADS, PEER_NKEYS, t)
    return pl.pallas_call(
        functools.partial(_peer_kernel, final_norm=final_gain is not None, n_blocks=n_blocks),
        grid=(n_rows // t, n_blocks + 1),
        in_specs=in_specs,
        out_specs=pl.BlockSpec((t, d), lambda i, e: (i, 0)),
        out_shape=jax.ShapeDtypeStruct((n_rows, d), F32),
        scratch_shapes=[
            pltpu.VMEM((t, d), BF16),
            pltpu.VMEM(per_head, F32),
            pltpu.VMEM(per_head, F32),
            pltpu.VMEM(per_head, BF16),
            pltpu.VMEM(per_head, BF16),
            pltpu.VMEM((eb, t), BF16),
            pltpu.VMEM((eb, t), BF16),
            pltpu.VMEM((d, t), F32),
        ],
        compiler_params=_cparams(("parallel", "arbitrary")),
        name="peer",
    )(*args)


def _rope_tables(seq, pad_rows):
    half = HEAD_DIM // 2
    t = jnp.arange(seq, dtype=jnp.int32)
    row = (t // GRID_W).astype(F32)
    col = (t % GRID_W).astype(F32)
    inv = 1.0 / (ROPE_THETA ** (jnp.arange(0, half, 2, dtype=F32) / half))
    ang_r = row[:, None] * inv[None, :]
    ang_c = col[:, None] * inv[None, :]
    cos_h = jnp.concatenate([jnp.cos(ang_r)] * 2 + [jnp.cos(ang_c)] * 2, axis=1)
    sin_h = jnp.concatenate([-jnp.sin(ang_r), jnp.sin(ang_r), -jnp.sin(ang_c), jnp.sin(ang_c)], axis=1)
    cos_t = jnp.concatenate([jnp.tile(cos_h, (1, LANES // HEAD_DIM)), jnp.ones((pad_rows, LANES), F32)], axis=0)
    sin_t = jnp.concatenate([jnp.tile(sin_h, (1, LANES // HEAD_DIM)), jnp.zeros((pad_rows, LANES), F32)], axis=0)
    return cos_t, sin_t


def _head_norm_consts(q_gain, k_gain, n_q_heads, n_kv_heads):
    n_heads = n_q_heads + n_kv_heads
    head_of_lane = np.arange(n_heads * HEAD_DIM) // HEAD_DIM
    onehot = (head_of_lane[:, None] == np.arange(LANES)[None, :]).astype(np.float32)
    gsum = jnp.asarray(onehot / HEAD_DIM, BF16)
    gexp = jnp.asarray(onehot.T, BF16)
    hg = jnp.concatenate([jnp.tile(q_gain, n_q_heads), jnp.tile(k_gain, n_kv_heads)])[None, :]
    return gsum, gexp, hg


def kernel(x, c, ctx, c_ctx, ada_w, ada_b, norm1_g, norm2_g, final_g, e_w_in, e_q_norm_g, e_k_norm_g,
           e_pool_w, e_pool_scale, e_w_out, o_w_in, o_sink, o_w_out, p_w_q, p_keys1, p_keys2, p_u, p_v):
    batch, seq, d = x.shape
    ctx_len = ctx.shape[1]
    depth = ada_w.shape[0]
    n_lat = batch * seq
    n_all = n_lat + batch * ctx_len

    x_all = jnp.concatenate([x.reshape(n_lat, d), ctx.reshape(batch * ctx_len, d)], axis=0)
    cond = jnp.concatenate([c, c_ctx[None, :], jnp.zeros((8 - (batch + 1) % 8, d), F32)], axis=0)
    mods = _mod_vectors(cond, ada_w, ada_b)
    mods = mods.reshape(depth, cond.shape[0], 6, d)
    cos_t, sin_t = _rope_tables(seq, ROW_TILE)

    for i in range(depth):
        last = i == depth - 1
        mod = mods[i]
        j = i // 2
        g1 = norm1_g[i][None, :]
        g2 = norm2_g[i][None, :]
        if i % 2 == 0:
            consts = _head_norm_consts(e_q_norm_g[j], e_k_norm_g[j], B_Q_HEADS, B_KV_HEADS)
            u, q, k, v = _in_projection(
                x_all, mod, g1, e_w_in[j].astype(BF16), cos_t, sin_t, seq, batch,
                n_pre=POOL_DIM, n_q=B_Q_HEADS * HEAD_DIM, n_kv=B_KV_HEADS * HEAD_DIM, norm_consts=consts)
            w_pool = jax.scipy.linalg.block_diag(*[e_pool_w[j, g] for g in range(POOL_GROUPS)]).astype(BF16)
            p_scale = e_pool_scale[j][None, :]
            logit_bound = (NORM_BOUND_MARGIN * math.sqrt(HEAD_DIM) * jnp.max(jnp.abs(e_q_norm_g[j]))
                           * jnp.max(jnp.abs(e_k_norm_g[j]))).reshape(1)
            attn = _global_attention(logit_bound, q, k, v, batch, seq, ctx_len, latent=True)
            mixed = _pool_mixer(u, w_pool, p_scale, 0, batch, seq)
            if not last:
                attn = jnp.concatenate(
                    [attn, _global_attention(logit_bound, q, k, v, batch, seq, ctx_len, latent=False)], axis=0)
                mixed = jnp.concatenate([mixed, _pool_mixer(u, w_pool, p_scale, n_lat, batch, ctx_len)], axis=0)
            w_out = e_w_out[j].astype(BF16)
        else:
            _q = C_Q_HEADS * HEAD_DIM
            q, k, v = _in_projection(
                x_all, mod, g1, o_w_in[j].astype(BF16), cos_t, sin_t, seq, batch,
                n_pre=0, n_q=_q, n_kv=C_KV_HEADS * HEAD_DIM, norm_consts=None)
            attn = _window_attention(q, k, v, o_sink[j], batch, seq, ctx_len)
            if not last:
                raise NotImplementedError("context stream after an odd layer")
            mixed = None
            w_out = o_w_out[j].astype(BF16)
        n_rows = n_lat if last else n_all
        x_mid = _out_projection(x_all, mod, mixed, attn, w_out, seq, batch, n_rows)
        wq_heads = p_w_q[i].astype(BF16).reshape(d, PEER_HEADS, PEER_DKEY).transpose(1, 0, 2)
        x_all = _peer(x_mid, mod, g2, wq_heads, p_keys1[i].astype(BF16), p_keys2[i].astype(BF16),
                      (p_u[i] * (1.0 / math.sqrt(2.0))).astype(BF16).T, p_v[i].astype(BF16).T, seq, batch, n_rows,
                      final_g[None, :] if last else None)
    return x_all[:n_lat].reshape(batch, seq, d)
```

```python
import functools
import math

import jax
import jax.numpy as jnp
import numpy as np
from jax import lax
from jax.experimental import pallas as pl
from jax.experimental.pallas import tpu as pltpu

F32 = jnp.float32
BF16 = jnp.bfloat16

GRID_W = 64
HEAD_DIM = 64
ROPE_THETA = 10000.0
EPS = 1e-6
MASK_VALUE = -1e30
POOL_GROUPS = 4
POOL_GROUP_DIM = 64
POOL_DIM = POOL_GROUPS * POOL_GROUP_DIM
POOL_WINDOWS = (2, 4, 8, 16)
POOL_HALO = 8
B_Q_HEADS = 12
B_KV_HEADS = 4
C_Q_HEADS = 16
C_KV_HEADS = 4
WINDOW = 128
PEER_HEADS = 8
PEER_NKEYS = 128
PEER_DKEY = 256
PEER_TOPK = 16
LOWEST = -3.0e38
SAFE_LOGIT_BOUND = 40.0
NORM_BOUND_MARGIN = 1.05

LANES = 128
VMEM_LIMIT_BYTES = 56 * 1024 * 1024
ROW_TILE = 512
ATTN_Q_TILE = 512
ATTN_KV_TILE = 1024
WIN_Q_TILE = 256
WIN_BAND = WIN_Q_TILE + 2 * WINDOW
PEER_TOKENS = 512
PEER_EXPERT_BLOCK = 2048
POOL_CHUNK = 256


def _cparams(semantics):
    return pltpu.CompilerParams(dimension_semantics=semantics, vmem_limit_bytes=VMEM_LIMIT_BYTES)


def _dot(a, b):
    return jnp.dot(a, b, preferred_element_type=F32)


def _dot_nt(a, b):
    return lax.dot_general(a, b, (((1,), (1,)), ((), ())), preferred_element_type=F32)


def _split3(a):
    hi = a.astype(BF16)
    lo = (a - hi.astype(F32)).astype(BF16)
    return hi, lo


def _dot_f32ish(a, b):
    ah, al = _split3(a)
    bh, bl = _split3(b)
    return _dot(ah, bh) + (_dot(ah, bl) + _dot(al, bh))


def _norm_mod(x, gain, shift, scale):
    y = x * lax.rsqrt(jnp.mean(x * x, axis=-1, keepdims=True) + EPS)
    return (y * gain) * (1.0 + scale) + shift


def _mod_kernel(c_ref, w_ref, b_ref, o_ref):
    c = c_ref[...]
    act = c * (1.0 / (1.0 + jnp.exp(-c)))
    o_ref[0] = _dot_f32ish(act, w_ref[0]) + b_ref[0]


def _mod_vectors(cond, ada_w, ada_b):
    depth, d, d6 = ada_w.shape
    g8 = cond.shape[0]
    nblk = d6 // d
    return pl.pallas_call(
        _mod_kernel,
        grid=(depth, nblk),
        in_specs=[
            pl.BlockSpec((g8, d), lambda i, j: (0, 0)),
            pl.BlockSpec((1, d, d), lambda i, j: (i, 0, j)),
            pl.BlockSpec((1, 1, d), lambda i, j: (i, 0, j)),
        ],
        out_specs=pl.BlockSpec((1, g8, d), lambda i, j: (i, 0, j)),
        out_shape=jax.ShapeDtypeStruct((depth, g8, d6), F32),
        compiler_params=_cparams(("arbitrary", "arbitrary")),
        name="mod_vectors",
    )(cond, ada_w, ada_b.reshape(depth, 1, d6))


def _swap16(x):
    lane = lax.broadcasted_iota(jnp.int32, x.shape, 1)
    return jnp.where((lane & 16) == 0, pltpu.roll(x, LANES - 16, 1), pltpu.roll(x, 16, 1))


def _inproj_kernel(x_ref, mod_ref, g_ref, w_ref, cos_ref, sin_ref, *rest, n_pre, n_qk, qk_norm):
    if qk_norm:
        gsum_ref, gexp_ref, hg_ref, *outs = rest
    else:
        outs = rest
    x = x_ref[...]
    h = _norm_mod(x, g_ref[...], mod_ref[0, 0:1, :], mod_ref[0, 1:2, :])
    proj = _dot(h.astype(BF16), w_ref[...])
    qk = proj[:, n_pre:n_pre + n_qk]
    if qk_norm:
        ms = _dot((qk * qk).astype(BF16), gsum_ref[...])
        rinv = lax.rsqrt(ms + EPS)
        r_hi, r_lo = _split3(rinv)
        qk = qk * (_dot(r_hi, gexp_ref[...]) + _dot(r_lo, gexp_ref[...])) * hg_ref[...]
    cos = cos_ref[...]
    sin = sin_ref[...]
    blocks = []
    for j in range(n_qk // LANES):
        blk = qk[:, j * LANES:(j + 1) * LANES]
        blocks.append(blk * cos + _swap16(blk) * sin)
    if n_pre:
        u_ref, q_ref, k_ref, v_ref = outs
        u_ref[...] = proj[:, :n_pre]
    else:
        q_ref, k_ref, v_ref = outs
    n_q = q_ref.shape[1]
    scale = HEAD_DIM ** -0.5
    for j, blk in enumerate(blocks):
        lo = j * LANES
        if lo < n_q:
            q_ref[:, lo:lo + LANES] = (blk * scale).astype(BF16)
        else:
            k_ref[:, lo - n_q:lo - n_q + LANES] = blk.astype(BF16)
    v_ref[...] = proj[:, n_pre + n_qk:].astype(BF16)


def _in_projection(x_all, mod, gain, w_bf16, cos_t, sin_t, seq, n_groups, *, n_pre, n_q, n_kv, norm_consts):
    n, d = x_all.shape
    tm = ROW_TILE
    n_qk = n_q + n_kv
    d_in = w_bf16.shape[1]
    lat_tiles = (n_groups * seq) // tm
    seq_tiles = seq // tm

    def grp(i):
        return (jnp.minimum((i * tm) // seq, n_groups), 0, 0)

    def rope_idx(i):
        return (jnp.where(i < lat_tiles, i % seq_tiles, seq_tiles), 0)

    in_specs = [
        pl.BlockSpec((tm, d), lambda i: (i, 0)),
        pl.BlockSpec((1, 6, d), grp),
        pl.BlockSpec((1, d), lambda i: (0, 0)),
        pl.BlockSpec((d, d_in), lambda i: (0, 0)),
        pl.BlockSpec((tm, LANES), rope_idx),
        pl.BlockSpec((tm, LANES), rope_idx),
    ]
    args = [x_all, mod, gain, w_bf16, cos_t, sin_t]
    if norm_consts is not None:
        gsum, gexp, hg = norm_consts
        in_specs += [
            pl.BlockSpec(gsum.shape, lambda i: (0, 0)),
            pl.BlockSpec(gexp.shape, lambda i: (0, 0)),
            pl.BlockSpec(hg.shape, lambda i: (0, 0)),
        ]
        args += [gsum, gexp, hg]
    out_specs, out_shape = [], []
    if n_pre:
        out_specs.append(pl.BlockSpec((tm, n_pre), lambda i: (i, 0)))
        out_shape.append(jax.ShapeDtypeStruct((n, n_pre), F32))
    for width in (n_q, n_kv, n_kv):
        out_specs.append(pl.BlockSpec((tm, width), lambda i: (i, 0)))
        out_shape.append(jax.ShapeDtypeStruct((n, width), BF16))
    return pl.pallas_call(
        functools.partial(_inproj_kernel, n_pre=n_pre, n_qk=n_qk, qk_norm=norm_consts is not None),
        grid=(n // tm,),
        in_specs=in_specs,
        out_specs=out_specs,
        out_shape=out_shape,
        compiler_params=_cparams(("parallel",)),
        name="in_projection",
    )(*args)


def _head_query(q_ref, head, kv_head):
    blk = q_ref[:, (head // 2) * LANES:(head // 2 + 1) * LANES]
    if head % 2 != kv_head % 2:
        blk = jnp.concatenate([blk[:, HEAD_DIM:], blk[:, :HEAD_DIM]], axis=1)
    lane = lax.broadcasted_iota(jnp.int32, blk.shape, 1)
    keep = (lane >= HEAD_DIM) if kv_head % 2 else (lane < HEAD_DIM)
    return jnp.where(keep, blk, jnp.zeros_like(blk))


def _store_heads(o_ref, outs, kv_of_head):
    for p in range(len(outs) // 2):
        halves = []
        for h in (2 * p, 2 * p + 1):
            half = kv_of_head(h) % 2
            halves.append(outs[h][:, half * HEAD_DIM:(half + 1) * HEAD_DIM])
        o_ref[:, p * LANES:(p + 1) * LANES] = jnp.concatenate(halves, axis=1).astype(o_ref.dtype)


def _global_attn_kernel(bound_ref, q_ref, kc_ref, vc_ref, *rest, n_heads, group, n_lat_chunks):
    if n_lat_chunks:
        kl_ref, vl_ref, o_ref = rest
    else:
        _, o_ref = rest
    tk = ATTN_KV_TILE

    def attend(bounded):
        outs = []
        for j in range(n_heads // group):
            cb = slice((j // 2) * LANES, (j // 2 + 1) * LANES)
            qzs = [_head_query(q_ref, h, j) for h in range(j * group, (j + 1) * group)]
            kc = kc_ref[:, cb]
            vc = vc_ref[:, cb]
            state = []
            for qz in qzs:
                s = _dot_nt(qz, kc)
                m = jnp.zeros((s.shape[0], 1), F32) if bounded else jnp.max(s, axis=-1, keepdims=True)
                p = jnp.exp(s) if bounded else jnp.exp(s - m)
                state.append((m, jnp.sum(p, axis=-1, keepdims=True), _dot(p.astype(BF16), vc)))

            if n_lat_chunks:
                def step(c, carry, qzs=qzs, cb=cb):
                    rows = pl.ds(pl.multiple_of(c * tk, tk), tk)
                    k = kl_ref[rows, cb]
                    v = vl_ref[rows, cb]
                    new = []
                    for qz, (m, l, o) in zip(qzs, carry):
                        s = _dot_nt(qz, k)
                        if bounded:
                            p = jnp.exp(s)
                            l = l + jnp.sum(p, axis=-1, keepdims=True)
                            o = o + _dot(p.astype(BF16), v)
                        else:
                            m_new = jnp.maximum(m, jnp.max(s, axis=-1, keepdims=True))
                            alpha = jnp.exp(m - m_new)
                            p = jnp.exp(s - m_new)
                            l = alpha * l + jnp.sum(p, axis=-1, keepdims=True)
                            o = alpha * o + _dot(p.astype(BF16), v)
                            m = m_new
                        new.append((m, l, o))
                    return tuple(new)

                state = lax.fori_loop(0, n_lat_chunks, step, tuple(state))
            outs.extend(o / l for (_, l, o) in state)
        _store_heads(o_ref, outs, lambda h: h // group)

    small = bound_ref[0] <= SAFE_LOGIT_BOUND
    pl.when(small)(functools.partial(attend, True))
    pl.when(jnp.logical_not(small))(functools.partial(attend, False))


def _global_attention(logit_bound, q_all, k_all, v_all, batch, seq, ctx_len, *, into=None):
    latent = into is None
    n_q = q_all.shape[1]
    n_kv = k_all.shape[1]
    n_heads = n_q // HEAD_DIM
    group = n_heads // (n_kv // HEAD_DIM)
    ctx_blk0 = (batch * seq) // ctx_len
    out_rows = q_all.shape[0]
    if latent:
        tq = ATTN_Q_TILE
        q_tiles = seq // tq
        q_idx = lambda b, i: (b * q_tiles + i, 0)
    else:
        tq = ctx_len
        q_tiles = 1
        q_idx = lambda b, i: (ctx_blk0 + b, 0)
    in_specs = [
        pl.BlockSpec(memory_space=pltpu.SMEM),
        pl.BlockSpec((tq, n_q), q_idx),
        pl.BlockSpec((ctx_len, n_kv), lambda b, i: (ctx_blk0 + b, 0)),
        pl.BlockSpec((ctx_len, n_kv), lambda b, i: (ctx_blk0 + b, 0)),
    ]
    args = [logit_bound, q_all, k_all, v_all]
    if latent:
        in_specs += [pl.BlockSpec((seq, n_kv), lambda b, i: (b, 0)),
                     pl.BlockSpec((seq, n_kv), lambda b, i: (b, 0))]
        args += [k_all, v_all]
        aliases = {}
    else:
        in_specs.append(pl.BlockSpec(memory_space=pl.ANY))
        args.append(into)
        aliases = {len(args) - 1: 0}
    return pl.pallas_call(
        functools.partial(_global_attn_kernel, n_heads=n_heads, group=group,
                          n_lat_chunks=(seq // ATTN_KV_TILE) if latent else 0),
        grid=(batch, q_tiles),
        in_specs=in_specs,
        out_specs=pl.BlockSpec((tq, n_q), q_idx),
        out_shape=jax.ShapeDtypeStruct((out_rows, n_q), BF16),
        input_output_aliases=aliases,
        compiler_params=_cparams(("parallel", "arbitrary")),
        name="global_attention" if latent else "context_attention",
    )(*args)


def _window_attn_kernel(sink_ref, q_ref, kc_ref, vc_ref, kl_ref, vl_ref, o_ref, *, n_heads, group, seq):
    tq = WIN_Q_TILE
    band = WIN_BAND
    start = pl.program_id(1) * tq
    k0 = jnp.clip(start - WINDOW, 0, seq - band)
    k0 = pl.multiple_of(k0, WINDOW)
    rows = pl.ds(k0, band)
    qpos = start + lax.broadcasted_iota(jnp.int32, (tq, band), 0)
    kpos = k0 + lax.broadcasted_iota(jnp.int32, (tq, band), 1)
    valid = jnp.abs(qpos - kpos) <= WINDOW
    outs = []
    for h in range(n_heads):
        j = h // group
        cb = slice((j // 2) * LANES, (j // 2 + 1) * LANES)
        qz = _head_query(q_ref, h, j)
        s_c = _dot_nt(qz, kc_ref[:, cb])
        s_b = jnp.where(valid, _dot_nt(qz, kl_ref[rows, cb]), MASK_VALUE)
        sink = sink_ref[h]
        m = jnp.maximum(jnp.maximum(jnp.max(s_c, axis=-1, keepdims=True),
                                    jnp.max(s_b, axis=-1, keepdims=True)), sink)
        e_c = jnp.exp(s_c - m)
        e_b = jnp.exp(s_b - m)
        denom = (jnp.sum(e_c, axis=-1, keepdims=True) + jnp.sum(e_b, axis=-1, keepdims=True)
                 + jnp.exp(sink - m))
        o = _dot(e_c.astype(BF16), vc_ref[:, cb]) + _dot(e_b.astype(BF16), vl_ref[rows, cb])
        outs.append(o / denom)
    _store_heads(o_ref, outs, lambda h: h // group)


def _window_attention(q_all, k_all, v_all, sink, batch, seq, ctx_len):
    n_q = q_all.shape[1]
    n_kv = k_all.shape[1]
    n_heads = n_q // HEAD_DIM
    group = n_heads // (n_kv // HEAD_DIM)
    tq = WIN_Q_TILE
    q_tiles = seq // tq
    ctx_blk0 = (batch * seq) // ctx_len
    return pl.pallas_call(
        functools.partial(_window_attn_kernel, n_heads=n_heads, group=group, seq=seq),
        grid=(batch, q_tiles),
        in_specs=[
            pl.BlockSpec(memory_space=pltpu.SMEM),
            pl.BlockSpec((tq, n_q), lambda b, i: (b * q_tiles + i, 0)),
            pl.BlockSpec((ctx_len, n_kv), lambda b, i: (ctx_blk0 + b, 0)),
            pl.BlockSpec((ctx_len, n_kv), lambda b, i: (ctx_blk0 + b, 0)),
            pl.BlockSpec((seq, n_kv), lambda b, i: (b, 0)),
            pl.BlockSpec((seq, n_kv), lambda b, i: (b, 0)),
        ],
        out_specs=pl.BlockSpec((tq, n_q), lambda b, i: (b * q_tiles + i, 0)),
        out_shape=jax.ShapeDtypeStruct((batch * seq, n_q), BF16),
        compiler_params=_cparams(("parallel", "arbitrary")),
        name="window_attention",
    )(sink, q_all, k_all, v_all, k_all, v_all)


def _pool_kernel(u_ref, w_ref, sc_ref, *rest, length):
    o_ref, pad_ref = rest[-2:]
    halo = POOL_HALO
    ch = POOL_CHUNK
    zeros = jnp.zeros((halo, POOL_DIM), F32)
    pad_ref[0:halo, :] = zeros
    pad_ref[halo + length:halo + length + halo, :] = zeros
    pad_ref[halo:halo + length, :] = u_ref[...]
    lane = lax.broadcasted_iota(jnp.int32, (ch, POOL_DIM), 1)
    grp = lane // POOL_GROUP_DIM
    half_w = jnp.left_shift(1, grp)
    row0 = lax.broadcasted_iota(jnp.int32, (ch, POOL_DIM), 0)
    win_rows = ch + 2 * halo

    def chunk(c, carry):
        r0 = pl.multiple_of(c * ch, ch)
        win = pad_ref[pl.ds(r0, win_rows), :]
        acc = jnp.zeros((ch, POOL_DIM), F32)
        for dlt in range(-halo, halo):
            shifted = pltpu.roll(win, win_rows - (halo + dlt), 0)[:ch] if halo + dlt else win[:ch]
            inside = (half_w >= -dlt) if dlt < 0 else (half_w > dlt)
            acc = acc + jnp.where(inside, shifted, 0.0)
        t = row0 + r0
        hi = jnp.minimum(t + half_w, length)
        lo = jnp.maximum(t - half_w, 0)
        centre = win[halo:halo + ch]
        pooled = acc / (hi - lo).astype(F32) - centre
        mixed = _dot(pooled.astype(BF16), w_ref[...]) * sc_ref[...]
        o_ref[pl.ds(r0, ch), :] = mixed.astype(o_ref.dtype)
        return carry

    lax.fori_loop(0, length // ch, chunk, 0)


def _pool_mixer(u_all, w_blockdiag, scale, row0, n_seq, length, into=None):
    blk0 = row0 // length
    in_specs = [
        pl.BlockSpec((length, POOL_DIM), lambda s: (blk0 + s, 0)),
        pl.BlockSpec((POOL_DIM, POOL_DIM), lambda s: (0, 0)),
        pl.BlockSpec((1, POOL_DIM), lambda s: (0, 0)),
    ]
    args = [u_all, w_blockdiag, scale]
    aliases = {}
    if into is not None:
        in_specs.append(pl.BlockSpec(memory_space=pl.ANY))
        args.append(into)
        aliases = {3: 0}
    return pl.pallas_call(
        functools.partial(_pool_kernel, length=length),
        grid=(n_seq,),
        in_specs=in_specs,
        out_specs=pl.BlockSpec((length, POOL_DIM), lambda s: (blk0 + s, 0)),
        out_shape=jax.ShapeDtypeStruct((u_all.shape[0], POOL_DIM), BF16),
        input_output_aliases=aliases,
        scratch_shapes=[pltpu.VMEM((length + 2 * POOL_HALO, POOL_DIM), F32)],
        compiler_params=_cparams(("parallel",)),
        name="pool_mixer",
    )(*args)


def _outproj_kernel(x_ref, mod_ref, *rest, n_pool):
    if n_pool:
        m_ref, a_ref, w_ref, o_ref = rest
        y = _dot(m_ref[...], w_ref[:n_pool, :]) + _dot(a_ref[...], w_ref[n_pool:, :])
    else:
        a_ref, w_ref, o_ref = rest
        y = _dot(a_ref[...], w_ref[...])
    o_ref[...] = x_ref[...] + mod_ref[0, 2:3, :] * y


def _out_projection(x_all, mod, mixed, attn, w_bf16, seq, n_groups, n_rows):
    d = x_all.shape[1]
    tm = ROW_TILE
    n_pool = 0 if mixed is None else mixed.shape[1]
    n_att = attn.shape[1]

    def grp(i):
        return (jnp.minimum((i * tm) // seq, n_groups), 0, 0)

    in_specs = [pl.BlockSpec((tm, d), lambda i: (i, 0)), pl.BlockSpec((1, 6, d), grp)]
    args = [x_all, mod]
    if n_pool:
        in_specs.append(pl.BlockSpec((tm, n_pool), lambda i: (i, 0)))
        args.append(mixed)
    in_specs += [pl.BlockSpec((tm, n_att), lambda i: (i, 0)),
                 pl.BlockSpec((n_pool + n_att, d), lambda i: (0, 0))]
    args += [attn, w_bf16]
    return pl.pallas_call(
        functools.partial(_outproj_kernel, n_pool=n_pool),
        grid=(n_rows // tm,),
        in_specs=in_specs,
        out_specs=pl.BlockSpec((tm, d), lambda i: (i, 0)),
        out_shape=jax.ShapeDtypeStruct((n_rows, d), F32),
        compiler_params=_cparams(("parallel",)),
        name="out_projection",
    )(*args)


def _oddeven_merge(lo, hi, r):
    step = r * 2
    if step < hi - lo:
        yield from _oddeven_merge(lo, hi, step)
        yield from _oddeven_merge(lo + r, hi, step)
        yield from [(i, i + r) for i in range(lo + r, hi - r, step)]
    else:
        yield (lo, lo + r)


def _oddeven_sort(lo, hi):
    if hi - lo >= 1:
        mid = lo + (hi - lo) // 2
        yield from _oddeven_sort(lo, mid)
        yield from _oddeven_sort(mid + 1, hi)
        yield from _oddeven_merge(lo, hi, 1)


_SORT16 = tuple(_oddeven_sort(0, 15))
_SORT8 = tuple(_oddeven_sort(0, 7))
_TOP_PAIRS = tuple((r, c) for r in range(PEER_TOPK) for c in range(PEER_TOPK) if (r + 1) * (c + 1) <= PEER_TOPK)


def _cmpx(xs, i, j):
    a, b = xs[i], xs[j]
    xs[i] = jnp.maximum(a, b)
    xs[j] = jnp.minimum(a, b)


def _sort_desc(xs, net):
    xs = list(xs)
    for i, j in net:
        _cmpx(xs, i, j)
    return xs


def _bitonic_merge_desc(xs):
    xs = list(xs)
    n = len(xs)
    d = n // 2
    while d:
        for i in range(n):
            if not i & d:
                _cmpx(xs, i, i + d)
        d //= 2
    return xs


def _merge_top(a, b):
    n = len(a)
    return _bitonic_merge_desc([jnp.maximum(a[i], b[n - 1 - i]) for i in range(n)])


def _top16_sorted(rows):
    w = _sort_desc(rows, _SORT16)
    for shift in (4, 2, 1):
        w = _merge_top(w, [pltpu.roll(x, shift, 0) for x in w])
    return w


def _pair_threshold(pair):
    low = jnp.full_like(pair[(0, 0)], LOWEST)
    row0 = [pair[(0, c)] for c in range(16)]
    col0 = [pair[(r, 0)] for r in range(1, 16)] + [low]
    t01 = _merge_top(row0, col0)
    g2 = _bitonic_merge_desc([pair[(1, c)] for c in range(1, 8)] + [low] * 3
                             + [pair[(r, 1)] for r in range(7, 1, -1)])
    rest = [pair[k] for k in ((2, 2), (2, 3), (2, 4), (3, 2), (4, 2), (3, 3))]
    g3 = _sort_desc(rest + [low, low], _SORT8) + [low] * 8
    t23 = _merge_top(g2, g3)
    tau = None
    for i in range(16):
        m = jnp.maximum(t01[i], t23[15 - i])
        tau = m if tau is None else jnp.minimum(tau, m)
    return tau


def _route_tile(s1, s2):
    assert PEER_TOPK == 16 and PEER_NKEYS == 128
    n_col = s1.shape[1] // LANES
    assert 8 % n_col == 0
    span = 8 // n_col
    cols = []
    for j in range(n_col):
        c1 = s1[:, j * LANES:(j + 1) * LANES]
        c2 = s2[:, j * LANES:(j + 1) * LANES]
        rows1 = [c1[8 * g:8 * g + 8] for g in range(16)]
        rows2 = [c2[8 * g:8 * g + 8] for g in range(16)]
        cols.append((rows1, rows2, _top16_sorted(rows1), _top16_sorted(rows2)))

    sub = lax.broadcasted_iota(jnp.int32, (8, LANES), 0)

    def pack(vals):
        out = vals[-1]
        for j in range(n_col - 2, -1, -1):
            out = jnp.where(sub < (j + 1) * span, vals[j], out)
        return out

    def unpack(x, j):
        return jnp.broadcast_to(x[j * span:j * span + 1, :], (8, LANES))

    v1 = [pack([c[2][r] for c in cols]) for r in range(16)]
    v2 = [pack([c[3][r] for c in cols]) for r in range(16)]
    pair = {(r, c): v1[r] + v2[c] for (r, c) in _TOP_PAIRS}
    tau = _pair_threshold(pair)
    e1 = [jnp.exp(v - v1[0]) for v in v1]
    e2 = [jnp.exp(v - v2[0]) for v in v2]
    z = None
    for (r, c) in _TOP_PAIRS:
        term = jnp.where(pair[(r, c)] >= tau, e1[r] * e2[c], 0.0)
        z = term if z is None else z + term
    inv_z = (1.0 / math.sqrt(2.0)) / z
    n_top = jnp.zeros_like(tau)
    for c in range(16):
        n_top = jnp.where(pair[(0, c)] >= tau, c + 1.0, n_top)

    out = [[], [], [], []]
    for j, (rows1, rows2, t1, t2) in enumerate(cols):
        tau_j, inv_z_j, n_top_j = unpack(tau, j), unpack(inv_z, j), unpack(n_top, j)
        cnt, p1n, rank2, p2 = [], [], [], []
        for g in range(16):
            a = rows1[g]
            n = jnp.zeros_like(a)
            for c in range(8):
                n = jnp.where(a + t2[c] >= tau_j, c + 1.0, n)
            cnt.append(jnp.where(a >= t1[0], n_top_j, n))
            p1n.append(jnp.exp(a - t1[0]) * inv_z_j)
            b = rows2[g]
            k = jnp.zeros_like(b)
            for r in range(16):
                k = jnp.where(t2[r] > b, r + 1.0, k)
            rank2.append(k)
            p2.append(jnp.exp(b - t2[0]))
        for dst, parts in zip(out, (cnt, p1n, rank2, p2)):
            dst.append(jnp.concatenate(parts, axis=0))
    return tuple(jnp.concatenate(parts, axis=1) for parts in out)


def _peer_kernel(x_ref, mod_ref, g_ref, wq_ref, k1_ref, k2_ref, ut_ref, vt_ref, *rest, final_norm, n_blocks):
    if final_norm:
        fg_ref, o_ref, *scratch = rest
    else:
        o_ref, *scratch = rest
    hb_ref, cnt_ref, p1_ref, rank_ref, p2_ref, gw0_ref, gw1_ref, acc_ref = scratch
    s = pl.program_id(1)
    eb, t = gw0_ref.shape
    blocks = eb // PEER_NKEYS
    half = PEER_DKEY // 2

    @pl.when(s == 0)
    def _route():
        x = x_ref[...]
        h2 = _norm_mod(x, g_ref[...], mod_ref[0, 3:4, :], mod_ref[0, 4:5, :])
        hb_ref[...] = h2.astype(BF16)
        acc_ref[...] = jnp.zeros_like(acc_ref)

        def head(h, carry):
            qh = _dot(hb_ref[...], wq_ref[h])
            s1 = _dot_nt(k1_ref[...], qh[:, :half].astype(BF16))
            s2 = _dot_nt(k2_ref[...], qh[:, half:].astype(BF16))
            cnt, p1n, rank2, p2 = _route_tile(s1, s2)
            cnt_ref[h] = cnt
            p1_ref[h] = p1n
            rank_ref[h] = rank2.astype(BF16)
            p2_ref[h] = p2.astype(BF16)
            return carry

        lax.fori_loop(0, PEER_HEADS, head, 0)

    def stage(parity, build, consume):
        gw_w, gw_r = (gw0_ref, gw1_ref) if parity == 0 else (gw1_ref, gw0_ref)
        reps = PEER_NKEYS // 16
        span = 2 * PEER_NKEYS
        n_pairs = blocks // 2

        def build_pair(pair):
            y_pair = _dot(hb_ref[...], ut_ref[:, pair * span:(pair + 1) * span]).T
            for i in (2 * pair, 2 * pair + 1):
                a = s * blocks + i
                w = None
                for h in range(PEER_HEADS):
                    cnt16 = jnp.broadcast_to(cnt_ref[h, pl.ds(a, 1), :], (16, t)).astype(BF16)
                    p16 = jnp.broadcast_to(p1_ref[h, pl.ds(a, 1), :], (16, t)).astype(BF16)
                    sel = rank_ref[h] < jnp.concatenate([cnt16] * reps, axis=0)
                    term = (jnp.where(sel, p2_ref[h], jnp.zeros((), BF16))
                            * jnp.concatenate([p16] * reps, axis=0))
                    w = term if w is None else w + term
                y = y_pair[(i % 2) * PEER_NKEYS:(i % 2 + 1) * PEER_NKEYS, :]
                act = y * (1.0 + lax.erf(y))
                gw_w[i * PEER_NKEYS:(i + 1) * PEER_NKEYS, :] = act.astype(BF16) * w

        if build:
            for k in range(n_pairs):
                build_pair(k)
        if consume:
            acc_ref[...] += _dot(vt_ref[...], gw_r[...])

    pl.when(s == 0)(functools.partial(stage, 0, True, False))
    steady = (s >= 1) & (s < n_blocks)
    pl.when(steady & (s % 2 == 0))(functools.partial(stage, 0, True, True))
    pl.when(steady & (s % 2 == 1))(functools.partial(stage, 1, True, True))
    pl.when(s == n_blocks)(functools.partial(stage, n_blocks % 2, False, True))

    @pl.when(s == n_blocks)
    def _finish():
        y = x_ref[...] + mod_ref[0, 5:6, :] * acc_ref[...].T
        if final_norm:
            y = y * lax.rsqrt(jnp.mean(y * y, axis=-1, keepdims=True) + EPS) * fg_ref[...]
        o_ref[...] = y


def _peer(x_all, mod, gain, wq_heads, keys1, keys2, ut_bf16, vt_bf16, seq, n_groups, n_rows, final_gain):
    d = x_all.shape[1]
    t = PEER_TOKENS
    eb = PEER_EXPERT_BLOCK
    n_blocks = ut_bf16.shape[1] // eb

    def grp(i, e):
        return (jnp.minimum((i * t) // seq, n_groups), 0, 0)

    in_specs = [
        pl.BlockSpec((t, d), lambda i, e: (i, 0)),
        pl.BlockSpec((1, 6, d), grp),
        pl.BlockSpec((1, d), lambda i, e: (0, 0)),
        pl.BlockSpec(wq_heads.shape, lambda i, e: (0, 0, 0)),
        pl.BlockSpec(keys1.shape, lambda i, e: (0, 0)),
        pl.BlockSpec(keys2.shape, lambda i, e: (0, 0)),
        pl.BlockSpec((d, eb), lambda i, e: (0, jnp.minimum(e, n_blocks - 1))),
        pl.BlockSpec((d, eb), lambda i, e: (0, jnp.maximum(e - 1, 0))),
    ]
    args = [x_all, mod, gain, wq_heads, keys1, keys2, ut_bf16, vt_bf16]
    if final_gain is not None:
        in_specs.append(pl.BlockSpec((1, d), lambda i, e: (0, 0)))
        args.append(final_gain)
    per_head = (PEER_HEADS, PEER_NKEYS, t)
    return pl.pallas_call(
        functools.partial(_peer_kernel, final_norm=final_gain is not None, n_blocks=n_blocks),
        grid=(n_rows // t, n_blocks + 1),
        in_specs=in_specs,
        out_specs=pl.BlockSpec((t, d), lambda i, e: (i, 0)),
        out_shape=jax.ShapeDtypeStruct((n_rows, d), F32),
        scratch_shapes=[
            pltpu.VMEM((t, d), BF16),
            pltpu.VMEM(per_head, F32),
            pltpu.VMEM(per_head, F32),
            pltpu.VMEM(per_head, BF16),
            pltpu.VMEM(per_head, BF16),
            pltpu.VMEM((eb, t), BF16),
            pltpu.VMEM((eb, t), BF16),
            pltpu.VMEM((d, t), F32),
        ],
        compiler_params=_cparams(("parallel", "arbitrary")),
        name="peer",
    )(*args)


def _rope_tables(seq, pad_rows):
    half = HEAD_DIM // 2
    t = jnp.arange(seq, dtype=jnp.int32)
    row = (t // GRID_W).astype(F32)
    col = (t % GRID_W).astype(F32)
    inv = 1.0 / (ROPE_THETA ** (jnp.arange(0, half, 2, dtype=F32) / half))
    ang_r = row[:, None] * inv[None, :]
    ang_c = col[:, None] * inv[None, :]
    cos_h = jnp.concatenate([jnp.cos(ang_r)] * 2 + [jnp.cos(ang_c)] * 2, axis=1)
    sin_h = jnp.concatenate([-jnp.sin(ang_r), jnp.sin(ang_r), -jnp.sin(ang_c), jnp.sin(ang_c)], axis=1)
    cos_t = jnp.concatenate([jnp.tile(cos_h, (1, LANES // HEAD_DIM)), jnp.ones((pad_rows, LANES), F32)], axis=0)
    sin_t = jnp.concatenate([jnp.tile(sin_h, (1, LANES // HEAD_DIM)), jnp.zeros((pad_rows, LANES), F32)], axis=0)
    return cos_t, sin_t


def _head_norm_consts(q_gain, k_gain, n_q_heads, n_kv_heads):
    n_heads = n_q_heads + n_kv_heads
    head_of_lane = np.arange(n_heads * HEAD_DIM) // HEAD_DIM
    onehot = (head_of_lane[:, None] == np.arange(LANES)[None, :]).astype(np.float32)
    gsum = jnp.asarray(onehot / HEAD_DIM, BF16)
    gexp = jnp.asarray(onehot.T, BF16)
    hg = jnp.concatenate([jnp.tile(q_gain, n_q_heads), jnp.tile(k_gain, n_kv_heads)])[None, :]
    return gsum, gexp, hg


def kernel(x, c, ctx, c_ctx, ada_w, ada_b, norm1_g, norm2_g, final_g, e_w_in, e_q_norm_g, e_k_norm_g,
           e_pool_w, e_pool_scale, e_w_out, o_w_in, o_sink, o_w_out, p_w_q, p_keys1, p_keys2, p_u, p_v):
    batch, seq, d = x.shape
    ctx_len = ctx.shape[1]
    depth = ada_w.shape[0]
    n_lat = batch * seq
    n_all = n_lat + batch * ctx_len

    x_all = jnp.concatenate([x.reshape(n_lat, d), ctx.reshape(batch * ctx_len, d)], axis=0)
    cond = jnp.concatenate([c, c_ctx[None, :], jnp.zeros((8 - (batch + 1) % 8, d), F32)], axis=0)
    mods = _mod_vectors(cond, ada_w, ada_b)
    mods = mods.reshape(depth, cond.shape[0], 6, d)
    cos_t, sin_t = _rope_tables(seq, ROW_TILE)

    for i in range(depth):
        last = i == depth - 1
        mod = mods[i]
        j = i // 2
        g1 = norm1_g[i][None, :]
        g2 = norm2_g[i][None, :]
        if i % 2 == 0:
            consts = _head_norm_consts(e_q_norm_g[j], e_k_norm_g[j], B_Q_HEADS, B_KV_HEADS)
            u, q, k, v = _in_projection(
                x_all, mod, g1, e_w_in[j].astype(BF16), cos_t, sin_t, seq, batch,
                n_pre=POOL_DIM, n_q=B_Q_HEADS * HEAD_DIM, n_kv=B_KV_HEADS * HEAD_DIM, norm_consts=consts)
            w_pool = jax.scipy.linalg.block_diag(*[e_pool_w[j, g] for g in range(POOL_GROUPS)]).astype(BF16)
            p_scale = e_pool_scale[j][None, :]
            logit_bound = (NORM_BOUND_MARGIN * math.sqrt(HEAD_DIM) * jnp.max(jnp.abs(e_q_norm_g[j]))
                           * jnp.max(jnp.abs(e_k_norm_g[j]))).reshape(1)
            attn = _global_attention(logit_bound, q, k, v, batch, seq, ctx_len)
            mixed = _pool_mixer(u, w_pool, p_scale, 0, batch, seq)
            if not last:
                attn = _global_attention(logit_bound, q, k, v, batch, seq, ctx_len, into=attn)
                mixed = _pool_mixer(u, w_pool, p_scale, n_lat, batch, ctx_len, into=mixed)
            w_out = e_w_out[j].astype(BF16)
        else:
            _q = C_Q_HEADS * HEAD_DIM
            q, k, v = _in_projection(
                x_all, mod, g1, o_w_in[j].astype(BF16), cos_t, sin_t, seq, batch,
                n_pre=0, n_q=_q, n_kv=C_KV_HEADS * HEAD_DIM, norm_consts=None)
            attn = _window_attention(q, k, v, o_sink[j], batch, seq, ctx_len)
            if not last:
                raise NotImplementedError("context stream after an odd layer")
            mixed = None
            w_out = o_w_out[j].astype(BF16)
        n_rows = n_lat if last else n_all
        x_mid = _out_projection(x_all, mod, mixed, attn, w_out, seq, batch, n_rows)
        wq_heads = p_w_q[i].astype(BF16).reshape(d, PEER_HEADS, PEER_DKEY).transpose(1, 0, 2)
        x_all = _peer(x_mid, mod, g2, wq_heads, p_keys1[i].astype(BF16), p_keys2[i].astype(BF16),
                      (p_u[i] * (1.0 / math.sqrt(2.0))).astype(BF16).T, p_v[i].astype(BF16).T, seq, batch, n_rows,
                      final_g[None, :] if last else None)
    return x_all[:n_lat].reshape(batch, seq, d)
```

```python
import functools
import math

import jax
import jax.numpy as jnp
import numpy as np
from jax import lax
from jax.experimental import pallas as pl
from jax.experimental.pallas import tpu as pltpu

F32 = jnp.float32
BF16 = jnp.bfloat16

GRID_W = 64
HEAD_DIM = 64
ROPE_THETA = 10000.0
EPS = 1e-6
MASK_VALUE = -1e30
POOL_GROUPS = 4
POOL_GROUP_DIM = 64
POOL_DIM = POOL_GROUPS * POOL_GROUP_DIM
POOL_WINDOWS = (2, 4, 8, 16)
POOL_HALO = 8
B_Q_HEADS = 12
B_KV_HEADS = 4
C_Q_HEADS = 16
C_KV_HEADS = 4
WINDOW = 128
PEER_HEADS = 8
PEER_NKEYS = 128
PEER_DKEY = 256
PEER_TOPK = 16
LOWEST = -3.0e38
SAFE_LOGIT_BOUND = 40.0
NORM_BOUND_MARGIN = 1.05

LANES = 128
VMEM_LIMIT_BYTES = 56 * 1024 * 1024
ROW_TILE = 512
ATTN_Q_TILE = 512
ATTN_KV_TILE = 1024
WIN_Q_TILE = 256
WIN_BAND = WIN_Q_TILE + 2 * WINDOW
PEER_TOKENS = 512
PEER_EXPERT_BLOCK = 2048
POOL_CHUNK = 256


def _cparams(semantics):
    return pltpu.CompilerParams(dimension_semantics=semantics, vmem_limit_bytes=VMEM_LIMIT_BYTES)


def _dot(a, b):
    return jnp.dot(a, b, preferred_element_type=F32)


def _dot_nt(a, b):
    return lax.dot_general(a, b, (((1,), (1,)), ((), ())), preferred_element_type=F32)


def _split3(a):
    hi = a.astype(BF16)
    lo = (a - hi.astype(F32)).astype(BF16)
    return hi, lo


def _dot_f32ish(a, b):
    ah, al = _split3(a)
    bh, bl = _split3(b)
    return _dot(ah, bh) + (_dot(ah, bl) + _dot(al, bh))


def _norm_mod(x, gain, shift, scale):
    y = x * lax.rsqrt(jnp.mean(x * x, axis=-1, keepdims=True) + EPS)
    return (y * gain) * (1.0 + scale) + shift


def _mod_kernel(c_ref, w_ref, b_ref, o_ref):
    c = c_ref[...]
    act = c * (1.0 / (1.0 + jnp.exp(-c)))
    o_ref[0] = _dot_f32ish(act, w_ref[0]) + b_ref[0]


def _mod_vectors(cond, ada_w, ada_b):
    depth, d, d6 = ada_w.shape
    g8 = cond.shape[0]
    nblk = d6 // d
    return pl.pallas_call(
        _mod_kernel,
        grid=(depth, nblk),
        in_specs=[
            pl.BlockSpec((g8, d), lambda i, j: (0, 0)),
            pl.BlockSpec((1, d, d), lambda i, j: (i, 0, j)),
            pl.BlockSpec((1, 1, d), lambda i, j: (i, 0, j)),
        ],
        out_specs=pl.BlockSpec((1, g8, d), lambda i, j: (i, 0, j)),
        out_shape=jax.ShapeDtypeStruct((depth, g8, d6), F32),
        compiler_params=_cparams(("arbitrary", "arbitrary")),
        name="mod_vectors",
    )(cond, ada_w, ada_b.reshape(depth, 1, d6))


def _swap16(x):
    lane = lax.broadcasted_iota(jnp.int32, x.shape, 1)
    return jnp.where((lane & 16) == 0, pltpu.roll(x, LANES - 16, 1), pltpu.roll(x, 16, 1))


def _inproj_kernel(x_ref, mod_ref, g_ref, w_ref, cos_ref, sin_ref, *rest, n_pre, n_qk, qk_norm):
    if qk_norm:
        gsum_ref, gexp_ref, hg_ref, *outs = rest
    else:
        outs = rest
    x = x_ref[...]
    h = _norm_mod(x, g_ref[...], mod_ref[0, 0:1, :], mod_ref[0, 1:2, :])
    proj = _dot(h.astype(BF16), w_ref[...])
    qk = proj[:, n_pre:n_pre + n_qk]
    if qk_norm:
        ms = _dot((qk * qk).astype(BF16), gsum_ref[...])
        rinv = lax.rsqrt(ms + EPS)
        r_hi, r_lo = _split3(rinv)
        qk = qk * (_dot(r_hi, gexp_ref[...]) + _dot(r_lo, gexp_ref[...])) * hg_ref[...]
    cos = cos_ref[...]
    sin = sin_ref[...]
    blocks = []
    for j in range(n_qk // LANES):
        blk = qk[:, j * LANES:(j + 1) * LANES]
        blocks.append(blk * cos + _swap16(blk) * sin)
    if n_pre:
        u_ref, q_ref, k_ref, v_ref = outs
        u_ref[...] = proj[:, :n_pre]
    else:
        q_ref, k_ref, v_ref = outs
    n_q = q_ref.shape[1]
    scale = HEAD_DIM ** -0.5
    for j, blk in enumerate(blocks):
        lo = j * LANES
        if lo < n_q:
            q_ref[:, lo:lo + LANES] = (blk * scale).astype(BF16)
        else:
            k_ref[:, lo - n_q:lo - n_q + LANES] = blk.astype(BF16)
    v_ref[...] = proj[:, n_pre + n_qk:].astype(BF16)


def _in_projection(x_all, mod, gain, w_bf16, cos_t, sin_t, seq, n_groups, *, n_pre, n_q, n_kv, norm_consts):
    n, d = x_all.shape
    tm = ROW_TILE
    n_qk = n_q + n_kv
    d_in = w_bf16.shape[1]
    lat_tiles = (n_groups * seq) // tm
    seq_tiles = seq // tm

    def grp(i):
        return (jnp.minimum((i * tm) // seq, n_groups), 0, 0)

    def rope_idx(i):
        return (jnp.where(i < lat_tiles, i % seq_tiles, seq_tiles), 0)

    in_specs = [
        pl.BlockSpec((tm, d), lambda i: (i, 0)),
        pl.BlockSpec((1, 6, d), grp),
        pl.BlockSpec((1, d), lambda i: (0, 0)),
        pl.BlockSpec((d, d_in), lambda i: (0, 0)),
        pl.BlockSpec((tm, LANES), rope_idx),
        pl.BlockSpec((tm, LANES), rope_idx),
    ]
    args = [x_all, mod, gain, w_bf16, cos_t, sin_t]
    if norm_consts is not None:
        gsum, gexp, hg = norm_consts
        in_specs += [
            pl.BlockSpec(gsum.shape, lambda i: (0, 0)),
            pl.BlockSpec(gexp.shape, lambda i: (0, 0)),
            pl.BlockSpec(hg.shape, lambda i: (0, 0)),
        ]
        args += [gsum, gexp, hg]
    out_specs, out_shape = [], []
    if n_pre:
        out_specs.append(pl.BlockSpec((tm, n_pre), lambda i: (i, 0)))
        out_shape.append(jax.ShapeDtypeStruct((n, n_pre), F32))
    for width in (n_q, n_kv, n_kv):
        out_specs.append(pl.BlockSpec((tm, width), lambda i: (i, 0)))
        out_shape.append(jax.ShapeDtypeStruct((n, width), BF16))
    return pl.pallas_call(
        functools.partial(_inproj_kernel, n_pre=n_pre, n_qk=n_qk, qk_norm=norm_consts is not None),
        grid=(n // tm,),
        in_specs=in_specs,
        out_specs=out_specs,
        out_shape=out_shape,
        compiler_params=_cparams(("parallel",)),
        name="in_projection",
    )(*args)


def _head_query(q_ref, head, kv_head):
    blk = q_ref[:, (head // 2) * LANES:(head // 2 + 1) * LANES]
    if head % 2 != kv_head % 2:
        blk = jnp.concatenate([blk[:, HEAD_DIM:], blk[:, :HEAD_DIM]], axis=1)
    lane = lax.broadcasted_iota(jnp.int32, blk.shape, 1)
    keep = (lane >= HEAD_DIM) if kv_head % 2 else (lane < HEAD_DIM)
    return jnp.where(keep, blk, jnp.zeros_like(blk))


def _store_heads(o_ref, outs, kv_of_head, first_pair=0):
    for p in range(first_pair, len(outs) // 2):
        halves = []
        for h in (2 * p, 2 * p + 1):
            half = kv_of_head(h) % 2
            halves.append(outs[h][:, half * HEAD_DIM:(half + 1) * HEAD_DIM])
        o_ref[:, p * LANES:(p + 1) * LANES] = jnp.concatenate(halves, axis=1).astype(o_ref.dtype)


def _global_attn_kernel(bound_ref, q_ref, kc_ref, vc_ref, *rest, n_heads, group, n_lat_chunks):
    if n_lat_chunks:
        kl_ref, vl_ref, o_ref = rest
    else:
        (o_ref,) = rest
    tk = ATTN_KV_TILE

    def attend(bounded):
        outs = []
        for j in range(n_heads // group):
            cb = slice((j // 2) * LANES, (j // 2 + 1) * LANES)
            qzs = [_head_query(q_ref, h, j) for h in range(j * group, (j + 1) * group)]
            kc = kc_ref[:, cb]
            vc = vc_ref[:, cb]
            state = []
            for qz in qzs:
                s = _dot_nt(qz, kc)
                m = jnp.zeros((s.shape[0], 1), F32) if bounded else jnp.max(s, axis=-1, keepdims=True)
                p = jnp.exp(s) if bounded else jnp.exp(s - m)
                state.append((m, jnp.sum(p, axis=-1, keepdims=True), _dot(p.astype(BF16), vc)))

            if n_lat_chunks:
                def step(c, carry, qzs=qzs, cb=cb):
                    rows = pl.ds(pl.multiple_of(c * tk, tk), tk)
                    k = kl_ref[rows, cb]
                    v = vl_ref[rows, cb]
                    new = []
                    for qz, (m, l, o) in zip(qzs, carry):
                        s = _dot_nt(qz, k)
                        if bounded:
                            p = jnp.exp(s)
                            l = l + jnp.sum(p, axis=-1, keepdims=True)
                            o = o + _dot(p.astype(BF16), v)
                        else:
                            m_new = jnp.maximum(m, jnp.max(s, axis=-1, keepdims=True))
                            alpha = jnp.exp(m - m_new)
                            p = jnp.exp(s - m_new)
                            l = alpha * l + jnp.sum(p, axis=-1, keepdims=True)
                            o = alpha * o + _dot(p.astype(BF16), v)
                            m = m_new
                        new.append((m, l, o))
                    return tuple(new)

                state = lax.fori_loop(0, n_lat_chunks, step, tuple(state))
            done_pairs = len(outs) // 2
            outs.extend(o / l for (_, l, o) in state)
            _store_heads(o_ref, outs, lambda h: h // group, first_pair=done_pairs)

    small = bound_ref[0] <= SAFE_LOGIT_BOUND
    pl.when(small)(functools.partial(attend, True))
    pl.when(jnp.logical_not(small))(functools.partial(attend, False))


def _global_attention(logit_bound, q_all, k_all, v_all, batch, seq, ctx_len, *, latent):
    n_q = q_all.shape[1]
    n_kv = k_all.shape[1]
    n_heads = n_q // HEAD_DIM
    group = n_heads // (n_kv // HEAD_DIM)
    ctx_blk0 = (batch * seq) // ctx_len
    if latent:
        tq = ATTN_Q_TILE
        q_tiles = seq // tq
        q_idx = lambda b, i: (b * q_tiles + i, 0)
        out_rows = batch * seq
    else:
        tq = ctx_len
        q_tiles = 1
        q_idx = lambda b, i: (ctx_blk0 + b, 0)
        out_rows = batch * ctx_len
    in_specs = [
        pl.BlockSpec(memory_space=pltpu.SMEM),
        pl.BlockSpec((tq, n_q), q_idx),
        pl.BlockSpec((ctx_len, n_kv), lambda b, i: (ctx_blk0 + b, 0)),
        pl.BlockSpec((ctx_len, n_kv), lambda b, i: (ctx_blk0 + b, 0)),
    ]
    args = [logit_bound, q_all, k_all, v_all]
    if latent:
        in_specs += [pl.BlockSpec((seq, n_kv), lambda b, i: (b, 0)),
                     pl.BlockSpec((seq, n_kv), lambda b, i: (b, 0))]
        args += [k_all, v_all]
    return pl.pallas_call(
        functools.partial(_global_attn_kernel, n_heads=n_heads, group=group,
                          n_lat_chunks=(seq // ATTN_KV_TILE) if latent else 0),
        grid=(batch, q_tiles),
        in_specs=in_specs,
        out_specs=pl.BlockSpec((tq, n_q), lambda b, i: (b * q_tiles + i, 0)),
        out_shape=jax.ShapeDtypeStruct((out_rows, n_q), BF16),
        compiler_params=_cparams(("parallel", "arbitrary")),
        name="global_attention" if latent else "context_attention",
    )(*args)


def _window_attn_kernel(sink_ref, q_ref, kc_ref, vc_ref, kl_ref, vl_ref, o_ref, *, n_heads, group, seq):
    tq = WIN_Q_TILE
    band = WIN_BAND
    start = pl.program_id(1) * tq
    k0 = jnp.clip(start - WINDOW, 0, seq - band)
    k0 = pl.multiple_of(k0, WINDOW)
    rows = pl.ds(k0, band)
    qpos = start + lax.broadcasted_iota(jnp.int32, (tq, band), 0)
    kpos = k0 + lax.broadcasted_iota(jnp.int32, (tq, band), 1)
    valid = jnp.abs(qpos - kpos) <= WINDOW
    outs = []
    for h in range(n_heads):
        j = h // group
        cb = slice((j // 2) * LANES, (j // 2 + 1) * LANES)
        qz = _head_query(q_ref, h, j)
        s_c = _dot_nt(qz, kc_ref[:, cb])
        s_b = jnp.where(valid, _dot_nt(qz, kl_ref[rows, cb]), MASK_VALUE)
        sink = sink_ref[h]
        m = jnp.maximum(jnp.maximum(jnp.max(s_c, axis=-1, keepdims=True),
                                    jnp.max(s_b, axis=-1, keepdims=True)), sink)
        e_c = jnp.exp(s_c - m)
        e_b = jnp.exp(s_b - m)
        denom = (jnp.sum(e_c, axis=-1, keepdims=True) + jnp.sum(e_b, axis=-1, keepdims=True)
                 + jnp.exp(sink - m))
        o = _dot(e_c.astype(BF16), vc_ref[:, cb]) + _dot(e_b.astype(BF16), vl_ref[rows, cb])
        outs.append(o / denom)
    _store_heads(o_ref, outs, lambda h: h // group)


def _window_attention(q_all, k_all, v_all, sink, batch, seq, ctx_len):
    n_q = q_all.shape[1]
    n_kv = k_all.shape[1]
    n_heads = n_q // HEAD_DIM
    group = n_heads // (n_kv // HEAD_DIM)
    tq = WIN_Q_TILE
    q_tiles = seq // tq
    ctx_blk0 = (batch * seq) // ctx_len
    return pl.pallas_call(
        functools.partial(_window_attn_kernel, n_heads=n_heads, group=group, seq=seq),
        grid=(batch, q_tiles),
        in_specs=[
            pl.BlockSpec(memory_space=pltpu.SMEM),
            pl.BlockSpec((tq, n_q), lambda b, i: (b * q_tiles + i, 0)),
            pl.BlockSpec((ctx_len, n_kv), lambda b, i: (ctx_blk0 + b, 0)),
            pl.BlockSpec((ctx_len, n_kv), lambda b, i: (ctx_blk0 + b, 0)),
            pl.BlockSpec((seq, n_kv), lambda b, i: (b, 0)),
            pl.BlockSpec((seq, n_kv), lambda b, i: (b, 0)),
        ],
        out_specs=pl.BlockSpec((tq, n_q), lambda b, i: (b * q_tiles + i, 0)),
        out_shape=jax.ShapeDtypeStruct((batch * seq, n_q), BF16),
        compiler_params=_cparams(("parallel", "arbitrary")),
        name="window_attention",
    )(sink, q_all, k_all, v_all, k_all, v_all)


def _pool_kernel(u_ref, w_ref, sc_ref, o_ref, pad_ref, *, length):
    halo = POOL_HALO
    ch = POOL_CHUNK
    zeros = jnp.zeros((halo, POOL_DIM), F32)
    pad_ref[0:halo, :] = zeros
    pad_ref[halo + length:halo + length + halo, :] = zeros
    pad_ref[halo:halo + length, :] = u_ref[...]
    lane = lax.broadcasted_iota(jnp.int32, (ch, POOL_DIM), 1)
    grp = lane // POOL_GROUP_DIM
    half_w = jnp.left_shift(1, grp)
    row0 = lax.broadcasted_iota(jnp.int32, (ch, POOL_DIM), 0)
    win_rows = ch + 2 * halo

    def chunk(c, carry):
        r0 = pl.multiple_of(c * ch, ch)
        win = pad_ref[pl.ds(r0, win_rows), :]
        acc = jnp.zeros((ch, POOL_DIM), F32)
        for dlt in range(-halo, halo):
            shifted = pltpu.roll(win, win_rows - (halo + dlt), 0)[:ch] if halo + dlt else win[:ch]
            inside = (half_w >= -dlt) if dlt < 0 else (half_w > dlt)
            acc = acc + jnp.where(inside, shifted, 0.0)
        t = row0 + r0
        hi = jnp.minimum(t + half_w, length)
        lo = jnp.maximum(t - half_w, 0)
        centre = win[halo:halo + ch]
        pooled = acc / (hi - lo).astype(F32) - centre
        mixed = _dot(pooled.astype(BF16), w_ref[...]) * sc_ref[...]
        o_ref[pl.ds(r0, ch), :] = mixed.astype(o_ref.dtype)
        return carry

    lax.fori_loop(0, length // ch, chunk, 0)


def _pool_mixer(u_all, w_blockdiag, scale, row0, n_seq, length):
    blk0 = row0 // length
    return pl.pallas_call(
        functools.partial(_pool_kernel, length=length),
        grid=(n_seq,),
        in_specs=[
            pl.BlockSpec((length, POOL_DIM), lambda s: (blk0 + s, 0)),
            pl.BlockSpec((POOL_DIM, POOL_DIM), lambda s: (0, 0)),
            pl.BlockSpec((1, POOL_DIM), lambda s: (0, 0)),
        ],
        out_specs=pl.BlockSpec((length, POOL_DIM), lambda s: (s, 0)),
        out_shape=jax.ShapeDtypeStruct((n_seq * length, POOL_DIM), BF16),
        scratch_shapes=[pltpu.VMEM((length + 2 * POOL_HALO, POOL_DIM), F32)],
        compiler_params=_cparams(("parallel",)),
        name="pool_mixer",
    )(u_all, w_blockdiag, scale)


def _outproj_kernel(x_ref, mod_ref, *rest, n_pool):
    if n_pool:
        m_ref, a_ref, w_ref, o_ref = rest
        y = _dot(m_ref[...], w_ref[:n_pool, :]) + _dot(a_ref[...], w_ref[n_pool:, :])
    else:
        a_ref, w_ref, o_ref = rest
        y = _dot(a_ref[...], w_ref[...])
    o_ref[...] = x_ref[...] + mod_ref[0, 2:3, :] * y


def _out_projection(x_all, mod, mixed, attn, w_bf16, seq, n_groups, n_rows):
    d = x_all.shape[1]
    tm = ROW_TILE
    n_pool = 0 if mixed is None else mixed.shape[1]
    n_att = attn.shape[1]

    def grp(i):
        return (jnp.minimum((i * tm) // seq, n_groups), 0, 0)

    in_specs = [pl.BlockSpec((tm, d), lambda i: (i, 0)), pl.BlockSpec((1, 6, d), grp)]
    args = [x_all, mod]
    if n_pool:
        in_specs.append(pl.BlockSpec((tm, n_pool), lambda i: (i, 0)))
        args.append(mixed)
    in_specs += [pl.BlockSpec((tm, n_att), lambda i: (i, 0)),
                 pl.BlockSpec((n_pool + n_att, d), lambda i: (0, 0))]
    args += [attn, w_bf16]
    return pl.pallas_call(
        functools.partial(_outproj_kernel, n_pool=n_pool),
        grid=(n_rows // tm,),
        in_specs=in_specs,
        out_specs=pl.BlockSpec((tm, d), lambda i: (i, 0)),
        out_shape=jax.ShapeDtypeStruct((n_rows, d), F32),
        compiler_params=_cparams(("parallel",)),
        name="out_projection",
    )(*args)


def _oddeven_merge(lo, hi, r):
    step = r * 2
    if step < hi - lo:
        yield from _oddeven_merge(lo, hi, step)
        yield from _oddeven_merge(lo + r, hi, step)
        yield from [(i, i + r) for i in range(lo + r, hi - r, step)]
    else:
        yield (lo, lo + r)


def _oddeven_sort(lo, hi):
    if hi - lo >= 1:
        mid = lo + (hi - lo) // 2
        yield from _oddeven_sort(lo, mid)
        yield from _oddeven_sort(mid + 1, hi)
        yield from _oddeven_merge(lo, hi, 1)


_SORT16 = tuple(_oddeven_sort(0, 15))
_SORT8 = tuple(_oddeven_sort(0, 7))
_TOP_PAIRS = tuple((r, c) for r in range(PEER_TOPK) for c in range(PEER_TOPK) if (r + 1) * (c + 1) <= PEER_TOPK)


def _cmpx(xs, i, j):
    a, b = xs[i], xs[j]
    xs[i] = jnp.maximum(a, b)
    xs[j] = jnp.minimum(a, b)


def _sort_desc(xs, net):
    xs = list(xs)
    for i, j in net:
        _cmpx(xs, i, j)
    return xs


def _bitonic_merge_desc(xs):
    xs = list(xs)
    n = len(xs)
    d = n // 2
    while d:
        for i in range(n):
            if not i & d:
                _cmpx(xs, i, i + d)
        d //= 2
    return xs


def _merge_top(a, b):
    n = len(a)
    return _bitonic_merge_desc([jnp.maximum(a[i], b[n - 1 - i]) for i in range(n)])


def _top16_sorted(rows):
    w = _sort_desc(rows, _SORT16)
    for shift in (4, 2, 1):
        w = _merge_top(w, [pltpu.roll(x, shift, 0) for x in w])
    return w


def _pair_threshold(pair):
    low = jnp.full_like(pair[(0, 0)], LOWEST)
    row0 = [pair[(0, c)] for c in range(16)]
    col0 = [pair[(r, 0)] for r in range(1, 16)] + [low]
    t01 = _merge_top(row0, col0)
    g2 = _bitonic_merge_desc([pair[(1, c)] for c in range(1, 8)] + [low] * 3
                             + [pair[(r, 1)] for r in range(7, 1, -1)])
    rest = [pair[k] for k in ((2, 2), (2, 3), (2, 4), (3, 2), (4, 2), (3, 3))]
    g3 = _sort_desc(rest + [low, low], _SORT8) + [low] * 8
    t23 = _merge_top(g2, g3)
    tau = None
    for i in range(16):
        m = jnp.maximum(t01[i], t23[15 - i])
        tau = m if tau is None else jnp.minimum(tau, m)
    return tau


def _route_tile(s1, s2):
    assert PEER_TOPK == 16 and PEER_NKEYS == 128
    n_col = s1.shape[1] // LANES
    assert 8 % n_col == 0
    span = 8 // n_col
    cols = []
    for j in range(n_col):
        c1 = s1[:, j * LANES:(j + 1) * LANES]
        c2 = s2[:, j * LANES:(j + 1) * LANES]
        rows1 = [c1[8 * g:8 * g + 8] for g in range(16)]
        rows2 = [c2[8 * g:8 * g + 8] for g in range(16)]
        cols.append((rows1, rows2, _top16_sorted(rows1), _top16_sorted(rows2)))

    sub = lax.broadcasted_iota(jnp.int32, (8, LANES), 0)

    def pack(vals):
        out = vals[-1]
        for j in range(n_col - 2, -1, -1):
            out = jnp.where(sub < (j + 1) * span, vals[j], out)
        return out

    def unpack(x, j):
        return jnp.broadcast_to(x[j * span:j * span + 1, :], (8, LANES))

    v1 = [pack([c[2][r] for c in cols]) for r in range(16)]
    v2 = [pack([c[3][r] for c in cols]) for r in range(16)]
    pair = {(r, c): v1[r] + v2[c] for (r, c) in _TOP_PAIRS}
    tau = _pair_threshold(pair)
    e1 = [jnp.exp(v - v1[0]) for v in v1]
    e2 = [jnp.exp(v - v2[0]) for v in v2]
    z = None
    for (r, c) in _TOP_PAIRS:
        term = jnp.where(pair[(r, c)] >= tau, e1[r] * e2[c], 0.0)
        z = term if z is None else z + term
    inv_z = (1.0 / math.sqrt(2.0)) / z
    n_top = jnp.zeros_like(tau)
    for c in range(16):
        n_top = jnp.where(pair[(0, c)] >= tau, c + 1.0, n_top)

    out = [[], [], [], []]
    for j, (rows1, rows2, t1, t2) in enumerate(cols):
        tau_j, inv_z_j, n_top_j = unpack(tau, j), unpack(inv_z, j), unpack(n_top, j)
        cnt, p1n, rank2, p2 = [], [], [], []
        for g in range(16):
            a = rows1[g]
            n = jnp.zeros_like(a)
            for c in range(8):
                n = jnp.where(a + t2[c] >= tau_j, c + 1.0, n)
            cnt.append(jnp.where(a >= t1[0], n_top_j, n))
            p1n.append(jnp.exp(a - t1[0]) * inv_z_j)
            b = rows2[g]
            k = jnp.zeros_like(b)
            for r in range(16):
                k = jnp.where(t2[r] > b, r + 1.0, k)
            rank2.append(k)
            p2.append(jnp.exp(b - t2[0]))
        for dst, parts in zip(out, (cnt, p1n, rank2, p2)):
            dst.append(jnp.concatenate(parts, axis=0))
    return tuple(jnp.concatenate(parts, axis=1) for parts in out)


def _peer_kernel(x_ref, mod_ref, g_ref, wq_ref, k1_ref, k2_ref, ut_ref, vt_ref, *rest, final_norm, n_blocks):
    if final_norm:
        fg_ref, o_ref, *scratch = rest
    else:
        o_ref, *scratch = rest
    hb_ref, cnt_ref, p1_ref, rank_ref, p2_ref, gw0_ref, gw1_ref, acc_ref = scratch
    s = pl.program_id(1)
    eb, t = gw0_ref.shape
    blocks = eb // PEER_NKEYS
    half = PEER_DKEY // 2

    @pl.when(s == 0)
    def _route():
        x = x_ref[...]
        h2 = _norm_mod(x, g_ref[...], mod_ref[0, 3:4, :], mod_ref[0, 4:5, :])
        hb_ref[...] = h2.astype(BF16)
        acc_ref[...] = jnp.zeros_like(acc_ref)

        def head(h, carry):
            qh = _dot(hb_ref[...], wq_ref[h])
            s1 = _dot_nt(k1_ref[...], qh[:, :half].astype(BF16))
            s2 = _dot_nt(k2_ref[...], qh[:, half:].astype(BF16))
            cnt, p1n, rank2, p2 = _route_tile(s1, s2)
            cnt_ref[h] = cnt
            p1_ref[h] = p1n
            rank_ref[h] = rank2.astype(BF16)
            p2_ref[h] = p2.astype(BF16)
            return carry

        lax.fori_loop(0, PEER_HEADS, head, 0)

    def stage(parity, build, consume):
        gw_w, gw_r = (gw0_ref, gw1_ref) if parity == 0 else (gw1_ref, gw0_ref)
        reps = PEER_NKEYS // 16
        span = 2 * PEER_NKEYS
        n_pairs = blocks // 2

        def build_pair(pair):
            y_pair = _dot(hb_ref[...], ut_ref[:, pair * span:(pair + 1) * span]).T
            for i in (2 * pair, 2 * pair + 1):
                a = s * blocks + i
                w = None
                for h in range(PEER_HEADS):
                    cnt16 = jnp.broadcast_to(cnt_ref[h, pl.ds(a, 1), :], (16, t)).astype(BF16)
                    p16 = jnp.broadcast_to(p1_ref[h, pl.ds(a, 1), :], (16, t)).astype(BF16)
                    sel = rank_ref[h] < jnp.concatenate([cnt16] * reps, axis=0)
                    term = (jnp.where(sel, p2_ref[h], jnp.zeros((), BF16))
                            * jnp.concatenate([p16] * reps, axis=0))
                    w = term if w is None else w + term
                y = y_pair[(i % 2) * PEER_NKEYS:(i % 2 + 1) * PEER_NKEYS, :]
                act = y * (1.0 + lax.erf(y))
                gw_w[i * PEER_NKEYS:(i + 1) * PEER_NKEYS, :] = act.astype(BF16) * w

        if build:
            for k in range(n_pairs):
                build_pair(k)
        if consume:
            acc_ref[...] += _dot(vt_ref[...], gw_r[...])

    pl.when(s == 0)(functools.partial(stage, 0, True, False))
    steady = (s >= 1) & (s < n_blocks)
    pl.when(steady & (s % 2 == 0))(functools.partial(stage, 0, True, True))
    pl.when(steady & (s % 2 == 1))(functools.partial(stage, 1, True, True))
    pl.when(s == n_blocks)(functools.partial(stage, n_blocks % 2, False, True))

    @pl.when(s == n_blocks)
    def _finish():
        y = x_ref[...] + mod_ref[0, 5:6, :] * acc_ref[...].T
        if final_norm:
            y = y * lax.rsqrt(jnp.mean(y * y, axis=-1, keepdims=True) + EPS) * fg_ref[...]
        o_ref[...] = y


def _peer(x_all, mod, gain, wq_heads, keys1, keys2, ut_bf16, vt_bf16, seq, n_groups, n_rows, final_gain):
    d = x_all.shape[1]
    t = PEER_TOKENS
    eb = PEER_EXPERT_BLOCK
    n_blocks = ut_bf16.shape[1] // eb

    def grp(i, e):
        return (jnp.minimum((i * t) // seq, n_groups), 0, 0)

    in_specs = [
        pl.BlockSpec((t, d), lambda i, e: (i, 0)),
        pl.BlockSpec((1, 6, d), grp),
        pl.BlockSpec((1, d), lambda i, e: (0, 0)),
        pl.BlockSpec(wq_heads.shape, lambda i, e: (0, 0, 0)),
        pl.BlockSpec(keys1.shape, lambda i, e: (0, 0)),
        pl.BlockSpec(keys2.shape, lambda i, e: (0, 0)),
        pl.BlockSpec((d, eb), lambda i, e: (0, jnp.minimum(e, n_blocks - 1))),
        pl.BlockSpec((d, eb), lambda i, e: (0, jnp.maximum(e - 1, 0))),
    ]
    args = [x_all, mod, gain, wq_heads, keys1, keys2, ut_bf16, vt_bf16]
    if final_gain is not None:
        in_specs.append(pl.BlockSpec((1, d), lambda i, e: (0, 0)))
        args.append(final_gain)
    per_head = (PEER_HEADS, PEER_NKEYS, t)
    return pl.pallas_call(
        functools.partial(_peer_kernel, final_norm=final_gain is not None, n_blocks=n_blocks),
        grid=(n_rows // t, n_blocks + 1),
        in_specs=in_specs,
        out_specs=pl.BlockSpec((t, d), lambda i, e: (i, 0)),
        out_shape=jax.ShapeDtypeStruct((n_rows, d), F32),
        scratch_shapes=[
            pltpu.VMEM((t, d), BF16),
            pltpu.VMEM(per_head, F32),
            pltpu.VMEM(per_head, F32),
            pltpu.VMEM(per_head, BF16),
            pltpu.VMEM(per_head, BF16),
            pltpu.VMEM((eb, t), BF16),
            pltpu.VMEM((eb, t), BF16),
            pltpu.VMEM((d, t), F32),
        ],
        compiler_params=_cparams(("parallel", "arbitrary")),
        name="peer",
    )(*args)


def _rope_tables(seq, pad_rows):
    half = HEAD_DIM // 2
    t = jnp.arange(seq, dtype=jnp.int32)
    row = (t // GRID_W).astype(F32)
    col = (t % GRID_W).astype(F32)
    inv = 1.0 / (ROPE_THETA ** (jnp.arange(0, half, 2, dtype=F32) / half))
    ang_r = row[:, None] * inv[None, :]
    ang_c = col[:, None] * inv[None, :]
    cos_h = jnp.concatenate([jnp.cos(ang_r)] * 2 + [jnp.cos(ang_c)] * 2, axis=1)
    sin_h = jnp.concatenate([-jnp.sin(ang_r), jnp.sin(ang_r), -jnp.sin(ang_c), jnp.sin(ang_c)], axis=1)
    cos_t = jnp.concatenate([jnp.tile(cos_h, (1, LANES // HEAD_DIM)), jnp.ones((pad_rows, LANES), F32)], axis=0)
    sin_t = jnp.concatenate([jnp.tile(sin_h, (1, LANES // HEAD_DIM)), jnp.zeros((pad_rows, LANES), F32)], axis=0)
    return cos_t, sin_t


def _head_norm_consts(q_gain, k_gain, n_q_heads, n_kv_heads):
    n_heads = n_q_heads + n_kv_heads
    head_of_lane = np.arange(n_heads * HEAD_DIM) // HEAD_DIM
    onehot = (head_of_lane[:, None] == np.arange(LANES)[None, :]).astype(np.float32)
    gsum = jnp.asarray(onehot / HEAD_DIM, BF16)
    gexp = jnp.asarray(onehot.T, BF16)
    hg = jnp.concatenate([jnp.tile(q_gain, n_q_heads), jnp.tile(k_gain, n_kv_heads)])[None, :]
    return gsum, gexp, hg


def kernel(x, c, ctx, c_ctx, ada_w, ada_b, norm1_g, norm2_g, final_g, e_w_in, e_q_norm_g, e_k_norm_g,
           e_pool_w, e_pool_scale, e_w_out, o_w_in, o_sink, o_w_out, p_w_q, p_keys1, p_keys2, p_u, p_v):
    batch, seq, d = x.shape
    ctx_len = ctx.shape[1]
    depth = ada_w.shape[0]
    n_lat = batch * seq
    n_all = n_lat + batch * ctx_len

    x_all = jnp.concatenate([x.reshape(n_lat, d), ctx.reshape(batch * ctx_len, d)], axis=0)
    cond = jnp.concatenate([c, c_ctx[None, :], jnp.zeros((8 - (batch + 1) % 8, d), F32)], axis=0)
    mods = _mod_vectors(cond, ada_w, ada_b)
    mods = mods.reshape(depth, cond.shape[0], 6, d)
    cos_t, sin_t = _rope_tables(seq, ROW_TILE)

    for i in range(depth):
        last = i == depth - 1
        mod = mods[i]
        j = i // 2
        g1 = norm1_g[i][None, :]
        g2 = norm2_g[i][None, :]
        if i % 2 == 0:
            consts = _head_norm_consts(e_q_norm_g[j], e_k_norm_g[j], B_Q_HEADS, B_KV_HEADS)
            u, q, k, v = _in_projection(
                x_all, mod, g1, e_w_in[j].astype(BF16), cos_t, sin_t, seq, batch,
                n_pre=POOL_DIM, n_q=B_Q_HEADS * HEAD_DIM, n_kv=B_KV_HEADS * HEAD_DIM, norm_consts=consts)
            w_pool = jax.scipy.linalg.block_diag(*[e_pool_w[j, g] for g in range(POOL_GROUPS)]).astype(BF16)
            p_scale = e_pool_scale[j][None, :]
            logit_bound = (NORM_BOUND_MARGIN * math.sqrt(HEAD_DIM) * jnp.max(jnp.abs(e_q_norm_g[j]))
                           * jnp.max(jnp.abs(e_k_norm_g[j]))).reshape(1)
            attn = _global_attention(logit_bound, q, k, v, batch, seq, ctx_len, latent=True)
            mixed = _pool_mixer(u, w_pool, p_scale, 0, batch, seq)
            if not last:
                attn = jnp.concatenate(
                    [attn, _global_attention(logit_bound, q, k, v, batch, seq, ctx_len, latent=False)], axis=0)
                mixed = jnp.concatenate([mixed, _pool_mixer(u, w_pool, p_scale, n_lat, batch, ctx_len)], axis=0)
            w_out = e_w_out[j].astype(BF16)
        else:
            _q = C_Q_HEADS * HEAD_DIM
            q, k, v = _in_projection(
                x_all, mod, g1, o_w_in[j].astype(BF16), cos_t, sin_t, seq, batch,
                n_pre=0, n_q=_q, n_kv=C_KV_HEADS * HEAD_DIM, norm_consts=None)
            attn = _window_attention(q, k, v, o_sink[j], batch, seq, ctx_len)
            if not last:
                raise NotImplementedError("context stream after an odd layer")
            mixed = None
            w_out = o_w_out[j].astype(BF16)
        n_rows = n_lat if last else n_all
        x_mid = _out_projection(x_all, mod, mixed, attn, w_out, seq, batch, n_rows)
        wq_heads = p_w_q[i].astype(BF16).reshape(d, PEER_HEADS, PEER_DKEY).transpose(1, 0, 2)
        x_all = _peer(x_mid, mod, g2, wq_heads, p_keys1[i].astype(BF16), p_keys2[i].astype(BF16),
                      (p_u[i] * (1.0 / math.sqrt(2.0))).astype(BF16).T, p_v[i].astype(BF16).T, seq, batch, n_rows,
                      final_g[None, :] if last else None)
    return x_all[:n_lat].reshape(batch, seq, d)
```
